```python
import math
import jax, jax.numpy as jnp
from jax import lax
import numpy as np

D_MODEL = 2048
BATCH = 2
SEQ = 4096
DEPTH = 1
DEC_BATCH = 32
DEC_SEQ = 8
PAST_LEN = 8192
PAGE_SIZE = 128

N_HEADS = 8
HEAD_DIM = 128
ATTN_WIDTH = N_HEADS * HEAD_DIM
MOBA_BLOCK = 256
MOBA_TOPK = 3
ROPE_THETA = 10000.0
POOL_WINDOWS = (2, 4, 8, 16)
POOL_WIDTH = D_MODEL // 2
POOL_GROUP = POOL_WIDTH // len(POOL_WINDOWS)
POOL_STATE = max(POOL_WINDOWS) - 1
N_BRANCHES = 2
IN_WIDTH = 3 * ATTN_WIDTH + POOL_WIDTH + N_BRANCHES * D_MODEL
FFN_HIDDEN = ((8 * D_MODEL + 3 * 256 - 1) // (3 * 256)) * 256
QUERY_BLOCK = 32
RMS_EPS = 1e-6

kernel_name = "moba_pool_gated_hybrid_step"


def rms_norm(x, g):
    xf = x.astype(jnp.float32)
    y = xf * lax.rsqrt(jnp.mean(xf * xf, axis=-1, keepdims=True) + RMS_EPS)
    return (y * g.astype(jnp.float32)).astype(x.dtype)


def rope(x, pos):
    half = HEAD_DIM // 2
    inv = ROPE_THETA ** (-jnp.arange(half, dtype=jnp.float32) / half)
    ang = pos.astype(jnp.float32)[:, None] * inv[None, :]
    cos = jnp.cos(ang)[None, :, None, :]
    sin = jnp.sin(ang)[None, :, None, :]
    xf = x.astype(jnp.float32)
    x1, x2 = xf[..., :half], xf[..., half:]
    return jnp.concatenate([x1 * cos - x2 * sin, x2 * cos + x1 * sin], axis=-1).astype(x.dtype)


def moba_attention(q, k_all, v_all, pos0):
    B, Sq, H, hd = q.shape
    Sk = k_all.shape[1]
    nb = Sk // MOBA_BLOCK
    kb = k_all.reshape(B, nb, MOBA_BLOCK, H, hd)
    vb = v_all.reshape(B, nb, MOBA_BLOCK, H, hd)
    kmean = jnp.mean(kb.astype(jnp.float32), axis=2)
    qc = math.gcd(Sq, QUERY_BLOCK)
    n_chunks = Sq // qc
    topk = min(MOBA_TOPK, nb)
    qch = q.reshape(B, n_chunks, qc, H, hd)
    scale = HEAD_DIM ** -0.5
    h_idx = jnp.arange(H)[None, :, None]
    blk = jnp.arange(nb)
    offs = jnp.arange(MOBA_BLOCK)

    def attend_chunk(ids):
        b, c = ids
        qq = qch[b, c]
        pos = pos0 + c * qc + jnp.arange(qc)
        own = pos // MOBA_BLOCK
        gate = jnp.einsum('qhd,nhd->qhn', qq.astype(jnp.float32), kmean[b])
        gate = jnp.where(blk[None, None, :] < own[:, None, None], gate, -jnp.inf)
        top_val, top_idx = lax.top_k(gate, topk)
        own_b = jnp.broadcast_to(own[:, None, None], (qc, H, 1))
        idx = jnp.concatenate([top_idx, own_b], axis=-1)
        valid = jnp.concatenate([jnp.isfinite(top_val), jnp.ones((qc, H, 1), bool)], axis=-1)
        ksel = kb[b][idx, :, h_idx, :]
        vsel = vb[b][idx, :, h_idx, :]
        s = jnp.einsum('qhd,qhjkd->qhjk', qq, ksel, preferred_element_type=jnp.float32) * scale
        kpos = idx[..., None] * MOBA_BLOCK + offs
        mask = valid[..., None] & (kpos <= pos[:, None, None, None])
        s = jnp.where(mask, s, -jnp.inf)
        p = jax.nn.softmax(s.reshape(qc, H, -1), axis=-1).reshape(s.shape)
        o = jnp.einsum('qhjk,qhjkd->qhd', p.astype(vsel.dtype), vsel,
                       preferred_element_type=jnp.float32)
        return o.astype(q.dtype)

    b_ids = jnp.repeat(jnp.arange(B), n_chunks)
    c_ids = jnp.tile(jnp.arange(n_chunks), B)
    out = lax.map(attend_chunk, (b_ids, c_ids))
    return out.reshape(B, Sq, H, hd)


def multiscale_pool(u_ext, n_pre, pos0, w_pool, ls_pool):
    B, S_ext, _ = u_ext.shape
    S = S_ext - n_pre
    csum = jnp.cumsum(u_ext.astype(jnp.float32), axis=1)
    csum = jnp.concatenate([jnp.zeros((B, 1, POOL_WIDTH), jnp.float32), csum], axis=1)
    i = n_pre + np.arange(S)
    abs_pos = pos0 + np.arange(S)
    u_new = u_ext[:, n_pre:].astype(jnp.float32)
    outs = []
    for g, w in enumerate(POOL_WINDOWS):
        sl = slice(g * POOL_GROUP, (g + 1) * POOL_GROUP)
        cg = csum[:, :, sl]
        start = np.maximum(i + 1 - w, 0)
        cnt = jnp.asarray(np.minimum(w, abs_pos + 1).astype(np.float32))
        mean = (cg[:, i + 1] - cg[:, start]) / cnt[None, :, None]
        outs.append(mean - u_new[:, :, sl])
    y = jnp.stack(outs, axis=2)
    y = jnp.einsum('bsgc,gcd->bsgd', y.astype(u_ext.dtype), w_pool,
                   preferred_element_type=jnp.float32)
    y = y.reshape(B, S, POOL_WIDTH) * ls_pool.astype(jnp.float32)
    return y.astype(u_ext.dtype)


def trunk_layer(x, c, pos0, past_k, past_v, past_u,
                w_ada, b_ada, g_norm1, g_norm2, w_in, g_qnorm, g_knorm, w_pool, ls_pool,
                w_o_attn, w_o_pool, w_out, w_ffn_in, w_ffn_out):
    B, S, _ = x.shape
    mod = (c @ w_ada + b_ada)[:, None, :]
    sh1, sc1, gt1, sh2, sc2, gt2 = jnp.split(mod, 6, axis=-1)

    h = rms_norm(x, g_norm1) * (1 + sc1) + sh1
    proj = h @ w_in
    q, k, v, u, glog = jnp.split(
        proj, [ATTN_WIDTH, 2 * ATTN_WIDTH, 3 * ATTN_WIDTH, 3 * ATTN_WIDTH + POOL_WIDTH], axis=-1)
    pos = pos0 + jnp.arange(S)
    q = rope(rms_norm(q.reshape(B, S, N_HEADS, HEAD_DIM), g_qnorm), pos)
    k = rope(rms_norm(k.reshape(B, S, N_HEADS, HEAD_DIM), g_knorm), pos)
    v = v.reshape(B, S, N_HEADS, HEAD_DIM)

    if past_k is None:
        k_all, v_all, u_ext, n_pre = k, v, u, 0
    else:
        k_all = jnp.concatenate([past_k.astype(k.dtype), k], axis=1)
        v_all = jnp.concatenate([past_v.astype(v.dtype), v], axis=1)
        u_ext = jnp.concatenate([past_u.astype(u.dtype), u], axis=1)
        n_pre = past_u.shape[1]
    pad = (-k_all.shape[1]) % MOBA_BLOCK
    k_pad = jnp.pad(k_all, ((0, 0), (0, pad), (0, 0), (0, 0)))
    v_pad = jnp.pad(v_all, ((0, 0), (0, pad), (0, 0), (0, 0)))

    attn = moba_attention(q, k_pad, v_pad, pos0).reshape(B, S, ATTN_WIDTH)
    pooled = multiscale_pool(u_ext, n_pre, pos0, w_pool, ls_pool)
    y_a = attn @ w_o_attn
    y_b = pooled @ w_o_pool
    g_a, g_b = jnp.split(jax.nn.sigmoid(glog), N_BRANCHES, axis=-1)
    mix = (g_a * y_a + g_b * y_b) @ w_out
    x = x + gt1 * mix

    h2 = rms_norm(x, g_norm2) * (1 + sc2) + sh2
    a, b = jnp.split(h2 @ w_ffn_in, 2, axis=-1)
    x = x + gt2 * ((jax.nn.silu(a) * b) @ w_ffn_out)
    return x, k, v, u_ext[:, -POOL_STATE:]


def setup_inputs(seed: int = 0) -> dict:
    key = jax.random.key(seed)
    ks = jax.random.split(key, 24)
    f32 = jnp.float32

    def nrm(k, shape, s):
        return jax.random.normal(k, shape, f32) * s

    n_pages = PAST_LEN // PAGE_SIZE
    n_used = DEC_BATCH * n_pages
    n_phys = n_used + n_used // 4
    page_table = jax.random.permutation(ks[0], n_phys)[:n_used].astype(jnp.int32).reshape(DEC_BATCH, n_pages)
    return {
        "x_prompt": nrm(ks[1], (BATCH, SEQ, D_MODEL), 1.0),
        "x_sample": nrm(ks[2], (DEC_BATCH, DEC_SEQ, D_MODEL), 1.0),
        "c_prompt": nrm(ks[3], (BATCH, D_MODEL), 1.0),
        "c_sample": nrm(ks[4], (DEC_BATCH, D_MODEL), 1.0),
        "cache_k": nrm(ks[5], (n_phys, PAGE_SIZE, N_HEADS, HEAD_DIM), 1.0),
        "cache_v": nrm(ks[6], (n_phys, PAGE_SIZE, N_HEADS, HEAD_DIM), 1.0),
        "state_pool": nrm(ks[7], (DEC_BATCH, POOL_STATE, POOL_WIDTH), 1.0),
        "page_table": page_table,
        "w_ada": nrm(ks[8], (D_MODEL, 6 * D_MODEL), 0.3 * D_MODEL ** -0.5),
        "b_ada": nrm(ks[9], (6 * D_MODEL,), 0.01),
        "g_norm1": 1.0 + nrm(ks[10], (D_MODEL,), 0.02),
        "g_norm2": 1.0 + nrm(ks[11], (D_MODEL,), 0.02),
        "w_in": nrm(ks[12], (D_MODEL, IN_WIDTH), D_MODEL ** -0.5),
        "g_qnorm": 1.0 + nrm(ks[13], (HEAD_DIM,), 0.02),
        "g_knorm": 1.0 + nrm(ks[14], (HEAD_DIM,), 0.02),
        "w_pool": nrm(ks[15], (len(POOL_WINDOWS), POOL_GROUP, POOL_GROUP), POOL_GROUP ** -0.5),
        "ls_pool": 1.0 + nrm(ks[16], (POOL_WIDTH,), 0.05),
        "w_o_attn": nrm(ks[17], (ATTN_WIDTH, D_MODEL), ATTN_WIDTH ** -0.5),
        "w_o_pool": nrm(ks[18], (POOL_WIDTH, D_MODEL), POOL_WIDTH ** -0.5),
        "w_out": nrm(ks[19], (D_MODEL, D_MODEL), D_MODEL ** -0.5),
        "w_ffn_in": nrm(ks[20], (D_MODEL, 2 * FFN_HIDDEN), D_MODEL ** -0.5),
        "w_ffn_out": nrm(ks[21], (FFN_HIDDEN, D_MODEL), FFN_HIDDEN ** -0.5),
    }


def reference(x_prompt, x_sample, c_prompt, c_sample, cache_k, cache_v, state_pool, page_table,
              w_ada, b_ada, g_norm1, g_norm2, w_in, g_qnorm, g_knorm, w_pool, ls_pool,
              w_o_attn, w_o_pool, w_out, w_ffn_in, w_ffn_out):
    past_len = page_table.shape[1] * cache_k.shape[1]
    dec_b = page_table.shape[0]
    past_k = cache_k[page_table].reshape(dec_b, past_len, N_HEADS, HEAD_DIM)
    past_v = cache_v[page_table].reshape(dec_b, past_len, N_HEADS, HEAD_DIM)

    y_p = x_prompt
    y_s = x_sample
    for _ in range(DEPTH):
        y_p, k_prompt, v_prompt, pool_prompt = trunk_layer(
            y_p, c_prompt, 0, None, None, None,
            w_ada, b_ada, g_norm1, g_norm2, w_in, g_qnorm, g_knorm, w_pool, ls_pool,
            w_o_attn, w_o_pool, w_out, w_ffn_in, w_ffn_out)
        y_s, k_sample, v_sample, pool_sample = trunk_layer(
            y_s, c_sample, past_len, past_k, past_v, state_pool,
            w_ada, b_ada, g_norm1, g_norm2, w_in, g_qnorm, g_knorm, w_pool, ls_pool,
            w_o_attn, w_o_pool, w_out, w_ffn_in, w_ffn_out)
    return (y_p, y_s, k_prompt, v_prompt, pool_prompt, k_sample, v_sample, pool_sample)
```

```python
import functools

import jax
import jax.numpy as jnp
from jax import lax
from jax.experimental import pallas as pl
from jax.experimental.pallas import tpu as pltpu

N_HEADS = 8
HEAD_DIM = 128
ATTN_WIDTH = N_HEADS * HEAD_DIM
MOBA_BLOCK = 256
MOBA_TOPK = 3
ROPE_THETA = 10000.0
POOL_WINDOWS = (2, 4, 8, 16)
POOL_HALO = 16
RMS_EPS = 1e-6

F32 = jnp.float32
BF16 = jnp.bfloat16
NEG_INF = float("-inf")

_VMEM_LIMIT = 52 * 1024 * 1024


def _params(*sem):
    return pltpu.CompilerParams(dimension_semantics=sem, vmem_limit_bytes=_VMEM_LIMIT)


def _mod_norm(x, g, sc, sh):
    ms = jnp.mean(x * x, axis=-1, keepdims=True)
    y = x * lax.rsqrt(ms + RMS_EPS) * g
    return y * (1.0 + sc) + sh


def _ada_kernel(c_ref, w_ref, b_ref, o_ref):
    o_ref[...] = jnp.dot(c_ref[...].astype(BF16), w_ref[...].astype(BF16),
                         preferred_element_type=F32) + b_ref[...]


def _ada(c_all, w_ada, b_ada, tn=1024):
    m, d = c_all.shape
    n = w_ada.shape[1]
    return pl.pallas_call(
        _ada_kernel,
        grid=(n // tn,),
        in_specs=[pl.BlockSpec((m, d), lambda j: (0, 0)),
                  pl.BlockSpec((d, tn), lambda j: (0, j)),
                  pl.BlockSpec((1, tn), lambda j: (0, j))],
        out_specs=pl.BlockSpec((m, tn), lambda j: (0, j)),
        out_shape=jax.ShapeDtypeStruct((m, n), F32),
        compiler_params=_params("arbitrary"),
        name="ada_mod",
    )(c_all, w_ada, b_ada.reshape(1, n))


def _head_norm_rope(t, g, cos, sin_signed):
    outs = []
    for h in range(N_HEADS):
        th = t[:, h * HEAD_DIM:(h + 1) * HEAD_DIM]
        ms = jnp.mean(th * th, axis=-1, keepdims=True)
        y = th * lax.rsqrt(ms + RMS_EPS) * g
        outs.append(y * cos + pltpu.roll(y, HEAD_DIM // 2, 1) * sin_signed)
    return jnp.concatenate(outs, axis=-1)


def _qkvu_kernel(x_ref, sh_ref, sc_ref, g1_ref, w_ref, gq_ref, gk_ref, cos_ref, sin_ref,
                 q_ref, k_ref, v_ref, u_ref, h_scr):
    j = pl.program_id(2)

    @pl.when(j == 0)
    def _():
        h = _mod_norm(x_ref[...], g1_ref[...], sc_ref[...], sh_ref[...])
        h_scr[...] = h.reshape(h_scr.shape).astype(BF16)

    acc = jnp.dot(h_scr[...], w_ref[...], preferred_element_type=F32)

    @pl.when(j == 0)
    def _():
        q_ref[...] = _head_norm_rope(acc, gq_ref[...], cos_ref[...], sin_ref[...])

    @pl.when(j == 1)
    def _():
        k_ref[...] = _head_norm_rope(acc, gk_ref[...], cos_ref[...], sin_ref[...])

    @pl.when(j == 2)
    def _():
        v_ref[...] = acc

    @pl.when(j == 3)
    def _():
        u_ref[...] = acc


def _group_specs(x3, gb, r):
    g_total, s, d = x3.shape
    n_g, n_t = g_total // gb, s // r
    return n_g, n_t, d


def _qkvu(x3, mod, g_norm1, w_in_bf, g_q, g_k, cos, sin_signed, gb, r):
    n_g, n_t, d = _group_specs(x3, gb, r)
    tm = gb * r
    tokens = x3.shape[0] * x3.shape[1]
    wq = ATTN_WIDTH
    x_spec = pl.BlockSpec((gb, r, d), lambda g, t, j: (g, t, 0))
    mod_spec = lambda k: pl.BlockSpec((gb, 1, d), lambda g, t, j: (g, 0, k))
    row_spec = pl.BlockSpec((1, d), lambda g, t, j: (0, 0))
    head_spec = pl.BlockSpec((1, HEAD_DIM), lambda g, t, j: (0, 0))
    tab_spec = pl.BlockSpec((tm, HEAD_DIM), lambda g, t, j: (t, 0))
    out_spec = pl.BlockSpec((tm, wq), lambda g, t, j: (g * n_t + t, 0))
    out_sds = jax.ShapeDtypeStruct((tokens, wq), F32)
    return pl.pallas_call(
        _qkvu_kernel,
        grid=(n_g, n_t, 4),
        in_specs=[x_spec, mod_spec(0), mod_spec(1), row_spec,
                  pl.BlockSpec((d, wq), lambda g, t, j: (0, j)),
                  head_spec, head_spec, tab_spec, tab_spec],
        out_specs=[out_spec] * 4,
        out_shape=[out_sds] * 4,
        scratch_shapes=[pltpu.VMEM((tm, d), BF16)],
        compiler_params=_params("arbitrary", "arbitrary", "arbitrary"),
        name="qkvu_proj",
    )(x3, mod, mod, g_norm1.reshape(1, d), w_in_bf, g_q.reshape(1, HEAD_DIM),
      g_k.reshape(1, HEAD_DIM), cos, sin_signed)


def _topk_select(gate, valid, axis):
    n = gate.shape[axis]
    g = jnp.where(valid, gate, NEG_INF)
    idx = lax.broadcasted_iota(jnp.int32, gate.shape, axis)
    rank = jnp.zeros(gate.shape, jnp.int32)
    for m in range(n):
        gm = g[m:m + 1, :] if axis == 0 else g[:, m:m + 1]
        beats = (gm > g) | ((gm == g) & (m < idx))
        rank = rank + beats.astype(jnp.int32)
    return valid & (rank < MOBA_TOPK) & (jnp.abs(g) < float("inf"))


def _moba_prompt_kernel(q_ref, k_ref, v_ref, o_ref, kb_scr, vt_scr, kmean_scr, bias_scr, *, nb):
    qi = pl.program_id(2)
    blk = MOBA_BLOCK

    @pl.when(qi == 0)
    def _():
        for n in range(nb):
            kn = k_ref[n * blk:(n + 1) * blk, :]
            kb_scr[n] = kn.astype(BF16)
            kmean_scr[n:n + 1, :] = jnp.mean(kn, axis=0, keepdims=True)
            vt_scr[n] = v_ref[n * blk:(n + 1) * blk, :].T.astype(BF16)

    q = q_ref[...]
    qb = q.astype(BF16)
    scale = HEAD_DIM ** -0.5
    gate = lax.dot_general(kmean_scr[...], q, (((1,), (1,)), ((), ())),
                           precision=lax.Precision.HIGHEST, preferred_element_type=F32)
    n_idx = lax.broadcasted_iota(jnp.int32, gate.shape, 0)
    sel = _topk_select(gate, n_idx < qi, axis=0)
    bias_scr[...] = jnp.where(sel, 0.0, NEG_INF)

    def scores(j):
        return lax.dot_general(kb_scr[j], qb, (((1,), (1,)), ((), ())),
                               preferred_element_type=F32) * scale

    s = scores(qi)
    kpos = lax.broadcasted_iota(jnp.int32, s.shape, 0)
    qpos = lax.broadcasted_iota(jnp.int32, s.shape, 1)
    s = jnp.where(kpos <= qpos, s, NEG_INF)
    m0 = jnp.max(s, axis=0, keepdims=True)
    p = jnp.exp(s - m0)
    l0 = jnp.sum(p, axis=0, keepdims=True)
    acc0 = jnp.dot(vt_scr[qi], p.astype(BF16), preferred_element_type=F32)

    def body(j, carry):
        m, l, acc = carry
        s = scores(j) + bias_scr[pl.ds(j, 1), :]
        m_new = jnp.maximum(m, jnp.max(s, axis=0, keepdims=True))
        alpha = jnp.exp(m - m_new)
        p = jnp.exp(s - m_new)
        l = l * alpha + jnp.sum(p, axis=0, keepdims=True)
        acc = acc * alpha + jnp.dot(vt_scr[j], p.astype(BF16), preferred_element_type=F32)
        return m_new, l, acc

    _, l, acc = lax.fori_loop(0, qi, body, (m0, l0, acc0))
    o_ref[...] = (acc / l).T.astype(o_ref.dtype)


def _moba_prompt(q, k, v, n_seq, seq):
    nb = seq // MOBA_BLOCK
    blk = MOBA_BLOCK
    kv_spec = pl.BlockSpec((seq, HEAD_DIM), lambda b, h, i: (b, h))
    q_spec = pl.BlockSpec((blk, HEAD_DIM), lambda b, h, i: (b * nb + i, h))
    return pl.pallas_call(
        functools.partial(_moba_prompt_kernel, nb=nb),
        grid=(n_seq, N_HEADS, nb),
        in_specs=[q_spec, kv_spec, kv_spec],
        out_specs=q_spec,
        out_shape=jax.ShapeDtypeStruct(q.shape, BF16),
        scratch_shapes=[pltpu.VMEM((nb, blk, HEAD_DIM), BF16),
                        pltpu.VMEM((nb, HEAD_DIM, blk), BF16),
                        pltpu.VMEM((nb, HEAD_DIM), F32),
                        pltpu.VMEM((nb, blk), F32)],
        compiler_params=_params("arbitrary", "arbitrary", "arbitrary"),
        name="moba_prompt",
    )(q, k, v)


def _head_rows(page_ref, h, n_rows):
    return page_ref[pl.ds(h, n_rows, stride=N_HEADS), :]


def _moba_sample_k_kernel(pt_ref, q_ref, knew_ref, vnew_ref, *rest, pp, page, n_blocks):
    page_refs = rest[:pp]
    p_ref, oown_ref, s_scr, ksum_scr = rest[pp:]
    step = pl.program_id(1)
    n_steps = pl.num_programs(1)
    dq = q_ref.shape[0]
    scale = HEAD_DIM ** -0.5
    pages_per_block = MOBA_BLOCK // page

    qh = [q_ref[:, h * HEAD_DIM:(h + 1) * HEAD_DIM] for h in range(N_HEADS)]

    for p in range(pp):
        pr = page_refs[p]
        psum = jnp.sum(pr[...].reshape(page, N_HEADS, HEAD_DIM), axis=0)
        gpage = step * pp + p
        blk = gpage // pages_per_block
        first = (gpage % pages_per_block) == 0

        ksum_rows = pl.ds(pl.multiple_of(blk * N_HEADS, N_HEADS), N_HEADS)

        @pl.when(first)
        def _():
            ksum_scr[ksum_rows, :] = psum

        @pl.when(jnp.logical_not(first))
        def _():
            ksum_scr[ksum_rows, :] = ksum_scr[ksum_rows, :] + psum

        for h in range(N_HEADS):
            kh = _head_rows(pr, h, page).astype(BF16)
            s = lax.dot_general(qh[h].astype(BF16), kh, (((1,), (1,)), ((), ())),
                                preferred_element_type=F32) * scale
            s_scr[gpage, h * dq:(h + 1) * dq, :] = s

    @pl.when(step == n_steps - 1)
    def _():
        gates = []
        for h in range(N_HEADS):
            kmean_h = _head_rows(ksum_scr, h, n_blocks) * (1.0 / MOBA_BLOCK)
            gates.append(lax.dot_general(qh[h], kmean_h, (((1,), (1,)), ((), ())),
                                         precision=lax.Precision.HIGHEST,
                                         preferred_element_type=F32))
        gate = jnp.concatenate(gates, axis=0)
        sel = _topk_select(gate, jnp.ones(gate.shape, jnp.bool_), axis=1)
        bias = jnp.where(sel, 0.0, NEG_INF)

        s_own = []
        for h in range(N_HEADS):
            kn = knew_ref[:, h * HEAD_DIM:(h + 1) * HEAD_DIM].astype(BF16)
            s_own.append(lax.dot_general(qh[h].astype(BF16), kn, (((1,), (1,)), ((), ())),
                                         preferred_element_type=F32) * scale)
        s_own = jnp.concatenate(s_own, axis=0)
        row = lax.broadcasted_iota(jnp.int32, s_own.shape, 0) % dq
        col = lax.broadcasted_iota(jnp.int32, s_own.shape, 1)
        s_own = jnp.where(col <= row, s_own, NEG_INF)

        n_pages = n_blocks * pages_per_block
        m = jnp.max(s_own, axis=1, keepdims=True)
        for g in range(n_pages):
            n = g // pages_per_block
            sb = s_scr[g] + bias[:, n:n + 1]
            m = jnp.maximum(m, jnp.max(sb, axis=1, keepdims=True))
        p_own = jnp.exp(s_own - m)
        l = jnp.sum(p_own, axis=1, keepdims=True)
        for g in range(n_pages):
            n = g // pages_per_block
            pb = jnp.exp(s_scr[g] + bias[:, n:n + 1] - m)
            l = l + jnp.sum(pb, axis=1, keepdims=True)
            s_scr[g] = pb
        inv = 1.0 / l
        for g in range(n_pages):
            p_ref[0, :, g * page:(g + 1) * page] = s_scr[g] * inv
        p_own = p_own * inv
        for h in range(N_HEADS):
            vn = vnew_ref[:, h * HEAD_DIM:(h + 1) * HEAD_DIM].astype(BF16)
            oown_ref[0, h * dq:(h + 1) * dq, :] = jnp.dot(
                p_own[h * dq:(h + 1) * dq, :].astype(BF16), vn, preferred_element_type=F32)


def _moba_sample_v_kernel(pt_ref, p_ref, oown_ref, *rest, pp, page):
    page_refs = rest[:pp]
    o_ref, acc_scr = rest[pp:]
    step = pl.program_id(1)
    n_steps = pl.num_programs(1)
    dq = o_ref.shape[0]

    @pl.when(step == 0)
    def _():
        acc_scr[...] = oown_ref[0]

    for p in range(pp):
        pr = page_refs[p]
        for h in range(N_HEADS):
            vh = _head_rows(pr, h, page).astype(BF16)
            ph = p_ref[0, h * dq:(h + 1) * dq, p * page:(p + 1) * page].astype(BF16)
            acc_scr[h * dq:(h + 1) * dq, :] += jnp.dot(ph, vh, preferred_element_type=F32)

    @pl.when(step == n_steps - 1)
    def _():
        for h in range(N_HEADS):
            o_ref[:, h * HEAD_DIM:(h + 1) * HEAD_DIM] = acc_scr[h * dq:(h + 1) * dq, :]


def _moba_sample(q, k_new, v_new, cache_k, cache_v, page_table, pp=8):
    dec_b, n_pages = page_table.shape
    n_phys, page = cache_k.shape[0], cache_k.shape[1]
    dq = q.shape[0] // dec_b
    past = n_pages * page
    n_blocks = past // MOBA_BLOCK
    pp = min(pp, n_pages)
    n_steps = n_pages // pp
    rows = N_HEADS * dq
    ck = cache_k.reshape(n_phys, page * N_HEADS, HEAD_DIM)
    cv = cache_v.reshape(n_phys, page * N_HEADS, HEAD_DIM)
    pt = page_table.reshape(-1)

    def page_spec(p):
        return pl.BlockSpec((None, page * N_HEADS, HEAD_DIM),
                            lambda b, s, pt_ref: (pt_ref[b * n_pages + s * pp + p], 0, 0))

    new_spec = pl.BlockSpec((dq, ATTN_WIDTH), lambda b, s, pt_ref: (b, 0))
    probs, o_own = pl.pallas_call(
        functools.partial(_moba_sample_k_kernel, pp=pp, page=page, n_blocks=n_blocks),
        grid_spec=pltpu.PrefetchScalarGridSpec(
            num_scalar_prefetch=1,
            grid=(dec_b, n_steps),
            in_specs=[new_spec, new_spec, new_spec] + [page_spec(p) for p in range(pp)],
            out_specs=[pl.BlockSpec((1, rows, past), lambda b, s, pt_ref: (b, 0, 0)),
                       pl.BlockSpec((1, rows, HEAD_DIM), lambda b, s, pt_ref: (b, 0, 0))],
            scratch_shapes=[pltpu.VMEM((n_pages, rows, page), F32),
                            pltpu.VMEM((n_blocks * N_HEADS, HEAD_DIM), F32)]),
        out_shape=[jax.ShapeDtypeStruct((dec_b, rows, past), F32),
                   jax.ShapeDtypeStruct((dec_b, rows, HEAD_DIM), F32)],
        compiler_params=_params("arbitrary", "arbitrary"),
        name="moba_sample_k",
    )(pt, q, k_new, v_new, *([ck] * pp))

    return pl.pallas_call(
        functools.partial(_moba_sample_v_kernel, pp=pp, page=page),
        grid_spec=pltpu.PrefetchScalarGridSpec(
            num_scalar_prefetch=1,
            grid=(dec_b, n_steps),
            in_specs=[pl.BlockSpec((1, rows, pp * page), lambda b, s, pt_ref: (b, 0, s)),
                      pl.BlockSpec((1, rows, HEAD_DIM), lambda b, s, pt_ref: (b, 0, 0))]
                     + [page_spec(p) for p in range(pp)],
            out_specs=pl.BlockSpec((dq, ATTN_WIDTH), lambda b, s, pt_ref: (b, 0)),
            scratch_shapes=[pltpu.VMEM((rows, HEAD_DIM), F32)]),
        out_shape=jax.ShapeDtypeStruct(q.shape, F32),
        compiler_params=_params("arbitrary", "arbitrary"),
        name="moba_sample_v",
    )(pt, probs, o_own, *([cv] * pp))


def _pool_kernel(u_ref, halo_ref, w_ref, ls_ref, o_ref, ext_scr, *, pos0):
    t = pl.program_id(1)
    gb, r, width = u_ref.shape
    grp = width // len(POOL_WINDOWS)

    @pl.when(t == 0)
    def _():
        ext_scr[:, 0:POOL_HALO, :] = halo_ref[...]

    @pl.when(t > 0)
    def _():
        ext_scr[:, 0:POOL_HALO, :] = ext_scr[:, r:r + POOL_HALO, :]

    ext_scr[:, POOL_HALO:POOL_HALO + r, :] = u_ref[...]

    pos = pos0 + t * r + lax.broadcasted_iota(jnp.int32, (1, r, grp), 1)
    for g, w in enumerate(POOL_WINDOWS):
        cols = slice(g * grp, (g + 1) * grp)
        tot = ext_scr[:, POOL_HALO:POOL_HALO + r, cols]
        for d in range(1, w):
            tot = tot + ext_scr[:, POOL_HALO - d:POOL_HALO - d + r, cols]
        cnt = jnp.minimum(w, pos + 1).astype(F32)
        y = tot / cnt - u_ref[:, :, cols]
        y = jnp.dot(y.reshape(gb * r, grp).astype(BF16), w_ref[g], preferred_element_type=F32)
        o_ref[:, cols] = (y * ls_ref[:, cols]).astype(o_ref.dtype)


def _pool(u, halo, w_pool_bf, ls_pool, n_seq, seq, gb, r, pos0):
    width = u.shape[1]
    n_g, n_t = n_seq // gb, seq // r
    u3 = u.reshape(n_seq, seq, width)
    return pl.pallas_call(
        functools.partial(_pool_kernel, pos0=pos0),
        grid=(n_g, n_t),
        in_specs=[pl.BlockSpec((gb, r, width), lambda g, t: (g, t, 0)),
                  pl.BlockSpec((gb, POOL_HALO, width), lambda g, t: (g, 0, 0)),
                  pl.BlockSpec(w_pool_bf.shape, lambda g, t: (0, 0, 0)),
                  pl.BlockSpec((1, width), lambda g, t: (0, 0))],
        out_specs=pl.BlockSpec((gb * r, width), lambda g, t: (g * n_t + t, 0)),
        out_shape=jax.ShapeDtypeStruct(u.shape, BF16),
        scratch_shapes=[pltpu.VMEM((gb, r + POOL_HALO, width), F32)],
        compiler_params=_params("arbitrary", "arbitrary"),
        name="pool_mixer",
    )(u3, halo, w_pool_bf, ls_pool.reshape(1, width))


def _merge_kernel(x_ref, sh_ref, sc_ref, gt_ref, g1_ref, attn_ref, pool_ref,
                  wga_ref, wgb_ref, woa_ref, wob_ref, wout_ref, o_ref, h_scr, acc_scr):
    c = pl.program_id(2)

    @pl.when(c == 0)
    def _():
        h = _mod_norm(x_ref[...], g1_ref[...], sc_ref[...], sh_ref[...])
        h_scr[...] = h.reshape(h_scr.shape).astype(BF16)
        acc_scr[...] = jnp.zeros_like(acc_scr)

    h = h_scr[...]
    g_a = jax.nn.sigmoid(jnp.dot(h, wga_ref[...], preferred_element_type=F32))
    g_b = jax.nn.sigmoid(jnp.dot(h, wgb_ref[...], preferred_element_type=F32))
    y_a = jnp.dot(attn_ref[...].astype(BF16), woa_ref[...], preferred_element_type=F32)
    y_b = jnp.dot(pool_ref[...].astype(BF16), wob_ref[...], preferred_element_type=F32)
    mix_in = (g_a * y_a + g_b * y_b).astype(BF16)
    acc_scr[...] += jnp.dot(mix_in, wout_ref[...], preferred_element_type=F32)

    @pl.when(c == pl.num_programs(2) - 1)
    def _():
        o_ref[...] = x_ref[...] + gt_ref[...] * acc_scr[...].reshape(o_ref.shape)


def _merge(x3, mod, g_norm1, attn, pooled, w_in_bf, w_o_attn_bf, w_o_pool_bf, w_out_bf, gb, r, tn=512):
    n_g, n_t, d = _group_specs(x3, gb, r)
    tm = gb * r
    aw = attn.shape[1]
    pw = pooled.shape[1]
    gate_col0 = (w_in_bf.shape[1] - 2 * d) // tn
    n_c = d // tn
    x_spec = pl.BlockSpec((gb, r, d), lambda g, t, c: (g, t, 0))
    mod_spec = lambda k: pl.BlockSpec((gb, 1, d), lambda g, t, c: (g, 0, k))
    return pl.pallas_call(
        _merge_kernel,
        grid=(n_g, n_t, n_c),
        in_specs=[x_spec, mod_spec(0), mod_spec(1), mod_spec(2),
                  pl.BlockSpec((1, d), lambda g, t, c: (0, 0)),
                  pl.BlockSpec((tm, aw), lambda g, t, c: (g * n_t + t, 0)),
                  pl.BlockSpec((tm, pw), lambda g, t, c: (g * n_t + t, 0)),
                  pl.BlockSpec((d, tn), lambda g, t, c: (0, gate_col0 + c)),
                  pl.BlockSpec((d, tn), lambda g, t, c: (0, gate_col0 + n_c + c)),
                  pl.BlockSpec((aw, tn), lambda g, t, c: (0, c)),
                  pl.BlockSpec((pw, tn), lambda g, t, c: (0, c)),
                  pl.BlockSpec((tn, d), lambda g, t, c: (c, 0))],
        out_specs=x_spec,
        out_shape=jax.ShapeDtypeStruct(x3.shape, F32),
        scratch_shapes=[pltpu.VMEM((tm, d), BF16), pltpu.VMEM((tm, d), F32)],
        compiler_params=_params("arbitrary", "arbitrary", "arbitrary"),
        name="merge_out",
    )(x3, mod, mod, mod, g_norm1.reshape(1, d), attn, pooled,
      w_in_bf, w_in_bf, w_o_attn_bf, w_o_pool_bf, w_out_bf)


def _ffn_kernel(x_ref, sh_ref, sc_ref, gt_ref, g2_ref, wa_ref, wb_ref, wo_ref, o_ref, h_scr, acc_scr):
    c = pl.program_id(2)

    @pl.when(c == 0)
    def _():
        h = _mod_norm(x_ref[...], g2_ref[...], sc_ref[...], sh_ref[...])
        h_scr[...] = h.reshape(h_scr.shape).astype(BF16)
        acc_scr[...] = jnp.zeros_like(acc_scr)

    h = h_scr[...]
    a = jnp.dot(h, wa_ref[...], preferred_element_type=F32)
    b = jnp.dot(h, wb_ref[...], preferred_element_type=F32)
    hid = (jax.nn.silu(a) * b).astype(BF16)
    acc_scr[...] += jnp.dot(hid, wo_ref[...], preferred_element_type=F32)

    @pl.when(c == pl.num_programs(2) - 1)
    def _():
        o_ref[...] = x_ref[...] + gt_ref[...] * acc_scr[...].reshape(o_ref.shape)


def _ffn(x3, mod, g_norm2, w_ffn_in_bf, w_ffn_out_bf, gb, r, tn=512):
    n_g, n_t, d = _group_specs(x3, gb, r)
    tm = gb * r
    hidden = w_ffn_out_bf.shape[0]
    n_c = hidden // tn
    x_spec = pl.BlockSpec((gb, r, d), lambda g, t, c: (g, t, 0))
    mod_spec = lambda k: pl.BlockSpec((gb, 1, d), lambda g, t, c: (g, 0, k))
    return pl.pallas_call(
        _ffn_kernel,
        grid=(n_g, n_t, n_c),
        in_specs=[x_spec, mod_spec(3), mod_spec(4), mod_spec(5),
                  pl.BlockSpec((1, d), lambda g, t, c: (0, 0)),
                  pl.BlockSpec((d, tn), lambda g, t, c: (0, c)),
                  pl.BlockSpec((d, tn), lambda g, t, c: (0, n_c + c)),
                  pl.BlockSpec((tn, d), lambda g, t, c: (c, 0))],
        out_specs=x_spec,
        out_shape=jax.ShapeDtypeStruct(x3.shape, F32),
        scratch_shapes=[pltpu.VMEM((tm, d), BF16), pltpu.VMEM((tm, d), F32)],
        compiler_params=_params("arbitrary", "arbitrary", "arbitrary"),
        name="ffn_swiglu",
    )(x3, mod, mod, mod, g_norm2.reshape(1, d), w_ffn_in_bf, w_ffn_in_bf, w_ffn_out_bf)


def _rope_tables(pos):
    half = HEAD_DIM // 2
    inv = ROPE_THETA ** (-jnp.arange(half, dtype=F32) / half)
    ang = pos.astype(F32)[:, None] * inv[None, :]
    cos, sin = jnp.cos(ang), jnp.sin(ang)
    return jnp.concatenate([cos, cos], axis=-1), jnp.concatenate([-sin, sin], axis=-1)


def _pick_rows(seq, target=512):
    r = min(seq, target)
    while seq % r:
        r //= 2
    return r


def kernel(x_prompt, x_sample, c_prompt, c_sample, cache_k, cache_v, state_pool, page_table, w_ada, b_ada, g_norm1, g_norm2, w_in, g_qnorm, g_knorm, w_pool, ls_pool, w_o_attn, w_o_pool, w_out, w_ffn_in, w_ffn_out):
    n_p, seq, d = x_prompt.shape
    n_s, dq, _ = x_sample.shape
    past = page_table.shape[1] * cache_k.shape[1]
    pool_w = state_pool.shape[2]
    n_state = state_pool.shape[1]

    w_in_bf = w_in.astype(BF16)
    w_pool_bf = w_pool.astype(BF16)
    w_o_attn_bf = w_o_attn.astype(BF16)
    w_o_pool_bf = w_o_pool.astype(BF16)
    w_out_bf = w_out.astype(BF16)
    w_ffn_in_bf = w_ffn_in.astype(BF16)
    w_ffn_out_bf = w_ffn_out.astype(BF16)

    n_c = n_p + n_s
    c_all = jnp.concatenate([c_prompt, c_sample], axis=0)
    c_all = jnp.pad(c_all, ((0, (-n_c) % 8), (0, 0)))
    mod = _ada(c_all, w_ada, b_ada)
    mod_p = mod[:n_p].reshape(n_p, 1, 6 * d)
    mod_s = mod[n_p:n_c].reshape(n_s, 1, 6 * d)

    r_p = _pick_rows(seq)
    cos_p, sin_p = _rope_tables(jnp.arange(seq))
    q_p, k_p, v_p, u_p = _qkvu(x_prompt, mod_p, g_norm1, w_in_bf, g_qnorm, g_knorm, cos_p, sin_p, 1, r_p)
    attn_p = _moba_prompt(q_p, k_p, v_p, n_p, seq)
    pooled_p = _pool(u_p, jnp.zeros((n_p, POOL_HALO, pool_w), F32), w_pool_bf, ls_pool,
                     n_p, seq, 1, r_p, 0)
    x1_p = _merge(x_prompt, mod_p, g_norm1, attn_p, pooled_p, w_in_bf, w_o_attn_bf, w_o_pool_bf,
                  w_out_bf, 1, r_p)
    y_p = _ffn(x1_p, mod_p, g_norm2, w_ffn_in_bf, w_ffn_out_bf, 1, r_p)

    cos_s, sin_s = _rope_tables(past + jnp.arange(dq))
    cos_s, sin_s = jnp.tile(cos_s, (n_s, 1)), jnp.tile(sin_s, (n_s, 1))
    q_s, k_s, v_s, u_s = _qkvu(x_sample, mod_s, g_norm1, w_in_bf, g_qnorm, g_knorm, cos_s, sin_s, n_s, dq)
    attn_s = _moba_sample(q_s, k_s, v_s, cache_k, cache_v, page_table)
    halo_s = jnp.pad(state_pool, ((0, 0), (POOL_HALO - n_state, 0), (0, 0)))
    pooled_s = _pool(u_s, halo_s, w_pool_bf, ls_pool, n_s, dq, n_s, dq, past)
    x1_s = _merge(x_sample, mod_s, g_norm1, attn_s, pooled_s, w_in_bf, w_o_attn_bf, w_o_pool_bf,
                  w_out_bf, n_s, dq)
    y_s = _ffn(x1_s, mod_s, g_norm2, w_ffn_in_bf, w_ffn_out_bf, n_s, dq)

    k_prompt = k_p.reshape(n_p, seq, N_HEADS, HEAD_DIM)
    v_prompt = v_p.reshape(n_p, seq, N_HEADS, HEAD_DIM)
    pool_prompt = u_p.reshape(n_p, seq, pool_w)[:, seq - n_state:]
    k_sample = k_s.reshape(n_s, dq, N_HEADS, HEAD_DIM)
    v_sample = v_s.reshape(n_s, dq, N_HEADS, HEAD_DIM)
    pool_sample = jnp.concatenate([state_pool, u_s.reshape(n_s, dq, pool_w)], axis=1)[:, -n_state:]
    return (y_p, y_s, k_prompt, v_prompt, pool_prompt, k_sample, v_sample, pool_sample)
```

```python
import functools

import jax
import jax.numpy as jnp
from jax import lax
from jax.experimental import pallas as pl
from jax.experimental.pallas import tpu as pltpu

N_HEADS = 8
HEAD_DIM = 128
ATTN_WIDTH = N_HEADS * HEAD_DIM
MOBA_BLOCK = 256
MOBA_TOPK = 3
ROPE_THETA = 10000.0
POOL_WINDOWS = (2, 4, 8, 16)
POOL_HALO = 16
RMS_EPS = 1e-6
_TILE_GROUPS = (8, 4, 2)

F32 = jnp.float32
BF16 = jnp.bfloat16
NEG_INF = float("-inf")
_LOG2_E = 1.4426950408889634

_VMEM_LIMIT = 52 * 1024 * 1024


def _params(*sem):
    return pltpu.CompilerParams(dimension_semantics=sem, vmem_limit_bytes=_VMEM_LIMIT)


def _mod_norm(x, g, sc, sh):
    ms = jnp.mean(x * x, axis=-1, keepdims=True)
    y = x * lax.rsqrt(ms + RMS_EPS) * g
    return y * (1.0 + sc) + sh


def _ada_kernel(c_ref, w_ref, b_ref, o_ref):
    o_ref[...] = jnp.dot(c_ref[...].astype(BF16), w_ref[...].astype(BF16),
                         preferred_element_type=F32) + b_ref[...]


def _ada(c_all, w_ada, b_ada, tn=1024):
    m, d = c_all.shape
    n = w_ada.shape[1]
    return pl.pallas_call(
        _ada_kernel,
        grid=(n // tn,),
        in_specs=[pl.BlockSpec((m, d), lambda j: (0, 0)),
                  pl.BlockSpec((d, tn), lambda j: (0, j)),
                  pl.BlockSpec((1, tn), lambda j: (0, j))],
        out_specs=pl.BlockSpec((m, tn), lambda j: (0, j)),
        out_shape=jax.ShapeDtypeStruct((m, n), F32),
        compiler_params=_params("arbitrary"),
        name="ada_mod",
    )(c_all, w_ada, b_ada.reshape(1, n))


def _head_norm_rope(t, g, cos, sin_signed):
    outs = []
    for h in range(N_HEADS):
        th = t[:, h * HEAD_DIM:(h + 1) * HEAD_DIM]
        ms = jnp.mean(th * th, axis=-1, keepdims=True)
        y = th * lax.rsqrt(ms + RMS_EPS) * g
        outs.append(y * cos + pltpu.roll(y, HEAD_DIM // 2, 1) * sin_signed)
    return jnp.concatenate(outs, axis=-1)


def _qkvu_kernel(x_ref, sh_ref, sc_ref, g1_ref, w_ref, gq_ref, gk_ref, cos_ref, sin_ref,
                 q_ref, k_ref, v_ref, u_ref, h_scr):
    j = pl.program_id(2)

    @pl.when(j == 0)
    def _():
        h = _mod_norm(x_ref[...], g1_ref[...], sc_ref[...], sh_ref[...])
        h_scr[...] = h.reshape(h_scr.shape).astype(BF16)

    acc = jnp.dot(h_scr[...], w_ref[...], preferred_element_type=F32)

    @pl.when(j == 0)
    def _():
        q_ref[...] = _head_norm_rope(acc, gq_ref[...], cos_ref[...], sin_ref[...])

    @pl.when(j == 1)
    def _():
        k_ref[...] = _head_norm_rope(acc, gk_ref[...], cos_ref[...], sin_ref[...])

    @pl.when(j == 2)
    def _():
        v_ref[...] = acc

    @pl.when(j == 3)
    def _():
        u_ref[...] = acc


def _group_specs(x3, gb, r):
    g_total, s, d = x3.shape
    n_g, n_t = g_total // gb, s // r
    return n_g, n_t, d


def _qkvu(x3, mod, g_norm1, w_in_bf, g_q, g_k, cos, sin_signed, gb, r):
    n_g, n_t, d = _group_specs(x3, gb, r)
    tm = gb * r
    tokens = x3.shape[0] * x3.shape[1]
    wq = ATTN_WIDTH
    x_spec = pl.BlockSpec((gb, r, d), lambda g, t, j: (g, t, 0))
    mod_spec = lambda k: pl.BlockSpec((gb, 1, d), lambda g, t, j: (g, 0, k))
    row_spec = pl.BlockSpec((1, d), lambda g, t, j: (0, 0))
    head_spec = pl.BlockSpec((1, HEAD_DIM), lambda g, t, j: (0, 0))
    tab_spec = pl.BlockSpec((tm, HEAD_DIM), lambda g, t, j: (t, 0))
    out_spec = pl.BlockSpec((tm, wq), lambda g, t, j: (g * n_t + t, 0))
    out_sds = jax.ShapeDtypeStruct((tokens, wq), F32)
    return pl.pallas_call(
        _qkvu_kernel,
        grid=(n_g, n_t, 4),
        in_specs=[x_spec, mod_spec(0), mod_spec(1), row_spec,
                  pl.BlockSpec((d, wq), lambda g, t, j: (0, j)),
                  head_spec, head_spec, tab_spec, tab_spec],
        out_specs=[out_spec] * 4,
        out_shape=[out_sds] * 4,
        scratch_shapes=[pltpu.VMEM((tm, d), BF16)],
        compiler_params=_params("arbitrary", "arbitrary", "arbitrary"),
        name="qkvu_proj",
    )(x3, mod, mod, g_norm1.reshape(1, d), w_in_bf, g_q.reshape(1, HEAD_DIM),
      g_k.reshape(1, HEAD_DIM), cos, sin_signed)


def _topk_select(gate, valid, axis):
    n = gate.shape[axis]
    g = jnp.where(valid, gate, NEG_INF)
    idx = lax.broadcasted_iota(jnp.int32, gate.shape, axis)
    rank = jnp.zeros(gate.shape, jnp.int32)
    for m in range(n):
        gm = g[m:m + 1, :] if axis == 0 else g[:, m:m + 1]
        beats = (gm > g) | ((gm == g) & (m < idx))
        rank = rank + beats.astype(jnp.int32)
    return valid & (rank < MOBA_TOPK) & (jnp.abs(g) < float("inf"))


def _moba_prompt_kernel(q_ref, k_ref, v_ref, o_ref, kb_scr, vt_scr, kmean_scr, bias_scr, s_scr, *, nb):
    qi = pl.program_id(2)
    blk = MOBA_BLOCK

    @pl.when(qi == 0)
    def _():
        for n in range(nb):
            kn = k_ref[n * blk:(n + 1) * blk, :]
            kb_scr[n] = kn.astype(BF16)
            kmean_scr[n:n + 1, :] = jnp.mean(kn, axis=0, keepdims=True)
            vt_scr[n] = v_ref[n * blk:(n + 1) * blk, :].T.astype(BF16)
        kmean = kmean_scr[...]
        for qb_i in range(nb):
            gate = lax.dot_general(kmean, q_ref[qb_i * blk:(qb_i + 1) * blk, :],
                                   (((1,), (1,)), ((), ())),
                                   precision=lax.Precision.HIGHEST, preferred_element_type=F32)
            n_idx = lax.broadcasted_iota(jnp.int32, gate.shape, 0)
            sel = _topk_select(gate, n_idx < qb_i, axis=0)
            bias_scr[qb_i] = jnp.where(sel | (n_idx >= qb_i), 0.0, NEG_INF)

    qb = q_ref[pl.ds(pl.multiple_of(qi * blk, blk), blk), :].astype(BF16)
    exp2_scale = HEAD_DIM ** -0.5 * _LOG2_E
    kpos = lax.broadcasted_iota(jnp.int32, (blk, blk), 0)
    qpos = lax.broadcasted_iota(jnp.int32, (blk, blk), 1)

    def fold(x, op):
        return op(x.reshape(blk // 8, 8, blk), axis=0)

    n_tiles = qi + 1
    spans = []
    base = 0
    for width in _TILE_GROUPS[:-1]:
        trips = (n_tiles - base) // width
        spans.append((width, base, trips))
        base = base + trips * width
    spans.append((_TILE_GROUPS[-1], base, (n_tiles - base + 1) // _TILE_GROUPS[-1]))

    def pass1(width, base):
        def body(jj, m8):
            for t in range(width):
                j = base + width * jj + t
                jm = jnp.minimum(j, nb - 1)
                s = lax.dot_general(kb_scr[jm], qb, (((1,), (1,)), ((), ())),
                                    preferred_element_type=F32)
                s = jnp.where(kpos + (j - qi) * blk <= qpos, s + bias_scr[qi, pl.ds(jm, 1), :], NEG_INF)
                s_scr[j] = s
                m8 = jnp.maximum(m8, fold(s, jnp.max))
            return m8
        return body

    m8 = jnp.full((8, blk), NEG_INF, F32)
    for width, base, trips in spans:
        m8 = lax.fori_loop(0, trips, pass1(width, base), m8)
    m = jnp.max(m8, axis=0, keepdims=True)

    def pass2(width, base):
        def body(jj, carry):
            l8, acc = carry
            for t in range(width):
                j = base + width * jj + t
                p = jnp.exp2((s_scr[j] - m) * exp2_scale)
                l8 = l8 + fold(p, jnp.sum)
                acc = acc + jnp.dot(vt_scr[jnp.minimum(j, nb - 1)], p.astype(BF16),
                                    preferred_element_type=F32)
            return l8, acc
        return body

    l8 = jnp.zeros((8, blk), F32)
    acc = jnp.zeros((HEAD_DIM, blk), F32)
    for width, base, trips in spans:
        l8, acc = lax.fori_loop(0, trips, pass2(width, base), (l8, acc))
    l = jnp.sum(l8, axis=0, keepdims=True)
    o_ref[...] = (acc / l).T.astype(o_ref.dtype)


def _moba_prompt(q, k, v, n_seq, seq):
    nb = seq // MOBA_BLOCK
    blk = MOBA_BLOCK
    kv_spec = pl.BlockSpec((seq, HEAD_DIM), lambda b, h, i: (b, h))
    q_spec = pl.BlockSpec((blk, HEAD_DIM), lambda b, h, i: (b * nb + i, h))
    return pl.pallas_call(
        functools.partial(_moba_prompt_kernel, nb=nb),
        grid=(n_seq, N_HEADS, nb),
        in_specs=[kv_spec, kv_spec, kv_spec],
        out_specs=q_spec,
        out_shape=jax.ShapeDtypeStruct(q.shape, BF16),
        scratch_shapes=[pltpu.VMEM((nb, blk, HEAD_DIM), BF16),
                        pltpu.VMEM((nb, HEAD_DIM, blk), BF16),
                        pltpu.VMEM((nb, HEAD_DIM), F32),
                        pltpu.VMEM((nb, nb, blk), F32),
                        pltpu.VMEM((nb + 1, blk, blk), F32)],
        compiler_params=_params("arbitrary", "arbitrary", "arbitrary"),
        name="moba_prompt",
    )(q, k, v)


def _page_copies(cache_hbm, pt_ref, buf, sem, slot, first_entry, n_fetch):
    copies = []
    for p in range(n_fetch):
        pg = pt_ref[first_entry + p]
        for h in range(N_HEADS):
            copies.append(pltpu.make_async_copy(cache_hbm.at[pg, :, h, :], buf.at[slot, p, h],
                                                sem.at[slot]))
    return copies


def _paged_halves(pt_ref, cache_hbm, buf, sem, hp, n_pages, compute):
    b, step = pl.program_id(0), pl.program_id(1)
    n_b, n_steps = pl.num_programs(0), pl.num_programs(1)
    is_first = jnp.logical_and(b == 0, step == 0)
    is_last = jnp.logical_and(b == n_b - 1, step == n_steps - 1)
    wrap = step == n_steps - 1
    nxt_b = jnp.where(wrap, b + 1, b)
    nxt_step = jnp.where(wrap, 0, step + 1)

    def entry(bb, ss, slot):
        return bb * n_pages + (2 * ss + slot) * hp

    @pl.when(is_first)
    def _():
        for slot in range(2):
            for c in _page_copies(cache_hbm, pt_ref, buf, sem, slot, entry(b, step, slot), hp):
                c.start()

    for slot in range(2):
        for c in _page_copies(cache_hbm, pt_ref, buf, sem, slot, entry(b, step, slot), hp):
            c.wait()
        compute(slot, (2 * step + slot) * hp)

        @pl.when(jnp.logical_not(is_last))
        def _():
            for c in _page_copies(cache_hbm, pt_ref, buf, sem, slot, entry(nxt_b, nxt_step, slot), hp):
                c.start()


def _moba_sample_k_kernel(pt_ref, q_ref, knew_ref, vnew_ref, cache_hbm, p_ref, oown_ref, linv_ref,
                          buf, sem, s_scr, kpart_scr, *, hp, page, n_blocks):
    step = pl.program_id(1)
    n_steps = pl.num_programs(1)
    dq = q_ref.shape[0]
    rows = N_HEADS * dq
    scale = HEAD_DIM ** -0.5
    pages_per_block = MOBA_BLOCK // page
    n_pages = n_blocks * pages_per_block

    qh = [q_ref[:, h * HEAD_DIM:(h + 1) * HEAD_DIM] for h in range(N_HEADS)]
    qh_bf = [x.astype(BF16) for x in qh]

    @pl.when(step == 0)
    def _():
        kpart_scr[...] = jnp.zeros_like(kpart_scr)

    def compute(slot, first_page):
        for p in range(hp):
            gpage = first_page + p
            blk = gpage // pages_per_block
            for h in range(N_HEADS):
                kh = buf[slot, p, h]
                kpart_scr[blk, h] += jnp.sum(kh.reshape(page // 8, 8, HEAD_DIM), axis=0)
                s = lax.dot_general(qh_bf[h], kh.astype(BF16), (((1,), (1,)), ((), ())),
                                    preferred_element_type=F32) * scale
                s_scr[gpage, h * dq:(h + 1) * dq, :] = s

    _paged_halves(pt_ref, cache_hbm, buf, sem, hp, n_pages, compute)

    @pl.when(step == n_steps - 1)
    def _():
        gates = []
        for h in range(N_HEADS):
            kmean_h = jnp.sum(kpart_scr[:, h], axis=1) * (1.0 / MOBA_BLOCK)
            gates.append(lax.dot_general(qh[h], kmean_h, (((1,), (1,)), ((), ())),
                                         precision=lax.Precision.HIGHEST,
                                         preferred_element_type=F32))
        gate = jnp.concatenate(gates, axis=0)
        sel = _topk_select(gate, jnp.ones(gate.shape, jnp.bool_), axis=1)
        bias = jnp.where(sel, 0.0, NEG_INF)

        s_own = []
        for h in range(N_HEADS):
            kn = knew_ref[:, h * HEAD_DIM:(h + 1) * HEAD_DIM].astype(BF16)
            s_own.append(lax.dot_general(qh_bf[h], kn, (((1,), (1,)), ((), ())),
                                         preferred_element_type=F32) * scale)
        s_own = jnp.concatenate(s_own, axis=0)
        row = lax.broadcasted_iota(jnp.int32, s_own.shape, 0) % dq
        col = lax.broadcasted_iota(jnp.int32, s_own.shape, 1)
        s_own = jnp.where(col <= row, s_own, NEG_INF)

        m_vec = jnp.full((rows, page), NEG_INF, F32)
        for g in range(n_pages):
            n = g // pages_per_block
            m_vec = jnp.maximum(m_vec, s_scr[g] + bias[:, n:n + 1])
        m = jnp.maximum(jnp.max(m_vec, axis=1, keepdims=True), jnp.max(s_own, axis=1, keepdims=True))
        p_own = jnp.exp(s_own - m)
        l_vec = jnp.zeros((rows, page), F32)
        for g in range(n_pages):
            n = g // pages_per_block
            pb = jnp.exp(s_scr[g] + bias[:, n:n + 1] - m)
            l_vec = l_vec + pb
            p_ref[0, :, g * page:(g + 1) * page] = pb
        l = jnp.sum(l_vec, axis=1, keepdims=True) + jnp.sum(p_own, axis=1, keepdims=True)
        linv_ref[0] = jnp.broadcast_to(1.0 / l, (rows, HEAD_DIM))
        for h in range(N_HEADS):
            vn = vnew_ref[:, h * HEAD_DIM:(h + 1) * HEAD_DIM].astype(BF16)
            oown_ref[0, h * dq:(h + 1) * dq, :] = jnp.dot(
                p_own[h * dq:(h + 1) * dq, :].astype(BF16), vn, preferred_element_type=F32)


def _moba_sample_v_kernel(pt_ref, p_ref, oown_ref, linv_ref, cache_hbm, o_ref, buf, sem, acc_scr,
                          *, hp, page, n_pages):
    step = pl.program_id(1)
    n_steps = pl.num_programs(1)
    dq = o_ref.shape[0]

    @pl.when(step == 0)
    def _():
        acc_scr[...] = oown_ref[0]

    def compute(slot, first_page):
        for h in range(N_HEADS):
            tot = jnp.zeros((dq, HEAD_DIM), F32)
            for p in range(hp):
                col0 = (slot * hp + p) * page
                ph = p_ref[0, h * dq:(h + 1) * dq, col0:col0 + page].astype(BF16)
                tot = tot + jnp.dot(ph, buf[slot, p, h].astype(BF16), preferred_element_type=F32)
            acc_scr[h * dq:(h + 1) * dq, :] += tot

    _paged_halves(pt_ref, cache_hbm, buf, sem, hp, n_pages, compute)

    @pl.when(step == n_steps - 1)
    def _():
        for h in range(N_HEADS):
            rows_h = slice(h * dq, (h + 1) * dq)
            o_ref[:, h * HEAD_DIM:(h + 1) * HEAD_DIM] = acc_scr[rows_h, :] * linv_ref[0, rows_h, :]


def _moba_sample(q, k_new, v_new, cache_k, cache_v, page_table, hp=8):
    dec_b, n_pages = page_table.shape
    page = cache_k.shape[1]
    dq = q.shape[0] // dec_b
    past = n_pages * page
    n_blocks = past // MOBA_BLOCK
    while n_pages % (2 * hp):
        hp //= 2
    n_steps = n_pages // (2 * hp)
    rows = N_HEADS * dq
    pt = page_table.reshape(-1)

    new_spec = pl.BlockSpec((dq, ATTN_WIDTH), lambda b, s, pt_ref: (b, 0))
    any_spec = pl.BlockSpec(memory_space=pl.ANY)
    stat_spec = pl.BlockSpec((1, rows, HEAD_DIM), lambda b, s, pt_ref: (b, 0, 0))
    stat_sds = jax.ShapeDtypeStruct((dec_b, rows, HEAD_DIM), F32)
    page_buf = pltpu.VMEM((2, hp, N_HEADS, page, HEAD_DIM), F32)
    probs, o_own, l_inv = pl.pallas_call(
        functools.partial(_moba_sample_k_kernel, hp=hp, page=page, n_blocks=n_blocks),
        grid_spec=pltpu.PrefetchScalarGridSpec(
            num_scalar_prefetch=1,
            grid=(dec_b, n_steps),
            in_specs=[new_spec, new_spec, new_spec, any_spec],
            out_specs=[pl.BlockSpec((1, rows, past), lambda b, s, pt_ref: (b, 0, 0)),
                       stat_spec, stat_spec],
            scratch_shapes=[page_buf, pltpu.SemaphoreType.DMA((2,)),
                            pltpu.VMEM((n_pages, rows, page), F32),
                            pltpu.VMEM((n_blocks, N_HEADS, 8, HEAD_DIM), F32)]),
        out_shape=[jax.ShapeDtypeStruct((dec_b, rows, past), F32), stat_sds, stat_sds],
        compiler_params=_params("arbitrary", "arbitrary"),
        name="moba_sample_k",
    )(pt, q, k_new, v_new, cache_k)

    return pl.pallas_call(
        functools.partial(_moba_sample_v_kernel, hp=hp, page=page, n_pages=n_pages),
        grid_spec=pltpu.PrefetchScalarGridSpec(
            num_scalar_prefetch=1,
            grid=(dec_b, n_steps),
            in_specs=[pl.BlockSpec((1, rows, 2 * hp * page), lambda b, s, pt_ref: (b, 0, s)),
                      stat_spec, stat_spec, any_spec],
            out_specs=pl.BlockSpec((dq, ATTN_WIDTH), lambda b, s, pt_ref: (b, 0)),
            scratch_shapes=[page_buf, pltpu.SemaphoreType.DMA((2,)),
                            pltpu.VMEM((rows, HEAD_DIM), F32)]),
        out_shape=jax.ShapeDtypeStruct(q.shape, F32),
        compiler_params=_params("arbitrary", "arbitrary"),
        name="moba_sample_v",
    )(pt, probs, o_own, l_inv, cache_v)


def _pool_kernel(u_ref, halo_ref, w_ref, ls_ref, o_ref, ext_scr, *, pos0):
    t = pl.program_id(1)
    gb, r, width = u_ref.shape
    grp = width // len(POOL_WINDOWS)

    @pl.when(t == 0)
    def _():
        ext_scr[:, 0:POOL_HALO, :] = halo_ref[...]

    @pl.when(t > 0)
    def _():
        ext_scr[:, 0:POOL_HALO, :] = ext_scr[:, r:r + POOL_HALO, :]

    ext_scr[:, POOL_HALO:POOL_HALO + r, :] = u_ref[...]

    pos = pos0 + t * r + lax.broadcasted_iota(jnp.int32, (1, r, grp), 1)
    for g, w in enumerate(POOL_WINDOWS):
        cols = slice(g * grp, (g + 1) * grp)
        tot = ext_scr[:, POOL_HALO:POOL_HALO + r, cols]
        for d in range(1, w):
            tot = tot + ext_scr[:, POOL_HALO - d:POOL_HALO - d + r, cols]
        cnt = jnp.minimum(w, pos + 1).astype(F32)
        y = tot / cnt - u_ref[:, :, cols]
        y = jnp.dot(y.reshape(gb * r, grp).astype(BF16), w_ref[g], preferred_element_type=F32)
        o_ref[:, cols] = (y * ls_ref[:, cols]).astype(o_ref.dtype)


def _pool(u, halo, w_pool_bf, ls_pool, n_seq, seq, gb, r, pos0):
    width = u.shape[1]
    n_g, n_t = n_seq // gb, seq // r
    u3 = u.reshape(n_seq, seq, width)
    return pl.pallas_call(
        functools.partial(_pool_kernel, pos0=pos0),
        grid=(n_g, n_t),
        in_specs=[pl.BlockSpec((gb, r, width), lambda g, t: (g, t, 0)),
                  pl.BlockSpec((gb, POOL_HALO, width), lambda g, t: (g, 0, 0)),
                  pl.BlockSpec(w_pool_bf.shape, lambda g, t: (0, 0, 0)),
                  pl.BlockSpec((1, width), lambda g, t: (0, 0))],
        out_specs=pl.BlockSpec((gb * r, width), lambda g, t: (g * n_t + t, 0)),
        out_shape=jax.ShapeDtypeStruct(u.shape, BF16),
        scratch_shapes=[pltpu.VMEM((gb, r + POOL_HALO, width), F32)],
        compiler_params=_params("arbitrary", "arbitrary"),
        name="pool_mixer",
    )(u3, halo, w_pool_bf, ls_pool.reshape(1, width))


def _merge_kernel(x_ref, sh_ref, sc_ref, gt_ref, g1_ref, attn_ref, pool_ref,
                  wga_ref, wgb_ref, woa_ref, wob_ref, wout_ref, o_ref, h_scr, acc_scr):
    c = pl.program_id(2)

    @pl.when(c == 0)
    def _():
        h = _mod_norm(x_ref[...], g1_ref[...], sc_ref[...], sh_ref[...])
        h_scr[...] = h.reshape(h_scr.shape).astype(BF16)
        acc_scr[...] = jnp.zeros_like(acc_scr)

    h = h_scr[...]
    g_a = jax.nn.sigmoid(jnp.dot(h, wga_ref[...], preferred_element_type=F32))
    g_b = jax.nn.sigmoid(jnp.dot(h, wgb_ref[...], preferred_element_type=F32))
    y_a = jnp.dot(attn_ref[...].astype(BF16), woa_ref[...], preferred_element_type=F32)
    y_b = jnp.dot(pool_ref[...].astype(BF16), wob_ref[...], preferred_element_type=F32)
    mix_in = (g_a * y_a + g_b * y_b).astype(BF16)
    acc_scr[...] += jnp.dot(mix_in, wout_ref[...], preferred_element_type=F32)

    @pl.when(c == pl.num_programs(2) - 1)
    def _():
        o_ref[...] = x_ref[...] + gt_ref[...] * acc_scr[...].reshape(o_ref.shape)


def _merge(x3, mod, g_norm1, attn, pooled, w_in_bf, w_o_attn_bf, w_o_pool_bf, w_out_bf, gb, r, tn=512):
    n_g, n_t, d = _group_specs(x3, gb, r)
    tm = gb * r
    aw = attn.shape[1]
    pw = pooled.shape[1]
    gate_col0 = (w_in_bf.shape[1] - 2 * d) // tn
    n_c = d // tn
    x_spec = pl.BlockSpec((gb, r, d), lambda g, t, c: (g, t, 0))
    mod_spec = lambda k: pl.BlockSpec((gb, 1, d), lambda g, t, c: (g, 0, k))
    return pl.pallas_call(
        _merge_kernel,
        grid=(n_g, n_t, n_c),
        in_specs=[x_spec, mod_spec(0), mod_spec(1), mod_spec(2),
                  pl.BlockSpec((1, d), lambda g, t, c: (0, 0)),
                  pl.BlockSpec((tm, aw), lambda g, t, c: (g * n_t + t, 0)),
                  pl.BlockSpec((tm, pw), lambda g, t, c: (g * n_t + t, 0)),
                  pl.BlockSpec((d, tn), lambda g, t, c: (0, gate_col0 + c)),
                  pl.BlockSpec((d, tn), lambda g, t, c: (0, gate_col0 + n_c + c)),
                  pl.BlockSpec((aw, tn), lambda g, t, c: (0, c)),
                  pl.BlockSpec((pw, tn), lambda g, t, c: (0, c)),
                  pl.BlockSpec((tn, d), lambda g, t, c: (c, 0))],
        out_specs=x_spec,
        out_shape=jax.ShapeDtypeStruct(x3.shape, F32),
        scratch_shapes=[pltpu.VMEM((tm, d), BF16), pltpu.VMEM((tm, d), F32)],
        compiler_params=_params("arbitrary", "arbitrary", "arbitrary"),
        name="merge_out",
    )(x3, mod, mod, mod, g_norm1.reshape(1, d), attn, pooled,
      w_in_bf, w_in_bf, w_o_attn_bf, w_o_pool_bf, w_out_bf)


def _ffn_kernel(x_ref, sh_ref, sc_ref, gt_ref, g2_ref, wa_ref, wb_ref, wo_ref, o_ref, h_scr, acc_scr):
    c = pl.program_id(2)

    @pl.when(c == 0)
    def _():
        h = _mod_norm(x_ref[...], g2_ref[...], sc_ref[...], sh_ref[...])
        h_scr[...] = h.reshape(h_scr.shape).astype(BF16)
        acc_scr[...] = jnp.zeros_like(acc_scr)

    h = h_scr[...]
    a = jnp.dot(h, wa_ref[...], preferred_element_type=F32)
    b = jnp.dot(h, wb_ref[...], preferred_element_type=F32)
    hid = (jax.nn.silu(a) * b).astype(BF16)
    acc_scr[...] += jnp.dot(hid, wo_ref[...], preferred_element_type=F32)

    @pl.when(c == pl.num_programs(2) - 1)
    def _():
        o_ref[...] = x_ref[...] + gt_ref[...] * acc_scr[...].reshape(o_ref.shape)


def _ffn(x3, mod, g_norm2, w_ffn_in_bf, w_ffn_out_bf, gb, r, tn=512):
    n_g, n_t, d = _group_specs(x3, gb, r)
    tm = gb * r
    hidden = w_ffn_out_bf.shape[0]
    n_c = hidden // tn
    x_spec = pl.BlockSpec((gb, r, d), lambda g, t, c: (g, t, 0))
    mod_spec = lambda k: pl.BlockSpec((gb, 1, d), lambda g, t, c: (g, 0, k))
    return pl.pallas_call(
        _ffn_kernel,
        grid=(n_g, n_t, n_c),
        in_specs=[x_spec, mod_spec(3), mod_spec(4), mod_spec(5),
                  pl.BlockSpec((1, d), lambda g, t, c: (0, 0)),
                  pl.BlockSpec((d, tn), lambda g, t, c: (0, c)),
                  pl.BlockSpec((d, tn), lambda g, t, c: (0, n_c + c)),
                  pl.BlockSpec((tn, d), lambda g, t, c: (c, 0))],
        out_specs=x_spec,
        out_shape=jax.ShapeDtypeStruct(x3.shape, F32),
        scratch_shapes=[pltpu.VMEM((tm, d), BF16), pltpu.VMEM((tm, d), F32)],
        compiler_params=_params("arbitrary", "arbitrary", "arbitrary"),
        name="ffn_swiglu",
    )(x3, mod, mod, mod, g_norm2.reshape(1, d), w_ffn_in_bf, w_ffn_in_bf, w_ffn_out_bf)


def _rope_tables(pos):
    half = HEAD_DIM // 2
    inv = ROPE_THETA ** (-jnp.arange(half, dtype=F32) / half)
    ang = pos.astype(F32)[:, None] * inv[None, :]
    cos, sin = jnp.cos(ang), jnp.sin(ang)
    return jnp.concatenate([cos, cos], axis=-1), jnp.concatenate([-sin, sin], axis=-1)


def _pick_rows(seq, target=512):
    r = min(seq, target)
    while seq % r:
        r //= 2
    return r


def kernel(x_prompt, x_sample, c_prompt, c_sample, cache_k, cache_v, state_pool, page_table, w_ada, b_ada, g_norm1, g_norm2, w_in, g_qnorm, g_knorm, w_pool, ls_pool, w_o_attn, w_o_pool, w_out, w_ffn_in, w_ffn_out):
    n_p, seq, d = x_prompt.shape
    n_s, dq, _ = x_sample.shape
    past = page_table.shape[1] * cache_k.shape[1]
    pool_w = state_pool.shape[2]
    n_state = state_pool.shape[1]

    w_in_bf = w_in.astype(BF16)
    w_pool_bf = w_pool.astype(BF16)
    w_o_attn_bf = w_o_attn.astype(BF16)
    w_o_pool_bf = w_o_pool.astype(BF16)
    w_out_bf = w_out.astype(BF16)
    w_ffn_in_bf = w_ffn_in.astype(BF16)
    w_ffn_out_bf = w_ffn_out.astype(BF16)

    n_c = n_p + n_s
    c_all = jnp.concatenate([c_prompt, c_sample], axis=0)
    c_all = jnp.pad(c_all, ((0, (-n_c) % 8), (0, 0)))
    mod = _ada(c_all, w_ada, b_ada)
    mod_p = mod[:n_p].reshape(n_p, 1, 6 * d)
    mod_s = mod[n_p:n_c].reshape(n_s, 1, 6 * d)

    r_p = _pick_rows(seq)
    cos_p, sin_p = _rope_tables(jnp.arange(seq))
    q_p, k_p, v_p, u_p = _qkvu(x_prompt, mod_p, g_norm1, w_in_bf, g_qnorm, g_knorm, cos_p, sin_p, 1, r_p)
    attn_p = _moba_prompt(q_p, k_p, v_p, n_p, seq)
    pooled_p = _pool(u_p, jnp.zeros((n_p, POOL_HALO, pool_w), F32), w_pool_bf, ls_pool,
                     n_p, seq, 1, r_p, 0)
    x1_p = _merge(x_prompt, mod_p, g_norm1, attn_p, pooled_p, w_in_bf, w_o_attn_bf, w_o_pool_bf,
                  w_out_bf, 1, r_p)
    y_p = _ffn(x1_p, mod_p, g_norm2, w_ffn_in_bf, w_ffn_out_bf, 1, r_p)

    cos_s, sin_s = _rope_tables(past + jnp.arange(dq))
    cos_s, sin_s = jnp.tile(cos_s, (n_s, 1)), jnp.tile(sin_s, (n_s, 1))
    q_s, k_s, v_s, u_s = _qkvu(x_sample, mod_s, g_norm1, w_in_bf, g_qnorm, g_knorm, cos_s, sin_s, n_s, dq)
    attn_s = _moba_sample(q_s, k_s, v_s, cache_k, cache_v, page_table)
    halo_s = jnp.pad(state_pool, ((0, 0), (POOL_HALO - n_state, 0), (0, 0)))
    pooled_s = _pool(u_s, halo_s, w_pool_bf, ls_pool, n_s, dq, n_s, dq, past)
    x1_s = _merge(x_sample, mod_s, g_norm1, attn_s, pooled_s, w_in_bf, w_o_attn_bf, w_o_pool_bf,
                  w_out_bf, n_s, dq)
    y_s = _ffn(x1_s, mod_s, g_norm2, w_ffn_in_bf, w_ffn_out_bf, n_s, dq)

    k_prompt = k_p.reshape(n_p, seq, N_HEADS, HEAD_DIM)
    v_prompt = v_p.reshape(n_p, seq, N_HEADS, HEAD_DIM)
    pool_prompt = u_p.reshape(n_p, seq, pool_w)[:, seq - n_state:]
    k_sample = k_s.reshape(n_s, dq, N_HEADS, HEAD_DIM)
    v_sample = v_s.reshape(n_s, dq, N_HEADS, HEAD_DIM)
    pool_sample = jnp.concatenate([state_pool, u_s.reshape(n_s, dq, pool_w)], axis=1)[:, -n_state:]
    return (y_p, y_s, k_prompt, v_prompt, pool_prompt, k_sample, v_sample, pool_sample)
```

```python
import functools

import jax
import jax.numpy as jnp
from jax import lax
from jax.experimental import pallas as pl
from jax.experimental.pallas import tpu as pltpu

N_HEADS = 8
HEAD_DIM = 128
ATTN_WIDTH = N_HEADS * HEAD_DIM
MOBA_BLOCK = 256
MOBA_TOPK = 3
ROPE_THETA = 10000.0
POOL_WINDOWS = (2, 4, 8, 16)
POOL_HALO = 16
RMS_EPS = 1e-6
_TILE_GROUPS = (8, 4, 2)

F32 = jnp.float32
BF16 = jnp.bfloat16
NEG_INF = float("-inf")
_LOG2_E = 1.4426950408889634

_VMEM_LIMIT = 52 * 1024 * 1024


def _params(*sem):
    return pltpu.CompilerParams(dimension_semantics=sem, vmem_limit_bytes=_VMEM_LIMIT)


def _mod_norm(x, g, sc, sh):
    ms = jnp.mean(x * x, axis=-1, keepdims=True)
    y = x * lax.rsqrt(ms + RMS_EPS) * g
    return y * (1.0 + sc) + sh


def _ada_kernel(c_ref, w_ref, b_ref, o_ref):
    o_ref[...] = jnp.dot(c_ref[...].astype(BF16), w_ref[...].astype(BF16),
                         preferred_element_type=F32) + b_ref[...]


def _ada(c_all, w_ada, b_ada, tn=1024):
    m, d = c_all.shape
    n = w_ada.shape[1]
    return pl.pallas_call(
        _ada_kernel,
        grid=(n // tn,),
        in_specs=[pl.BlockSpec((m, d), lambda j: (0, 0)),
                  pl.BlockSpec((d, tn), lambda j: (0, j)),
                  pl.BlockSpec((1, tn), lambda j: (0, j))],
        out_specs=pl.BlockSpec((m, tn), lambda j: (0, j)),
        out_shape=jax.ShapeDtypeStruct((m, n), F32),
        compiler_params=_params("arbitrary"),
        name="ada_mod",
    )(c_all, w_ada, b_ada.reshape(1, n))


def _head_norm_rope(th, g, cos, sin_signed):
    ms = jnp.mean(th * th, axis=-1, keepdims=True)
    y = th * lax.rsqrt(ms + RMS_EPS) * g
    return y * cos + pltpu.roll(y, HEAD_DIM // 2, 1) * sin_signed


_PROJ_CHUNK = 2 * HEAD_DIM


def _qkvu_kernel(x_ref, sh_ref, sc_ref, g1_ref, w_ref, gq_ref, gk_ref, cos_ref, sin_ref,
                 q_ref, k_ref, v_ref, u_ref, h_scr):
    h = _mod_norm(x_ref[...], g1_ref[...], sc_ref[...], sh_ref[...])
    h_scr[...] = h.reshape(h_scr.shape).astype(BF16)

    width = q_ref.shape[1]
    outs = ((q_ref, gq_ref), (k_ref, gk_ref), (v_ref, None), (u_ref, None))
    for i, (o_ref, g_ref) in enumerate(outs):
        for c in range(0, width, _PROJ_CHUNK):
            col = i * width + c
            acc = jnp.dot(h_scr[...], w_ref[:, col:col + _PROJ_CHUNK], preferred_element_type=F32)
            if g_ref is None:
                o_ref[:, c:c + _PROJ_CHUNK] = acc
            else:
                for hh in range(0, _PROJ_CHUNK, HEAD_DIM):
                    o_ref[:, c + hh:c + hh + HEAD_DIM] = _head_norm_rope(
                        acc[:, hh:hh + HEAD_DIM], g_ref[...], cos_ref[...], sin_ref[...])


def _group_specs(x3, gb, r):
    g_total, s, d = x3.shape
    n_g, n_t = g_total // gb, s // r
    return n_g, n_t, d


def _qkvu(x3, mod, g_norm1, w_in_bf, g_q, g_k, cos, sin_signed, gb, r):
    n_g, n_t, d = _group_specs(x3, gb, r)
    tm = gb * r
    tokens = x3.shape[0] * x3.shape[1]
    wq = ATTN_WIDTH
    x_spec = pl.BlockSpec((gb, r, d), lambda g, t: (g, t, 0))
    mod_spec = lambda k: pl.BlockSpec((gb, 1, d), lambda g, t: (g, 0, k))
    row_spec = pl.BlockSpec((1, d), lambda g, t: (0, 0))
    head_spec = pl.BlockSpec((1, HEAD_DIM), lambda g, t: (0, 0))
    tab_spec = pl.BlockSpec((tm, HEAD_DIM), lambda g, t: (t, 0))
    out_spec = pl.BlockSpec((tm, wq), lambda g, t: (g * n_t + t, 0))
    out_sds = jax.ShapeDtypeStruct((tokens, wq), F32)
    w_spec = pl.BlockSpec((d, 4 * wq), lambda g, t: (0, 0), pipeline_mode=pl.Buffered(1))
    return pl.pallas_call(
        _qkvu_kernel,
        grid=(n_g, n_t),
        in_specs=[x_spec, mod_spec(0), mod_spec(1), row_spec, w_spec,
                  head_spec, head_spec, tab_spec, tab_spec],
        out_specs=[out_spec] * 4,
        out_shape=[out_sds] * 4,
        scratch_shapes=[pltpu.VMEM((tm, d), BF16)],
        compiler_params=_params("arbitrary", "arbitrary"),
        name="qkvu_proj",
    )(x3, mod, mod, g_norm1.reshape(1, d), w_in_bf, g_q.reshape(1, HEAD_DIM),
      g_k.reshape(1, HEAD_DIM), cos, sin_signed)


def _topk_select(gate, valid, axis):
    n = gate.shape[axis]
    g = jnp.where(valid, gate, NEG_INF)
    idx = lax.broadcasted_iota(jnp.int32, gate.shape, axis)
    rank = jnp.zeros(gate.shape, jnp.int32)
    for m in range(n):
        gm = g[m:m + 1, :] if axis == 0 else g[:, m:m + 1]
        beats = (gm > g) | ((gm == g) & (m < idx))
        rank = rank + beats.astype(jnp.int32)
    return valid & (rank < MOBA_TOPK) & (jnp.abs(g) < float("inf"))


def _moba_prompt_kernel(q_ref, k_ref, v_ref, o_ref, kb_scr, vt_scr, kmean_scr, bias_scr, s_scr, *, nb):
    qi = pl.program_id(2)
    blk = MOBA_BLOCK

    @pl.when(qi == 0)
    def _():
        for n in range(nb):
            kn = k_ref[n * blk:(n + 1) * blk, :]
            kb_scr[n] = kn.astype(BF16)
            kmean_scr[n:n + 1, :] = jnp.mean(kn, axis=0, keepdims=True)
            vt_scr[n] = v_ref[n * blk:(n + 1) * blk, :].T.astype(BF16)
        kmean = kmean_scr[...]
        for qb_i in range(nb):
            gate = lax.dot_general(kmean, q_ref[qb_i * blk:(qb_i + 1) * blk, :],
                                   (((1,), (1,)), ((), ())),
                                   precision=lax.Precision.HIGHEST, preferred_element_type=F32)
            n_idx = lax.broadcasted_iota(jnp.int32, gate.shape, 0)
            sel = _topk_select(gate, n_idx < qb_i, axis=0)
            bias_scr[qb_i] = jnp.where(sel | (n_idx >= qb_i), 0.0, NEG_INF)

    qb = q_ref[pl.ds(pl.multiple_of(qi * blk, blk), blk), :].astype(BF16)
    exp2_scale = HEAD_DIM ** -0.5 * _LOG2_E
    kpos = lax.broadcasted_iota(jnp.int32, (blk, blk), 0)
    qpos = lax.broadcasted_iota(jnp.int32, (blk, blk), 1)

    def fold(x, op):
        return op(x.reshape(blk // 8, 8, blk), axis=0)

    n_tiles = qi + 1
    spans = []
    base = 0
    for width in _TILE_GROUPS[:-1]:
        trips = (n_tiles - base) // width
        spans.append((width, base, trips))
        base = base + trips * width
    spans.append((_TILE_GROUPS[-1], base, (n_tiles - base + 1) // _TILE_GROUPS[-1]))

    def pass1(width, base):
        def body(jj, m8):
            for t in range(width):
                j = base + width * jj + t
                jm = jnp.minimum(j, nb - 1)
                s = lax.dot_general(kb_scr[jm], qb, (((1,), (1,)), ((), ())),
                                    preferred_element_type=F32)
                s = jnp.where(kpos + (j - qi) * blk <= qpos, s + bias_scr[qi, pl.ds(jm, 1), :], NEG_INF)
                s_scr[j] = s
                m8 = jnp.maximum(m8, fold(s, jnp.max))
            return m8
        return body

    m8 = jnp.full((8, blk), NEG_INF, F32)
    for width, base, trips in spans:
        m8 = lax.fori_loop(0, trips, pass1(width, base), m8)
    m = jnp.max(m8, axis=0, keepdims=True)

    def pass2(width, base):
        def body(jj, carry):
            l8, acc = carry
            for t in range(width):
                j = base + width * jj + t
                p = jnp.exp2((s_scr[j] - m) * exp2_scale)
                l8 = l8 + fold(p, jnp.sum)
                acc = acc + jnp.dot(vt_scr[jnp.minimum(j, nb - 1)], p.astype(BF16),
                                    preferred_element_type=F32)
            return l8, acc
        return body

    l8 = jnp.zeros((8, blk), F32)
    acc = jnp.zeros((HEAD_DIM, blk), F32)
    for width, base, trips in spans:
        l8, acc = lax.fori_loop(0, trips, pass2(width, base), (l8, acc))
    l = jnp.sum(l8, axis=0, keepdims=True)
    o_ref[...] = (acc / l).T.astype(o_ref.dtype)


def _moba_prompt(q, k, v, n_seq, seq):
    nb = seq // MOBA_BLOCK
    blk = MOBA_BLOCK
    kv_spec = pl.BlockSpec((seq, HEAD_DIM), lambda b, h, i: (b, h))
    q_spec = pl.BlockSpec((blk, HEAD_DIM), lambda b, h, i: (b * nb + i, h))
    return pl.pallas_call(
        functools.partial(_moba_prompt_kernel, nb=nb),
        grid=(n_seq, N_HEADS, nb),
        in_specs=[kv_spec, kv_spec, kv_spec],
        out_specs=q_spec,
        out_shape=jax.ShapeDtypeStruct(q.shape, BF16),
        scratch_shapes=[pltpu.VMEM((nb, blk, HEAD_DIM), BF16),
                        pltpu.VMEM((nb, HEAD_DIM, blk), BF16),
                        pltpu.VMEM((nb, HEAD_DIM), F32),
                        pltpu.VMEM((nb, nb, blk), F32),
                        pltpu.VMEM((nb + 1, blk, blk), F32)],
        compiler_params=_params("arbitrary", "arbitrary", "arbitrary"),
        name="moba_prompt",
    )(q, k, v)


def _page_copy_groups(cache_hbm, pt_ref, buf, sem, slot, b, first_page, hp, n_pages, ppb, wanted):
    groups = []
    for p0 in range(0, hp, ppb):
        pgs = [pt_ref[b * n_pages + first_page + p0 + i] for i in range(ppb)]
        for h in range(N_HEADS):
            pred = None if wanted is None else wanted(b, (first_page + p0) // ppb, h)
            groups.append((pred, [pltpu.make_async_copy(cache_hbm.at[pgs[i], :, h, :],
                                                        buf.at[slot, p0 + i, h], sem.at[slot])
                                  for i in range(ppb)]))
    return groups


def _for_each_copy(groups, method):
    for pred, copies in groups:
        def run(copies=copies):
            for c in copies:
                getattr(c, method)()
        if pred is None:
            run()
        else:
            pl.when(pred)(run)


def _paged_halves(pt_ref, cache_hbm, buf, sem, hp, n_pages, ppb, compute, wanted=None):
    b, step = pl.program_id(0), pl.program_id(1)
    n_b, n_steps = pl.num_programs(0), pl.num_programs(1)
    is_first = jnp.logical_and(b == 0, step == 0)
    is_last = jnp.logical_and(b == n_b - 1, step == n_steps - 1)
    wrap = step == n_steps - 1
    nxt_b = jnp.where(wrap, b + 1, b)
    nxt_step = jnp.where(wrap, 0, step + 1)

    def groups(bb, ss, slot):
        return _page_copy_groups(cache_hbm, pt_ref, buf, sem, slot, bb, (2 * ss + slot) * hp, hp,
                                 n_pages, ppb, wanted)

    @pl.when(is_first)
    def _():
        for slot in range(2):
            _for_each_copy(groups(b, step, slot), "start")

    for slot in range(2):
        _for_each_copy(groups(b, step, slot), "wait")
        compute(slot, (2 * step + slot) * hp)

        @pl.when(jnp.logical_not(is_last))
        def _():
            _for_each_copy(groups(nxt_b, nxt_step, slot), "start")


def _moba_sample_k_kernel(pt_ref, q_ref, knew_ref, vnew_ref, cache_hbm, p_ref, oown_ref, linv_ref,
                          used_ref, buf, sem, s_scr, kpart_scr, *, hp, page, n_blocks):
    step = pl.program_id(1)
    n_steps = pl.num_programs(1)
    dq = q_ref.shape[0]
    rows = N_HEADS * dq
    scale = HEAD_DIM ** -0.5
    pages_per_block = MOBA_BLOCK // page
    n_pages = n_blocks * pages_per_block

    qh = [q_ref[:, h * HEAD_DIM:(h + 1) * HEAD_DIM] for h in range(N_HEADS)]
    qh_bf = [x.astype(BF16) for x in qh]

    @pl.when(step == 0)
    def _():
        kpart_scr[...] = jnp.zeros_like(kpart_scr)

    def compute(slot, first_page):
        for p in range(hp):
            gpage = first_page + p
            blk = gpage // pages_per_block
            for h in range(N_HEADS):
                kh = buf[slot, p, h]
                kpart_scr[blk, h] += jnp.sum(kh.reshape(page // 8, 8, HEAD_DIM), axis=0)
                s = lax.dot_general(qh_bf[h], kh.astype(BF16), (((1,), (1,)), ((), ())),
                                    preferred_element_type=F32) * scale
                s_scr[gpage, h * dq:(h + 1) * dq, :] = s

    _paged_halves(pt_ref, cache_hbm, buf, sem, hp, n_pages, pages_per_block, compute)

    @pl.when(step == n_steps - 1)
    def _():
        gates = []
        for h in range(N_HEADS):
            kmean_h = jnp.sum(kpart_scr[:, h], axis=1) * (1.0 / MOBA_BLOCK)
            gates.append(lax.dot_general(qh[h], kmean_h, (((1,), (1,)), ((), ())),
                                         precision=lax.Precision.HIGHEST,
                                         preferred_element_type=F32))
        gate = jnp.concatenate(gates, axis=0)
        sel = _topk_select(gate, jnp.ones(gate.shape, jnp.bool_), axis=1)
        bias = jnp.where(sel, 0.0, NEG_INF)
        used_ref[0] = jnp.concatenate(
            [jnp.max(sel[h * dq:(h + 1) * dq, :].astype(jnp.int32), axis=0, keepdims=True)
             for h in range(N_HEADS)], axis=0)

        s_own = []
        for h in range(N_HEADS):
            kn = knew_ref[:, h * HEAD_DIM:(h + 1) * HEAD_DIM].astype(BF16)
            s_own.append(lax.dot_general(qh_bf[h], kn, (((1,), (1,)), ((), ())),
                                         preferred_element_type=F32) * scale)
        s_own = jnp.concatenate(s_own, axis=0)
        row = lax.broadcasted_iota(jnp.int32, s_own.shape, 0) % dq
        col = lax.broadcasted_iota(jnp.int32, s_own.shape, 1)
        s_own = jnp.where(col <= row, s_own, NEG_INF)

        m_vec = jnp.full((rows, page), NEG_INF, F32)
        for g in range(n_pages):
            n = g // pages_per_block
            m_vec = jnp.maximum(m_vec, s_scr[g] + bias[:, n:n + 1])
        m = jnp.maximum(jnp.max(m_vec, axis=1, keepdims=True), jnp.max(s_own, axis=1, keepdims=True))
        p_own = jnp.exp(s_own - m)
        l_vec = jnp.zeros((rows, page), F32)
        for g in range(n_pages):
            n = g // pages_per_block
            pb = jnp.exp(s_scr[g] + bias[:, n:n + 1] - m)
            l_vec = l_vec + pb
            p_ref[0, :, g * page:(g + 1) * page] = pb
        l = jnp.sum(l_vec, axis=1, keepdims=True) + jnp.sum(p_own, axis=1, keepdims=True)
        linv_ref[0] = jnp.broadcast_to(1.0 / l, (rows, HEAD_DIM))
        for h in range(N_HEADS):
            vn = vnew_ref[:, h * HEAD_DIM:(h + 1) * HEAD_DIM].astype(BF16)
            oown_ref[0, h * dq:(h + 1) * dq, :] = jnp.dot(
                p_own[h * dq:(h + 1) * dq, :].astype(BF16), vn, preferred_element_type=F32)


def _moba_sample_v_kernel(pt_ref, used_ref, p_ref, oown_ref, linv_ref, cache_hbm, o_ref, buf, sem,
                          acc_scr, *, hp, page, n_pages):
    step = pl.program_id(1)
    n_steps = pl.num_programs(1)
    dq = o_ref.shape[0]
    ppb = MOBA_BLOCK // page
    n_blocks = n_pages // ppb

    @pl.when(jnp.logical_and(pl.program_id(0) == 0, step == 0))
    def _():
        buf[...] = jnp.zeros_like(buf)

    def wanted(b, blk, h):
        return used_ref[(b * N_HEADS + h) * n_blocks + blk] != 0

    @pl.when(step == 0)
    def _():
        acc_scr[...] = oown_ref[0]

    def compute(slot, first_page):
        for h in range(N_HEADS):
            tot = jnp.zeros((dq, HEAD_DIM), F32)
            for p in range(hp):
                col0 = (slot * hp + p) * page
                ph = p_ref[0, h * dq:(h + 1) * dq, col0:col0 + page].astype(BF16)
                tot = tot + jnp.dot(ph, buf[slot, p, h].astype(BF16), preferred_element_type=F32)
            acc_scr[h * dq:(h + 1) * dq, :] += tot

    _paged_halves(pt_ref, cache_hbm, buf, sem, hp, n_pages, ppb, compute, wanted)

    @pl.when(step == n_steps - 1)
    def _():
        for h in range(N_HEADS):
            rows_h = slice(h * dq, (h + 1) * dq)
            o_ref[:, h * HEAD_DIM:(h + 1) * HEAD_DIM] = acc_scr[rows_h, :] * linv_ref[0, rows_h, :]


def _moba_sample(q, k_new, v_new, cache_k, cache_v, page_table, hp=8):
    dec_b, n_pages = page_table.shape
    page = cache_k.shape[1]
    dq = q.shape[0] // dec_b
    past = n_pages * page
    n_blocks = past // MOBA_BLOCK
    while n_pages % (2 * hp):
        hp //= 2
    n_steps = n_pages // (2 * hp)
    rows = N_HEADS * dq
    pt = page_table.reshape(-1)

    new_spec = pl.BlockSpec((dq, ATTN_WIDTH), lambda b, s, pt_ref: (b, 0))
    any_spec = pl.BlockSpec(memory_space=pl.ANY)
    stat_spec = pl.BlockSpec((1, rows, HEAD_DIM), lambda b, s, pt_ref: (b, 0, 0))
    stat_sds = jax.ShapeDtypeStruct((dec_b, rows, HEAD_DIM), F32)
    assert hp % (MOBA_BLOCK // page) == 0, "a page half must hold whole key blocks"
    page_buf = pltpu.VMEM((2, hp, N_HEADS, page, HEAD_DIM), F32)
    probs, o_own, l_inv, used = pl.pallas_call(
        functools.partial(_moba_sample_k_kernel, hp=hp, page=page, n_blocks=n_blocks),
        grid_spec=pltpu.PrefetchScalarGridSpec(
            num_scalar_prefetch=1,
            grid=(dec_b, n_steps),
            in_specs=[new_spec, new_spec, new_spec, any_spec],
            out_specs=[pl.BlockSpec((1, rows, past), lambda b, s, pt_ref: (b, 0, 0)),
                       stat_spec, stat_spec,
                       pl.BlockSpec((1, N_HEADS, n_blocks), lambda b, s, pt_ref: (b, 0, 0))],
            scratch_shapes=[page_buf, pltpu.SemaphoreType.DMA((2,)),
                            pltpu.VMEM((n_pages, rows, page), F32),
                            pltpu.VMEM((n_blocks, N_HEADS, 8, HEAD_DIM), F32)]),
        out_shape=[jax.ShapeDtypeStruct((dec_b, rows, past), F32), stat_sds, stat_sds,
                   jax.ShapeDtypeStruct((dec_b, N_HEADS, n_blocks), jnp.int32)],
        compiler_params=_params("arbitrary", "arbitrary"),
        name="moba_sample_k",
    )(pt, q, k_new, v_new, cache_k)

    return pl.pallas_call(
        functools.partial(_moba_sample_v_kernel, hp=hp, page=page, n_pages=n_pages),
        grid_spec=pltpu.PrefetchScalarGridSpec(
            num_scalar_prefetch=2,
            grid=(dec_b, n_steps),
            in_specs=[pl.BlockSpec((1, rows, 2 * hp * page), lambda b, s, *_: (b, 0, s)),
                      pl.BlockSpec((1, rows, HEAD_DIM), lambda b, s, *_: (b, 0, 0)),
                      pl.BlockSpec((1, rows, HEAD_DIM), lambda b, s, *_: (b, 0, 0)),
                      any_spec],
            out_specs=pl.BlockSpec((dq, ATTN_WIDTH), lambda b, s, *_: (b, 0)),
            scratch_shapes=[page_buf, pltpu.SemaphoreType.DMA((2,)),
                            pltpu.VMEM((rows, HEAD_DIM), F32)]),
        out_shape=jax.ShapeDtypeStruct(q.shape, F32),
        compiler_params=_params("arbitrary", "arbitrary"),
        name="moba_sample_v",
    )(pt, used.reshape(-1), probs, o_own, l_inv, cache_v)


def _pool_kernel(u_ref, halo_ref, w_ref, ls_ref, o_ref, ext_scr, *, pos0):
    t = pl.program_id(1)
    gb, r, width = u_ref.shape
    grp = width // len(POOL_WINDOWS)

    @pl.when(t == 0)
    def _():
        ext_scr[:, 0:POOL_HALO, :] = halo_ref[...]

    @pl.when(t > 0)
    def _():
        ext_scr[:, 0:POOL_HALO, :] = ext_scr[:, r:r + POOL_HALO, :]

    ext_scr[:, POOL_HALO:POOL_HALO + r, :] = u_ref[...]

    pos = pos0 + t * r + lax.broadcasted_iota(jnp.int32, (1, r, grp), 1)
    for g, w in enumerate(POOL_WINDOWS):
        cols = slice(g * grp, (g + 1) * grp)
        tot = ext_scr[:, POOL_HALO:POOL_HALO + r, cols]
        for d in range(1, w):
            tot = tot + ext_scr[:, POOL_HALO - d:POOL_HALO - d + r, cols]
        cnt = jnp.minimum(w, pos + 1).astype(F32)
        y = tot / cnt - u_ref[:, :, cols]
        y = jnp.dot(y.reshape(gb * r, grp).astype(BF16), w_ref[g], preferred_element_type=F32)
        o_ref[:, cols] = (y * ls_ref[:, cols]).astype(o_ref.dtype)


def _pool(u, halo, w_pool_bf, ls_pool, n_seq, seq, gb, r, pos0):
    width = u.shape[1]
    n_g, n_t = n_seq // gb, seq // r
    u3 = u.reshape(n_seq, seq, width)
    return pl.pallas_call(
        functools.partial(_pool_kernel, pos0=pos0),
        grid=(n_g, n_t),
        in_specs=[pl.BlockSpec((gb, r, width), lambda g, t: (g, t, 0)),
                  pl.BlockSpec((gb, POOL_HALO, width), lambda g, t: (g, 0, 0)),
                  pl.BlockSpec(w_pool_bf.shape, lambda g, t: (0, 0, 0)),
                  pl.BlockSpec((1, width), lambda g, t: (0, 0))],
        out_specs=pl.BlockSpec((gb * r, width), lambda g, t: (g * n_t + t, 0)),
        out_shape=jax.ShapeDtypeStruct(u.shape, BF16),
        scratch_shapes=[pltpu.VMEM((gb, r + POOL_HALO, width), F32)],
        compiler_params=_params("arbitrary", "arbitrary"),
        name="pool_mixer",
    )(u3, halo, w_pool_bf, ls_pool.reshape(1, width))


def _merge_kernel(x_ref, sh_ref, sc_ref, gt_ref, g1_ref, attn_ref, pool_ref,
                  wga_ref, wgb_ref, woa_ref, wob_ref, wout_ref, o_ref, h_scr, acc_scr):
    c = pl.program_id(2)

    @pl.when(c == 0)
    def _():
        h = _mod_norm(x_ref[...], g1_ref[...], sc_ref[...], sh_ref[...])
        h_scr[...] = h.reshape(h_scr.shape).astype(BF16)
        acc_scr[...] = jnp.zeros_like(acc_scr)

    h = h_scr[...]
    g_a = jax.nn.sigmoid(jnp.dot(h, wga_ref[...], preferred_element_type=F32))
    g_b = jax.nn.sigmoid(jnp.dot(h, wgb_ref[...], preferred_element_type=F32))
    y_a = jnp.dot(attn_ref[...].astype(BF16), woa_ref[...], preferred_element_type=F32)
    y_b = jnp.dot(pool_ref[...].astype(BF16), wob_ref[...], preferred_element_type=F32)
    mix_in = (g_a * y_a + g_b * y_b).astype(BF16)
    acc_scr[...] += jnp.dot(mix_in, wout_ref[...], preferred_element_type=F32)

    @pl.when(c == pl.num_programs(2) - 1)
    def _():
        o_ref[...] = x_ref[...] + gt_ref[...] * acc_scr[...].reshape(o_ref.shape)


def _merge(x3, mod, g_norm1, attn, pooled, w_in_bf, w_o_attn_bf, w_o_pool_bf, w_out_bf, gb, r, tn=512):
    n_g, n_t, d = _group_specs(x3, gb, r)
    tm = gb * r
    aw = attn.shape[1]
    pw = pooled.shape[1]
    gate_col0 = (w_in_bf.shape[1] - 2 * d) // tn
    n_c = d // tn
    x_spec = pl.BlockSpec((gb, r, d), lambda g, t, c: (g, t, 0))
    mod_spec = lambda k: pl.BlockSpec((gb, 1, d), lambda g, t, c: (g, 0, k))
    return pl.pallas_call(
        _merge_kernel,
        grid=(n_g, n_t, n_c),
        in_specs=[x_spec, mod_spec(0), mod_spec(1), mod_spec(2),
                  pl.BlockSpec((1, d), lambda g, t, c: (0, 0)),
                  pl.BlockSpec((tm, aw), lambda g, t, c: (g * n_t + t, 0)),
                  pl.BlockSpec((tm, pw), lambda g, t, c: (g * n_t + t, 0)),
                  pl.BlockSpec((d, tn), lambda g, t, c: (0, gate_col0 + c)),
                  pl.BlockSpec((d, tn), lambda g, t, c: (0, gate_col0 + n_c + c)),
                  pl.BlockSpec((aw, tn), lambda g, t, c: (0, c)),
                  pl.BlockSpec((pw, tn), lambda g, t, c: (0, c)),
                  pl.BlockSpec((tn, d), lambda g, t, c: (c, 0))],
        out_specs=x_spec,
        out_shape=jax.ShapeDtypeStruct(x3.shape, F32),
        scratch_shapes=[pltpu.VMEM((tm, d), BF16), pltpu.VMEM((tm, d), F32)],
        compiler_params=_params("arbitrary", "arbitrary", "arbitrary"),
        name="merge_out",
    )(x3, mod, mod, mod, g_norm1.reshape(1, d), attn, pooled,
      w_in_bf, w_in_bf, w_o_attn_bf, w_o_pool_bf, w_out_bf)


def _ffn_kernel(x_ref, sh_ref, sc_ref, gt_ref, g2_ref, wa_ref, wb_ref, wo_ref, o_ref, h_scr, acc_scr):
    c = pl.program_id(2)

    @pl.when(c == 0)
    def _():
        h = _mod_norm(x_ref[...], g2_ref[...], sc_ref[...], sh_ref[...])
        h_scr[...] = h.reshape(h_scr.shape).astype(BF16)
        acc_scr[...] = jnp.zeros_like(acc_scr)

    h = h_scr[...]
    a = jnp.dot(h, wa_ref[...], preferred_element_type=F32)
    b = jnp.dot(h, wb_ref[...], preferred_element_type=F32)
    hid = (jax.nn.silu(a) * b).astype(BF16)
    acc_scr[...] += jnp.dot(hid, wo_ref[...], preferred_element_type=F32)

    @pl.when(c == pl.num_programs(2) - 1)
    def _():
        o_ref[...] = x_ref[...] + gt_ref[...] * acc_scr[...].reshape(o_ref.shape)


def _ffn(x3, mod, g_norm2, w_ffn_in_bf, w_ffn_out_bf, gb, r, tn=512):
    n_g, n_t, d = _group_specs(x3, gb, r)
    tm = gb * r
    hidden = w_ffn_out_bf.shape[0]
    n_c = hidden // tn
    x_spec = pl.BlockSpec((gb, r, d), lambda g, t, c: (g, t, 0))
    mod_spec = lambda k: pl.BlockSpec((gb, 1, d), lambda g, t, c: (g, 0, k))
    return pl.pallas_call(
        _ffn_kernel,
        grid=(n_g, n_t, n_c),
        in_specs=[x_spec, mod_spec(3), mod_spec(4), mod_spec(5),
                  pl.BlockSpec((1, d), lambda g, t, c: (0, 0)),
                  pl.BlockSpec((d, tn), lambda g, t, c: (0, c)),
                  pl.BlockSpec((d, tn), lambda g, t, c: (0, n_c + c)),
                  pl.BlockSpec((tn, d), lambda g, t, c: (c, 0))],
        out_specs=x_spec,
        out_shape=jax.ShapeDtypeStruct(x3.shape, F32),
        scratch_shapes=[pltpu.VMEM((tm, d), BF16), pltpu.VMEM((tm, d), F32)],
        compiler_params=_params("arbitrary", "arbitrary", "arbitrary"),
        name="ffn_swiglu",
    )(x3, mod, mod, mod, g_norm2.reshape(1, d), w_ffn_in_bf, w_ffn_in_bf, w_ffn_out_bf)


def _rope_tables(pos):
    half = HEAD_DIM // 2
    inv = ROPE_THETA ** (-jnp.arange(half, dtype=F32) / half)
    ang = pos.astype(F32)[:, None] * inv[None, :]
    cos, sin = jnp.cos(ang), jnp.sin(ang)
    return jnp.concatenate([cos, cos], axis=-1), jnp.concatenate([-sin, sin], axis=-1)


def _pick_rows(seq, target=512):
    r = min(seq, target)
    while seq % r:
        r //= 2
    return r


def kernel(x_prompt, x_sample, c_prompt, c_sample, cache_k, cache_v, state_pool, page_table, w_ada, b_ada, g_norm1, g_norm2, w_in, g_qnorm, g_knorm, w_pool, ls_pool, w_o_attn, w_o_pool, w_out, w_ffn_in, w_ffn_out):
    n_p, seq, d = x_prompt.shape
    n_s, dq, _ = x_sample.shape
    past = page_table.shape[1] * cache_k.shape[1]
    pool_w = state_pool.shape[2]
    n_state = state_pool.shape[1]

    w_in_bf = w_in.astype(BF16)
    w_pool_bf = w_pool.astype(BF16)
    w_o_attn_bf = w_o_attn.astype(BF16)
    w_o_pool_bf = w_o_pool.astype(BF16)
    w_out_bf = w_out.astype(BF16)
    w_ffn_in_bf = w_ffn_in.astype(BF16)
    w_ffn_out_bf = w_ffn_out.astype(BF16)

    n_c = n_p + n_s
    c_all = jnp.concatenate([c_prompt, c_sample], axis=0)
    c_all = jnp.pad(c_all, ((0, (-n_c) % 8), (0, 0)))
    mod = _ada(c_all, w_ada, b_ada)
    mod_p = mod[:n_p].reshape(n_p, 1, 6 * d)
    mod_s = mod[n_p:n_c].reshape(n_s, 1, 6 * d)

    r_p = _pick_rows(seq)
    cos_p, sin_p = _rope_tables(jnp.arange(seq))
    q_p, k_p, v_p, u_p = _qkvu(x_prompt, mod_p, g_norm1, w_in_bf, g_qnorm, g_knorm, cos_p, sin_p, 1, r_p)
    attn_p = _moba_prompt(q_p, k_p, v_p, n_p, seq)
    pooled_p = _pool(u_p, jnp.zeros((n_p, POOL_HALO, pool_w), F32), w_pool_bf, ls_pool,
                     n_p, seq, 1, r_p, 0)
    x1_p = _merge(x_prompt, mod_p, g_norm1, attn_p, pooled_p, w_in_bf, w_o_attn_bf, w_o_pool_bf,
                  w_out_bf, 1, r_p)
    y_p = _ffn(x1_p, mod_p, g_norm2, w_ffn_in_bf, w_ffn_out_bf, 1, r_p)

    cos_s, sin_s = _rope_tables(past + jnp.arange(dq))
    cos_s, sin_s = jnp.tile(cos_s, (n_s, 1)), jnp.tile(sin_s, (n_s, 1))
    q_s, k_s, v_s, u_s = _qkvu(x_sample, mod_s, g_norm1, w_in_bf, g_qnorm, g_knorm, cos_s, sin_s, n_s, dq)
    attn_s = _moba_sample(q_s, k_s, v_s, cache_k, cache_v, page_table)
    halo_s = jnp.pad(state_pool, ((0, 0), (POOL_HALO - n_state, 0), (0, 0)))
    pooled_s = _pool(u_s, halo_s, w_pool_bf, ls_pool, n_s, dq, n_s, dq, past)
    x1_s = _merge(x_sample, mod_s, g_norm1, attn_s, pooled_s, w_in_bf, w_o_attn_bf, w_o_pool_bf,
                  w_out_bf, n_s, dq)
    y_s = _ffn(x1_s, mod_s, g_norm2, w_ffn_in_bf, w_ffn_out_bf, n_s, dq)

    k_prompt = k_p.reshape(n_p, seq, N_HEADS, HEAD_DIM)
    v_prompt = v_p.reshape(n_p, seq, N_HEADS, HEAD_DIM)
    pool_prompt = u_p.reshape(n_p, seq, pool_w)[:, seq - n_state:]
    k_sample = k_s.reshape(n_s, dq, N_HEADS, HEAD_DIM)
    v_sample = v_s.reshape(n_s, dq, N_HEADS, HEAD_DIM)
    pool_sample = jnp.concatenate([state_pool, u_s.reshape(n_s, dq, pool_w)], axis=1)[:, -n_state:]
    return (y_p, y_s, k_prompt, v_prompt, pool_prompt, k_sample, v_sample, pool_sample)
```

```python
import functools

import jax
import jax.numpy as jnp
from jax import lax
from jax.experimental import pallas as pl
from jax.experimental.pallas import tpu as pltpu

N_HEADS = 8
HEAD_DIM = 128
ATTN_WIDTH = N_HEADS * HEAD_DIM
MOBA_BLOCK = 256
MOBA_TOPK = 3
ROPE_THETA = 10000.0
POOL_WINDOWS = (2, 4, 8, 16)
POOL_HALO = 16
RMS_EPS = 1e-6
_TILE_GROUPS = (8, 4, 2)

F32 = jnp.float32
BF16 = jnp.bfloat16
NEG_INF = float("-inf")
_LOG2_E = 1.4426950408889634

_VMEM_LIMIT = 52 * 1024 * 1024


def _params(*sem):
    return pltpu.CompilerParams(dimension_semantics=sem, vmem_limit_bytes=_VMEM_LIMIT)


def _mod_norm(x, g, sc, sh):
    ms = jnp.mean(x * x, axis=-1, keepdims=True)
    y = x * lax.rsqrt(ms + RMS_EPS) * g
    return y * (1.0 + sc) + sh


def _ada_kernel(c_ref, w_ref, b_ref, o_ref):
    o_ref[...] = jnp.dot(c_ref[...].astype(BF16), w_ref[...].astype(BF16),
                         preferred_element_type=F32) + b_ref[...]


def _ada(c_all, w_ada, b_ada, tn=1024):
    m, d = c_all.shape
    n = w_ada.shape[1]
    return pl.pallas_call(
        _ada_kernel,
        grid=(n // tn,),
        in_specs=[pl.BlockSpec((m, d), lambda j: (0, 0)),
                  pl.BlockSpec((d, tn), lambda j: (0, j)),
                  pl.BlockSpec((1, tn), lambda j: (0, j))],
        out_specs=pl.BlockSpec((m, tn), lambda j: (0, j)),
        out_shape=jax.ShapeDtypeStruct((m, n), F32),
        compiler_params=_params("arbitrary"),
        name="ada_mod",
    )(c_all, w_ada, b_ada.reshape(1, n))


def _head_norm_rope(th, g, cos, sin_signed):
    ms = jnp.mean(th * th, axis=-1, keepdims=True)
    y = th * lax.rsqrt(ms + RMS_EPS) * g
    return y * cos + pltpu.roll(y, HEAD_DIM // 2, 1) * sin_signed


_PROJ_CHUNK = 2 * HEAD_DIM


def _qkvu_kernel(x_ref, sh_ref, sc_ref, g1_ref, w_ref, gq_ref, gk_ref, cos_ref, sin_ref,
                 q_ref, k_ref, v_ref, u_ref, h_scr):
    h = _mod_norm(x_ref[...], g1_ref[...], sc_ref[...], sh_ref[...])
    h_scr[...] = h.reshape(h_scr.shape).astype(BF16)

    width = q_ref.shape[1]
    outs = ((q_ref, gq_ref), (k_ref, gk_ref), (v_ref, None), (u_ref, None))
    for i, (o_ref, g_ref) in enumerate(outs):
        for c in range(0, width, _PROJ_CHUNK):
            col = i * width + c
            acc = jnp.dot(h_scr[...], w_ref[:, col:col + _PROJ_CHUNK], preferred_element_type=F32)
            if g_ref is None:
                o_ref[:, c:c + _PROJ_CHUNK] = acc
            else:
                for hh in range(0, _PROJ_CHUNK, HEAD_DIM):
                    o_ref[:, c + hh:c + hh + HEAD_DIM] = _head_norm_rope(
                        acc[:, hh:hh + HEAD_DIM], g_ref[...], cos_ref[...], sin_ref[...])


def _group_specs(x3, gb, r):
    g_total, s, d = x3.shape
    n_g, n_t = g_total // gb, s // r
    return n_g, n_t, d


def _qkvu(x3, mod, g_norm1, w_in_bf, g_q, g_k, cos, sin_signed, gb, r):
    n_g, n_t, d = _group_specs(x3, gb, r)
    tm = gb * r
    tokens = x3.shape[0] * x3.shape[1]
    wq = ATTN_WIDTH
    x_spec = pl.BlockSpec((gb, r, d), lambda g, t: (g, t, 0))
    mod_spec = lambda k: pl.BlockSpec((gb, 1, d), lambda g, t: (g, 0, k))
    row_spec = pl.BlockSpec((1, d), lambda g, t: (0, 0))
    head_spec = pl.BlockSpec((1, HEAD_DIM), lambda g, t: (0, 0))
    tab_spec = pl.BlockSpec((tm, HEAD_DIM), lambda g, t: (t, 0))
    out_spec = pl.BlockSpec((tm, wq), lambda g, t: (g * n_t + t, 0))
    out_sds = jax.ShapeDtypeStruct((tokens, wq), F32)
    w_spec = pl.BlockSpec((d, 4 * wq), lambda g, t: (0, 0), pipeline_mode=pl.Buffered(1))
    return pl.pallas_call(
        _qkvu_kernel,
        grid=(n_g, n_t),
        in_specs=[x_spec, mod_spec(0), mod_spec(1), row_spec, w_spec,
                  head_spec, head_spec, tab_spec, tab_spec],
        out_specs=[out_spec] * 4,
        out_shape=[out_sds] * 4,
        scratch_shapes=[pltpu.VMEM((tm, d), BF16)],
        compiler_params=_params("arbitrary", "arbitrary"),
        name="qkvu_proj",
    )(x3, mod, mod, g_norm1.reshape(1, d), w_in_bf, g_q.reshape(1, HEAD_DIM),
      g_k.reshape(1, HEAD_DIM), cos, sin_signed)


def _topk_select(gate, valid, axis):
    n = gate.shape[axis]
    g = jnp.where(valid, gate, NEG_INF)
    idx = lax.broadcasted_iota(jnp.int32, gate.shape, axis)
    rank = jnp.zeros(gate.shape, jnp.int32)
    for m in range(n):
        gm = g[m:m + 1, :] if axis == 0 else g[:, m:m + 1]
        beats = (gm > g) | ((gm == g) & (m < idx))
        rank = rank + beats.astype(jnp.int32)
    return valid & (rank < MOBA_TOPK) & (jnp.abs(g) < float("inf"))


def _moba_prompt_kernel(q_ref, k_ref, v_ref, o_ref, kb_scr, vt_scr, kmean_scr, bias_scr, s_scr, *, nb):
    qi = pl.program_id(2)
    blk = MOBA_BLOCK

    @pl.when(qi == 0)
    def _():
        for n in range(nb):
            kn = k_ref[n * blk:(n + 1) * blk, :]
            kb_scr[n] = kn.astype(BF16)
            kmean_scr[n:n + 1, :] = jnp.mean(kn, axis=0, keepdims=True)
            vt_scr[n] = v_ref[n * blk:(n + 1) * blk, :].T.astype(BF16)
        kmean = kmean_scr[...]
        for qb_i in range(nb):
            gate = lax.dot_general(kmean, q_ref[qb_i * blk:(qb_i + 1) * blk, :],
                                   (((1,), (1,)), ((), ())),
                                   precision=lax.Precision.HIGHEST, preferred_element_type=F32)
            n_idx = lax.broadcasted_iota(jnp.int32, gate.shape, 0)
            sel = _topk_select(gate, n_idx < qb_i, axis=0)
            bias_scr[qb_i] = jnp.where(sel | (n_idx >= qb_i), 0.0, NEG_INF)

    qb = q_ref[pl.ds(pl.multiple_of(qi * blk, blk), blk), :].astype(BF16)
    exp2_scale = HEAD_DIM ** -0.5 * _LOG2_E
    kpos = lax.broadcasted_iota(jnp.int32, (blk, blk), 0)
    qpos = lax.broadcasted_iota(jnp.int32, (blk, blk), 1)

    def fold(x, op):
        return op(x.reshape(blk // 8, 8, blk), axis=0)

    n_tiles = qi + 1
    spans = []
    base = 0
    for width in _TILE_GROUPS[:-1]:
        trips = (n_tiles - base) // width
        spans.append((width, base, trips))
        base = base + trips * width
    spans.append((_TILE_GROUPS[-1], base, (n_tiles - base + 1) // _TILE_GROUPS[-1]))

    def pass1(width, base):
        def body(jj, m8):
            for t in range(width):
                j = base + width * jj + t
                jm = jnp.minimum(j, nb - 1)
                s = lax.dot_general(kb_scr[jm], qb, (((1,), (1,)), ((), ())),
                                    preferred_element_type=F32)
                s = jnp.where(kpos + (j - qi) * blk <= qpos, s + bias_scr[qi, pl.ds(jm, 1), :], NEG_INF)
                s_scr[j] = s
                m8 = jnp.maximum(m8, fold(s, jnp.max))
            return m8
        return body

    m8 = jnp.full((8, blk), NEG_INF, F32)
    for width, base, trips in spans:
        m8 = lax.fori_loop(0, trips, pass1(width, base), m8)
    m = jnp.max(m8, axis=0, keepdims=True)

    def pass2(width, base):
        def body(jj, carry):
            l8, acc = carry
            for t in range(width):
                j = base + width * jj + t
                p = jnp.exp2((s_scr[j] - m) * exp2_scale)
                l8 = l8 + fold(p, jnp.sum)
                acc = acc + jnp.dot(vt_scr[jnp.minimum(j, nb - 1)], p.astype(BF16),
                                    preferred_element_type=F32)
            return l8, acc
        return body

    l8 = jnp.zeros((8, blk), F32)
    acc = jnp.zeros((HEAD_DIM, blk), F32)
    for width, base, trips in spans:
        l8, acc = lax.fori_loop(0, trips, pass2(width, base), (l8, acc))
    l = jnp.sum(l8, axis=0, keepdims=True)
    o_ref[...] = (acc / l).T.astype(o_ref.dtype)


def _moba_prompt(q, k, v, n_seq, seq):
    nb = seq // MOBA_BLOCK
    blk = MOBA_BLOCK
    kv_spec = pl.BlockSpec((seq, HEAD_DIM), lambda b, h, i: (b, h))
    q_spec = pl.BlockSpec((blk, HEAD_DIM), lambda b, h, i: (b * nb + i, h))
    return pl.pallas_call(
        functools.partial(_moba_prompt_kernel, nb=nb),
        grid=(n_seq, N_HEADS, nb),
        in_specs=[kv_spec, kv_spec, kv_spec],
        out_specs=q_spec,
        out_shape=jax.ShapeDtypeStruct(q.shape, BF16),
        scratch_shapes=[pltpu.VMEM((nb, blk, HEAD_DIM), BF16),
                        pltpu.VMEM((nb, HEAD_DIM, blk), BF16),
                        pltpu.VMEM((nb, HEAD_DIM), F32),
                        pltpu.VMEM((nb, nb, blk), F32),
                        pltpu.VMEM((nb + 1, blk, blk), F32)],
        compiler_params=_params("arbitrary", "arbitrary", "arbitrary"),
        name="moba_prompt",
    )(q, k, v)


def _page_copy_groups(cache_hbm, pt_ref, buf, sem, slot, b, first_page, hp, n_pages, ppb, wanted):
    groups = []
    for p0 in range(0, hp, ppb):
        pgs = [pt_ref[b * n_pages + first_page + p0 + i] for i in range(ppb)]
        for h in range(N_HEADS):
            pred = None if wanted is None else wanted(b, (first_page + p0) // ppb, h)
            groups.append((pred, [pltpu.make_async_copy(cache_hbm.at[pgs[i], :, h, :],
                                                        buf.at[slot, p0 + i, h], sem.at[slot])
                                  for i in range(ppb)]))
    return groups


def _for_each_copy(groups, method):
    for pred, copies in groups:
        def run(copies=copies):
            for c in copies:
                getattr(c, method)()
        if pred is None:
            run()
        else:
            pl.when(pred)(run)


def _paged_slots(pt_ref, cache_hbm, buf, sem, hp, n_pages, ppb, compute, wanted=None):
    n_slots = buf.shape[0]
    b, step = pl.program_id(0), pl.program_id(1)
    n_b, n_steps = pl.num_programs(0), pl.num_programs(1)
    is_first = jnp.logical_and(b == 0, step == 0)
    is_last = jnp.logical_and(b == n_b - 1, step == n_steps - 1)
    wrap = step == n_steps - 1
    nxt_b = jnp.where(wrap, b + 1, b)
    nxt_step = jnp.where(wrap, 0, step + 1)

    def groups(bb, ss, slot):
        return _page_copy_groups(cache_hbm, pt_ref, buf, sem, slot, bb, (n_slots * ss + slot) * hp,
                                 hp, n_pages, ppb, wanted)

    @pl.when(is_first)
    def _():
        for slot in range(n_slots):
            _for_each_copy(groups(b, step, slot), "start")

    for slot in range(n_slots):
        _for_each_copy(groups(b, step, slot), "wait")
        compute(slot, (n_slots * step + slot) * hp)

        @pl.when(jnp.logical_not(is_last))
        def _():
            _for_each_copy(groups(nxt_b, nxt_step, slot), "start")


def _moba_sample_k_kernel(pt_ref, q_ref, knew_ref, vnew_ref, cache_hbm, p_ref, oown_ref, linv_ref,
                          used_ref, buf, sem, s_scr, kpart_scr, qbd_scr, *, hp, page, n_blocks):
    step = pl.program_id(1)
    n_steps = pl.num_programs(1)
    dq = q_ref.shape[0]
    rows = N_HEADS * dq
    scale = HEAD_DIM ** -0.5
    pages_per_block = MOBA_BLOCK // page
    n_pages = n_blocks * pages_per_block

    qh = [q_ref[:, h * HEAD_DIM:(h + 1) * HEAD_DIM] for h in range(N_HEADS)]
    qh_bf = [x.astype(BF16) for x in qh]

    @pl.when(step == 0)
    def _():
        kpart_scr[...] = jnp.zeros_like(kpart_scr)
        q_rep = jnp.concatenate([q_ref[...]] * N_HEADS
                                + [jnp.zeros((qbd_scr.shape[1] - rows, ATTN_WIDTH), F32)], axis=0)
        row_head = lax.broadcasted_iota(jnp.int32, q_rep.shape, 0) // dq
        lane_head = lax.broadcasted_iota(jnp.int32, q_rep.shape, 1) // HEAD_DIM
        qbd_scr[...] = jnp.where(row_head == lane_head, q_rep, 0.0).T.astype(BF16)

    def compute(slot, first_page):
        heads = [[buf[slot, p, h] for h in range(N_HEADS)] for p in range(hp)]
        for p in range(hp):
            blk = (first_page + p) // pages_per_block
            for h in range(N_HEADS):
                kpart_scr[blk, h] += jnp.sum(heads[p][h].reshape(page // 8, 8, HEAD_DIM), axis=0)
        for p0 in range(0, hp, hp // 2):
            keys = jnp.concatenate([jnp.concatenate(hs, axis=1) for hs in heads[p0:p0 + hp // 2]],
                                   axis=0)
            s_t = jnp.dot(keys.astype(BF16), qbd_scr[...], preferred_element_type=F32) * scale
            for i in range(hp // 2):
                s_scr[first_page + p0 + i] = s_t[i * page:(i + 1) * page, :].T[:rows, :]

    _paged_slots(pt_ref, cache_hbm, buf, sem, hp, n_pages, pages_per_block, compute)

    @pl.when(step == n_steps - 1)
    def _():
        gates = []
        for h in range(N_HEADS):
            kmean_h = jnp.sum(kpart_scr[:, h], axis=1) * (1.0 / MOBA_BLOCK)
            gates.append(lax.dot_general(qh[h], kmean_h, (((1,), (1,)), ((), ())),
                                         precision=lax.Precision.HIGHEST,
                                         preferred_element_type=F32))
        gate = jnp.concatenate(gates, axis=0)
        sel = _topk_select(gate, jnp.ones(gate.shape, jnp.bool_), axis=1)
        bias = jnp.where(sel, 0.0, NEG_INF)
        used_ref[0] = jnp.concatenate(
            [jnp.max(sel[h * dq:(h + 1) * dq, :].astype(jnp.int32), axis=0, keepdims=True)
             for h in range(N_HEADS)], axis=0)

        s_own = []
        for h in range(N_HEADS):
            kn = knew_ref[:, h * HEAD_DIM:(h + 1) * HEAD_DIM].astype(BF16)
            s_own.append(lax.dot_general(qh_bf[h], kn, (((1,), (1,)), ((), ())),
                                         preferred_element_type=F32) * scale)
        s_own = jnp.concatenate(s_own, axis=0)
        row = lax.broadcasted_iota(jnp.int32, s_own.shape, 0) % dq
        col = lax.broadcasted_iota(jnp.int32, s_own.shape, 1)
        s_own = jnp.where(col <= row, s_own, NEG_INF)

        m_vec = jnp.full((rows, page), NEG_INF, F32)
        for g in range(n_pages):
            n = g // pages_per_block
            m_vec = jnp.maximum(m_vec, s_scr[g] + bias[:, n:n + 1])
        m = jnp.maximum(jnp.max(m_vec, axis=1, keepdims=True), jnp.max(s_own, axis=1, keepdims=True))
        p_own = jnp.exp(s_own - m)
        l_vec = jnp.zeros((rows, page), F32)
        for g in range(n_pages):
            n = g // pages_per_block
            pb = jnp.exp(s_scr[g] + bias[:, n:n + 1] - m)
            l_vec = l_vec + pb
            p_ref[0, :, g * page:(g + 1) * page] = pb
        l = jnp.sum(l_vec, axis=1, keepdims=True) + jnp.sum(p_own, axis=1, keepdims=True)
        linv_ref[0] = jnp.broadcast_to(1.0 / l, (rows, HEAD_DIM))
        for h in range(N_HEADS):
            vn = vnew_ref[:, h * HEAD_DIM:(h + 1) * HEAD_DIM].astype(BF16)
            oown_ref[0, h * dq:(h + 1) * dq, :] = jnp.dot(
                p_own[h * dq:(h + 1) * dq, :].astype(BF16), vn, preferred_element_type=F32)


def _moba_sample_v_kernel(pt_ref, used_ref, p_ref, oown_ref, linv_ref, cache_hbm, o_ref, buf, sem,
                          acc_scr, *, hp, page, n_pages):
    step = pl.program_id(1)
    n_steps = pl.num_programs(1)
    dq = o_ref.shape[0]
    ppb = MOBA_BLOCK // page
    n_blocks = n_pages // ppb

    @pl.when(jnp.logical_and(pl.program_id(0) == 0, step == 0))
    def _():
        buf[...] = jnp.zeros_like(buf)

    def wanted(b, blk, h):
        return used_ref[(b * N_HEADS + h) * n_blocks + blk] != 0

    @pl.when(step == 0)
    def _():
        acc_scr[...] = oown_ref[0]

    def compute(slot, first_page):
        for h in range(N_HEADS):
            tot = jnp.zeros((dq, HEAD_DIM), F32)
            for p in range(hp):
                col0 = (slot * hp + p) * page
                ph = p_ref[0, h * dq:(h + 1) * dq, col0:col0 + page].astype(BF16)
                tot = tot + jnp.dot(ph, buf[slot, p, h].astype(BF16), preferred_element_type=F32)
            acc_scr[h * dq:(h + 1) * dq, :] += tot

    _paged_slots(pt_ref, cache_hbm, buf, sem, hp, n_pages, ppb, compute, wanted)

    @pl.when(step == n_steps - 1)
    def _():
        for h in range(N_HEADS):
            rows_h = slice(h * dq, (h + 1) * dq)
            o_ref[:, h * HEAD_DIM:(h + 1) * HEAD_DIM] = acc_scr[rows_h, :] * linv_ref[0, rows_h, :]


def _moba_sample(q, k_new, v_new, cache_k, cache_v, page_table, hp=4, n_slots=4):
    dec_b, n_pages = page_table.shape
    page = cache_k.shape[1]
    dq = q.shape[0] // dec_b
    past = n_pages * page
    n_blocks = past // MOBA_BLOCK
    while n_pages % (n_slots * hp):
        n_slots //= 2
    n_steps = n_pages // (n_slots * hp)
    rows = N_HEADS * dq
    pt = page_table.reshape(-1)

    new_spec = pl.BlockSpec((dq, ATTN_WIDTH), lambda b, s, pt_ref: (b, 0))
    any_spec = pl.BlockSpec(memory_space=pl.ANY)
    stat_spec = pl.BlockSpec((1, rows, HEAD_DIM), lambda b, s, pt_ref: (b, 0, 0))
    stat_sds = jax.ShapeDtypeStruct((dec_b, rows, HEAD_DIM), F32)
    assert hp % (MOBA_BLOCK // page) == 0, "a page half must hold whole key blocks"
    page_buf = pltpu.VMEM((n_slots, hp, N_HEADS, page, HEAD_DIM), F32)
    probs, o_own, l_inv, used = pl.pallas_call(
        functools.partial(_moba_sample_k_kernel, hp=hp, page=page, n_blocks=n_blocks),
        grid_spec=pltpu.PrefetchScalarGridSpec(
            num_scalar_prefetch=1,
            grid=(dec_b, n_steps),
            in_specs=[new_spec, new_spec, new_spec, any_spec],
            out_specs=[pl.BlockSpec((1, rows, past), lambda b, s, pt_ref: (b, 0, 0)),
                       stat_spec, stat_spec,
                       pl.BlockSpec((1, N_HEADS, n_blocks), lambda b, s, pt_ref: (b, 0, 0))],
            scratch_shapes=[page_buf, pltpu.SemaphoreType.DMA((n_slots,)),
                            pltpu.VMEM((n_pages, rows, page), F32),
                            pltpu.VMEM((n_blocks, N_HEADS, 8, HEAD_DIM), F32),
                            pltpu.VMEM((ATTN_WIDTH, max(rows, HEAD_DIM)), BF16)]),
        out_shape=[jax.ShapeDtypeStruct((dec_b, rows, past), F32), stat_sds, stat_sds,
                   jax.ShapeDtypeStruct((dec_b, N_HEADS, n_blocks), jnp.int32)],
        compiler_params=_params("arbitrary", "arbitrary"),
        name="moba_sample_k",
    )(pt, q, k_new, v_new, cache_k)

    return pl.pallas_call(
        functools.partial(_moba_sample_v_kernel, hp=hp, page=page, n_pages=n_pages),
        grid_spec=pltpu.PrefetchScalarGridSpec(
            num_scalar_prefetch=2,
            grid=(dec_b, n_steps),
            in_specs=[pl.BlockSpec((1, rows, n_slots * hp * page), lambda b, s, *_: (b, 0, s)),
                      pl.BlockSpec((1, rows, HEAD_DIM), lambda b, s, *_: (b, 0, 0)),
                      pl.BlockSpec((1, rows, HEAD_DIM), lambda b, s, *_: (b, 0, 0)),
                      any_spec],
            out_specs=pl.BlockSpec((dq, ATTN_WIDTH), lambda b, s, *_: (b, 0)),
            scratch_shapes=[page_buf, pltpu.SemaphoreType.DMA((n_slots,)),
                            pltpu.VMEM((rows, HEAD_DIM), F32)]),
        out_shape=jax.ShapeDtypeStruct(q.shape, F32),
        compiler_params=_params("arbitrary", "arbitrary"),
        name="moba_sample_v",
    )(pt, used.reshape(-1), probs, o_own, l_inv, cache_v)


def _pool_kernel(u_ref, halo_ref, w_ref, ls_ref, o_ref, ext_scr, *, pos0):
    t = pl.program_id(1)
    gb, r, width = u_ref.shape
    grp = width // len(POOL_WINDOWS)

    @pl.when(t == 0)
    def _():
        ext_scr[:, 0:POOL_HALO, :] = halo_ref[...]

    @pl.when(t > 0)
    def _():
        ext_scr[:, 0:POOL_HALO, :] = ext_scr[:, r:r + POOL_HALO, :]

    ext_scr[:, POOL_HALO:POOL_HALO + r, :] = u_ref[...]

    pos = pos0 + t * r + lax.broadcasted_iota(jnp.int32, (1, r, grp), 1)
    for g, w in enumerate(POOL_WINDOWS):
        cols = slice(g * grp, (g + 1) * grp)
        tot = ext_scr[:, POOL_HALO:POOL_HALO + r, cols]
        for d in range(1, w):
            tot = tot + ext_scr[:, POOL_HALO - d:POOL_HALO - d + r, cols]
        cnt = jnp.minimum(w, pos + 1).astype(F32)
        y = tot / cnt - u_ref[:, :, cols]
        y = jnp.dot(y.reshape(gb * r, grp).astype(BF16), w_ref[g], preferred_element_type=F32)
        o_ref[:, cols] = (y * ls_ref[:, cols]).astype(o_ref.dtype)


def _pool(u, halo, w_pool_bf, ls_pool, n_seq, seq, gb, r, pos0):
    width = u.shape[1]
    n_g, n_t = n_seq // gb, seq // r
    u3 = u.reshape(n_seq, seq, width)
    return pl.pallas_call(
        functools.partial(_pool_kernel, pos0=pos0),
        grid=(n_g, n_t),
        in_specs=[pl.BlockSpec((gb, r, width), lambda g, t: (g, t, 0)),
                  pl.BlockSpec((gb, POOL_HALO, width), lambda g, t: (g, 0, 0)),
                  pl.BlockSpec(w_pool_bf.shape, lambda g, t: (0, 0, 0)),
                  pl.BlockSpec((1, width), lambda g, t: (0, 0))],
        out_specs=pl.BlockSpec((gb * r, width), lambda g, t: (g * n_t + t, 0)),
        out_shape=jax.ShapeDtypeStruct(u.shape, BF16),
        scratch_shapes=[pltpu.VMEM((gb, r + POOL_HALO, width), F32)],
        compiler_params=_params("arbitrary", "arbitrary"),
        name="pool_mixer",
    )(u3, halo, w_pool_bf, ls_pool.reshape(1, width))


def _merge_kernel(x_ref, sh_ref, sc_ref, gt_ref, g1_ref, attn_ref, pool_ref,
                  wga_ref, wgb_ref, woa_ref, wob_ref, wout_ref, o_ref, h_scr, acc_scr):
    c = pl.program_id(2)

    @pl.when(c == 0)
    def _():
        h = _mod_norm(x_ref[...], g1_ref[...], sc_ref[...], sh_ref[...])
        h_scr[...] = h.reshape(h_scr.shape).astype(BF16)
        acc_scr[...] = jnp.zeros_like(acc_scr)

    h = h_scr[...]
    g_a = jax.nn.sigmoid(jnp.dot(h, wga_ref[...], preferred_element_type=F32))
    g_b = jax.nn.sigmoid(jnp.dot(h, wgb_ref[...], preferred_element_type=F32))
    y_a = jnp.dot(attn_ref[...].astype(BF16), woa_ref[...], preferred_element_type=F32)
    y_b = jnp.dot(pool_ref[...].astype(BF16), wob_ref[...], preferred_element_type=F32)
    mix_in = (g_a * y_a + g_b * y_b).astype(BF16)
    acc_scr[...] += jnp.dot(mix_in, wout_ref[...], preferred_element_type=F32)

    @pl.when(c == pl.num_programs(2) - 1)
    def _():
        o_ref[...] = x_ref[...] + gt_ref[...] * acc_scr[...].reshape(o_ref.shape)


def _merge(x3, mod, g_norm1, attn, pooled, w_in_bf, w_o_attn_bf, w_o_pool_bf, w_out_bf, gb, r, tn=512):
    n_g, n_t, d = _group_specs(x3, gb, r)
    tm = gb * r
    aw = attn.shape[1]
    pw = pooled.shape[1]
    gate_col0 = (w_in_bf.shape[1] - 2 * d) // tn
    n_c = d // tn
    x_spec = pl.BlockSpec((gb, r, d), lambda g, t, c: (g, t, 0))
    mod_spec = lambda k: pl.BlockSpec((gb, 1, d), lambda g, t, c: (g, 0, k))
    return pl.pallas_call(
        _merge_kernel,
        grid=(n_g, n_t, n_c),
        in_specs=[x_spec, mod_spec(0), mod_spec(1), mod_spec(2),
                  pl.BlockSpec((1, d), lambda g, t, c: (0, 0)),
                  pl.BlockSpec((tm, aw), lambda g, t, c: (g * n_t + t, 0)),
                  pl.BlockSpec((tm, pw), lambda g, t, c: (g * n_t + t, 0)),
                  pl.BlockSpec((d, tn), lambda g, t, c: (0, gate_col0 + c)),
                  pl.BlockSpec((d, tn), lambda g, t, c: (0, gate_col0 + n_c + c)),
                  pl.BlockSpec((aw, tn), lambda g, t, c: (0, c)),
                  pl.BlockSpec((pw, tn), lambda g, t, c: (0, c)),
                  pl.BlockSpec((tn, d), lambda g, t, c: (c, 0))],
        out_specs=x_spec,
        out_shape=jax.ShapeDtypeStruct(x3.shape, F32),
        scratch_shapes=[pltpu.VMEM((tm, d), BF16), pltpu.VMEM((tm, d), F32)],
        compiler_params=_params("arbitrary", "arbitrary", "arbitrary"),
        name="merge_out",
    )(x3, mod, mod, mod, g_norm1.reshape(1, d), attn, pooled,
      w_in_bf, w_in_bf, w_o_attn_bf, w_o_pool_bf, w_out_bf)


def _ffn_kernel(x_ref, sh_ref, sc_ref, gt_ref, g2_ref, wa_ref, wb_ref, wo_ref, o_ref, h_scr, acc_scr):
    c = pl.program_id(2)

    @pl.when(c == 0)
    def _():
        h = _mod_norm(x_ref[...], g2_ref[...], sc_ref[...], sh_ref[...])
        h_scr[...] = h.reshape(h_scr.shape).astype(BF16)
        acc_scr[...] = jnp.zeros_like(acc_scr)

    h = h_scr[...]
    a = jnp.dot(h, wa_ref[...], preferred_element_type=F32)
    b = jnp.dot(h, wb_ref[...], preferred_element_type=F32)
    hid = (jax.nn.silu(a) * b).astype(BF16)
    acc_scr[...] += jnp.dot(hid, wo_ref[...], preferred_element_type=F32)

    @pl.when(c == pl.num_programs(2) - 1)
    def _():
        o_ref[...] = x_ref[...] + gt_ref[...] * acc_scr[...].reshape(o_ref.shape)


def _ffn(x3, mod, g_norm2, w_ffn_in_bf, w_ffn_out_bf, gb, r, tn=512):
    n_g, n_t, d = _group_specs(x3, gb, r)
    tm = gb * r
    hidden = w_ffn_out_bf.shape[0]
    n_c = hidden // tn
    x_spec = pl.BlockSpec((gb, r, d), lambda g, t, c: (g, t, 0))
    mod_spec = lambda k: pl.BlockSpec((gb, 1, d), lambda g, t, c: (g, 0, k))
    return pl.pallas_call(
        _ffn_kernel,
        grid=(n_g, n_t, n_c),
        in_specs=[x_spec, mod_spec(3), mod_spec(4), mod_spec(5),
                  pl.BlockSpec((1, d), lambda g, t, c: (0, 0)),
                  pl.BlockSpec((d, tn), lambda g, t, c: (0, c)),
                  pl.BlockSpec((d, tn), lambda g, t, c: (0, n_c + c)),
                  pl.BlockSpec((tn, d), lambda g, t, c: (c, 0))],
        out_specs=x_spec,
        out_shape=jax.ShapeDtypeStruct(x3.shape, F32),
        scratch_shapes=[pltpu.VMEM((tm, d), BF16), pltpu.VMEM((tm, d), F32)],
        compiler_params=_params("arbitrary", "arbitrary", "arbitrary"),
        name="ffn_swiglu",
    )(x3, mod, mod, mod, g_norm2.reshape(1, d), w_ffn_in_bf, w_ffn_in_bf, w_ffn_out_bf)


def _rope_tables(pos):
    half = HEAD_DIM // 2
    inv = ROPE_THETA ** (-jnp.arange(half, dtype=F32) / half)
    ang = pos.astype(F32)[:, None] * inv[None, :]
    cos, sin = jnp.cos(ang), jnp.sin(ang)
    return jnp.concatenate([cos, cos], axis=-1), jnp.concatenate([-sin, sin], axis=-1)


def _pick_rows(seq, target=512):
    r = min(seq, target)
    while seq % r:
        r //= 2
    return r


def kernel(x_prompt, x_sample, c_prompt, c_sample, cache_k, cache_v, state_pool, page_table, w_ada, b_ada, g_norm1, g_norm2, w_in, g_qnorm, g_knorm, w_pool, ls_pool, w_o_attn, w_o_pool, w_out, w_ffn_in, w_ffn_out):
    n_p, seq, d = x_prompt.shape
    n_s, dq, _ = x_sample.shape
    past = page_table.shape[1] * cache_k.shape[1]
    pool_w = state_pool.shape[2]
    n_state = state_pool.shape[1]

    w_in_bf = w_in.astype(BF16)
    w_pool_bf = w_pool.astype(BF16)
    w_o_attn_bf = w_o_attn.astype(BF16)
    w_o_pool_bf = w_o_pool.astype(BF16)
    w_out_bf = w_out.astype(BF16)
    w_ffn_in_bf = w_ffn_in.astype(BF16)
    w_ffn_out_bf = w_ffn_out.astype(BF16)

    n_c = n_p + n_s
    c_all = jnp.concatenate([c_prompt, c_sample], axis=0)
    c_all = jnp.pad(c_all, ((0, (-n_c) % 8), (0, 0)))
    mod = _ada(c_all, w_ada, b_ada)
    mod_p = mod[:n_p].reshape(n_p, 1, 6 * d)
    mod_s = mod[n_p:n_c].reshape(n_s, 1, 6 * d)

    r_p = _pick_rows(seq)
    cos_p, sin_p = _rope_tables(jnp.arange(seq))
    q_p, k_p, v_p, u_p = _qkvu(x_prompt, mod_p, g_norm1, w_in_bf, g_qnorm, g_knorm, cos_p, sin_p, 1, r_p)
    attn_p = _moba_prompt(q_p, k_p, v_p, n_p, seq)
    pooled_p = _pool(u_p, jnp.zeros((n_p, POOL_HALO, pool_w), F32), w_pool_bf, ls_pool,
                     n_p, seq, 1, r_p, 0)
    x1_p = _merge(x_prompt, mod_p, g_norm1, attn_p, pooled_p, w_in_bf, w_o_attn_bf, w_o_pool_bf,
                  w_out_bf, 1, r_p)
    y_p = _ffn(x1_p, mod_p, g_norm2, w_ffn_in_bf, w_ffn_out_bf, 1, r_p)

    cos_s, sin_s = _rope_tables(past + jnp.arange(dq))
    cos_s, sin_s = jnp.tile(cos_s, (n_s, 1)), jnp.tile(sin_s, (n_s, 1))
    q_s, k_s, v_s, u_s = _qkvu(x_sample, mod_s, g_norm1, w_in_bf, g_qnorm, g_knorm, cos_s, sin_s, n_s, dq)
    attn_s = _moba_sample(q_s, k_s, v_s, cache_k, cache_v, page_table)
    halo_s = jnp.pad(state_pool, ((0, 0), (POOL_HALO - n_state, 0), (0, 0)))
    pooled_s = _pool(u_s, halo_s, w_pool_bf, ls_pool, n_s, dq, n_s, dq, past)
    x1_s = _merge(x_sample, mod_s, g_norm1, attn_s, pooled_s, w_in_bf, w_o_attn_bf, w_o_pool_bf,
                  w_out_bf, n_s, dq)
    y_s = _ffn(x1_s, mod_s, g_norm2, w_ffn_in_bf, w_ffn_out_bf, n_s, dq)

    k_prompt = k_p.reshape(n_p, seq, N_HEADS, HEAD_DIM)
    v_prompt = v_p.reshape(n_p, seq, N_HEADS, HEAD_DIM)
    pool_prompt = u_p.reshape(n_p, seq, pool_w)[:, seq - n_state:]
    k_sample = k_s.reshape(n_s, dq, N_HEADS, HEAD_DIM)
    v_sample = v_s.reshape(n_s, dq, N_HEADS, HEAD_DIM)
    pool_sample = jnp.concatenate([state_pool, u_s.reshape(n_s, dq, pool_w)], axis=1)[:, -n_state:]
    return (y_p, y_s, k_prompt, v_prompt, pool_prompt, k_sample, v_sample, pool_sample)
```

```python
import functools

import jax
import jax.numpy as jnp
from jax import lax
from jax.experimental import pallas as pl
from jax.experimental.pallas import tpu as pltpu

N_HEADS = 8
HEAD_DIM = 128
ATTN_WIDTH = N_HEADS * HEAD_DIM
MOBA_BLOCK = 256
MOBA_TOPK = 3
ROPE_THETA = 10000.0
POOL_WINDOWS = (2, 4, 8, 16)
POOL_HALO = 16
RMS_EPS = 1e-6
_TILE_GROUPS = (8, 4, 2)
_TN_PROMPT = 512
_TN_SAMPLE = 256

F32 = jnp.float32
BF16 = jnp.bfloat16
NEG_INF = float("-inf")
_LOG2_E = 1.4426950408889634

_VMEM_LIMIT = 52 * 1024 * 1024


def _params(*sem):
    return pltpu.CompilerParams(dimension_semantics=sem, vmem_limit_bytes=_VMEM_LIMIT)


_NORM_ROWS = 16


def _mod_norm_store(x_ref, g_ref, sc_ref, sh_ref, h_scr):
    gb, r, d = x_ref.shape
    g = g_ref[...]
    if r >= _NORM_ROWS:
        per_group = r // _NORM_ROWS
        chunks = [(gi, 1, c * _NORM_ROWS, _NORM_ROWS) for gi in range(gb) for c in range(per_group)]
    else:
        n_g = _NORM_ROWS // r
        chunks = [(gi, n_g, 0, r) for gi in range(0, gb, n_g)]
    gains = {}
    for gi, n_g, r0, nr in chunks:
        if gi not in gains:
            gains[gi] = g * (1.0 + sc_ref[gi:gi + n_g])
        x = x_ref[gi:gi + n_g, r0:r0 + nr, :]
        ms = jnp.mean(x * x, axis=-1, keepdims=True)
        h = x * lax.rsqrt(ms + RMS_EPS) * gains[gi] + sh_ref[gi:gi + n_g]
        row0 = gi * r + r0
        h_scr[row0:row0 + n_g * nr, :] = h.reshape(n_g * nr, d).astype(BF16)


def _ada_kernel(c_ref, w_ref, b_ref, o_ref):
    o_ref[...] = jnp.dot(c_ref[...].astype(BF16), w_ref[...].astype(BF16),
                         preferred_element_type=F32) + b_ref[...]


def _ada(c_all, w_ada, b_ada, tn=1024):
    m, d = c_all.shape
    n = w_ada.shape[1]
    return pl.pallas_call(
        _ada_kernel,
        grid=(n // tn,),
        in_specs=[pl.BlockSpec((m, d), lambda j: (0, 0)),
                  pl.BlockSpec((d, tn), lambda j: (0, j)),
                  pl.BlockSpec((1, tn), lambda j: (0, j))],
        out_specs=pl.BlockSpec((m, tn), lambda j: (0, j)),
        out_shape=jax.ShapeDtypeStruct((m, n), F32),
        compiler_params=_params("arbitrary"),
        name="ada_mod",
    )(c_all, w_ada, b_ada.reshape(1, n))


def _head_norm_rope(th, g, cos, sin_signed):
    ms = jnp.mean(th * th, axis=-1, keepdims=True)
    y = th * lax.rsqrt(ms + RMS_EPS) * g
    return y * cos + pltpu.roll(y, HEAD_DIM // 2, 1) * sin_signed


_PROJ_CHUNK = 2 * HEAD_DIM


def _qkvu_kernel(x_ref, sh_ref, sc_ref, g1_ref, w_ref, gq_ref, gk_ref, cos_ref, sin_ref,
                 q_ref, k_ref, v_ref, u_ref, h_scr):
    _mod_norm_store(x_ref, g1_ref, sc_ref, sh_ref, h_scr)

    width = q_ref.shape[1]
    outs = ((q_ref, gq_ref), (k_ref, gk_ref), (v_ref, None), (u_ref, None))
    for i, (o_ref, g_ref) in enumerate(outs):
        for c in range(0, width, _PROJ_CHUNK):
            col = i * width + c
            acc = jnp.dot(h_scr[...], w_ref[:, col:col + _PROJ_CHUNK], preferred_element_type=F32)
            if g_ref is None:
                o_ref[:, c:c + _PROJ_CHUNK] = acc
            else:
                for hh in range(0, _PROJ_CHUNK, HEAD_DIM):
                    o_ref[:, c + hh:c + hh + HEAD_DIM] = _head_norm_rope(
                        acc[:, hh:hh + HEAD_DIM], g_ref[...], cos_ref[...], sin_ref[...])


def _group_specs(x3, gb, r):
    g_total, s, d = x3.shape
    n_g, n_t = g_total // gb, s // r
    return n_g, n_t, d


def _qkvu(x3, mod, g_norm1, w_in_bf, g_q, g_k, cos, sin_signed, gb, r):
    n_g, n_t, d = _group_specs(x3, gb, r)
    tm = gb * r
    tokens = x3.shape[0] * x3.shape[1]
    wq = ATTN_WIDTH
    x_spec = pl.BlockSpec((gb, r, d), lambda g, t: (g, t, 0))
    mod_spec = lambda k: pl.BlockSpec((gb, 1, d), lambda g, t: (g, 0, k))
    row_spec = pl.BlockSpec((1, d), lambda g, t: (0, 0))
    head_spec = pl.BlockSpec((1, HEAD_DIM), lambda g, t: (0, 0))
    tab_spec = pl.BlockSpec((tm, HEAD_DIM), lambda g, t: (t, 0))
    out_spec = pl.BlockSpec((tm, wq), lambda g, t: (g * n_t + t, 0))
    out_sds = jax.ShapeDtypeStruct((tokens, wq), F32)
    w_spec = pl.BlockSpec((d, 4 * wq), lambda g, t: (0, 0), pipeline_mode=pl.Buffered(1))
    return pl.pallas_call(
        _qkvu_kernel,
        grid=(n_g, n_t),
        in_specs=[x_spec, mod_spec(0), mod_spec(1), row_spec, w_spec,
                  head_spec, head_spec, tab_spec, tab_spec],
        out_specs=[out_spec] * 4,
        out_shape=[out_sds] * 4,
        scratch_shapes=[pltpu.VMEM((tm, d), BF16)],
        compiler_params=_params("arbitrary", "arbitrary"),
        name="qkvu_proj",
    )(x3, mod, mod, g_norm1.reshape(1, d), w_in_bf, g_q.reshape(1, HEAD_DIM),
      g_k.reshape(1, HEAD_DIM), cos, sin_signed)


def _topk_select(gate, valid, axis):
    n = gate.shape[axis]
    g = jnp.where(valid, gate, NEG_INF)
    idx = lax.broadcasted_iota(jnp.int32, gate.shape, axis)
    rank = jnp.zeros(gate.shape, jnp.int32)
    for m in range(n):
        gm = g[m:m + 1, :] if axis == 0 else g[:, m:m + 1]
        beats = (gm > g) | ((gm == g) & (m < idx))
        rank = rank + beats.astype(jnp.int32)
    return valid & (rank < MOBA_TOPK) & (jnp.abs(g) < float("inf"))


def _moba_prompt_kernel(q_ref, k_ref, v_ref, o_ref, kb_scr, vt_scr, kmean_scr, bias_scr, tri_scr, s_scr,
                        *, nb):
    qi = pl.program_id(2)
    blk = MOBA_BLOCK

    @pl.when(qi == 0)
    def _():
        for n in range(nb):
            kn = k_ref[n * blk:(n + 1) * blk, :]
            kb_scr[n * blk:(n + 1) * blk, :] = kn.astype(BF16)
            kmean_scr[n:n + 1, :] = jnp.mean(kn, axis=0, keepdims=True)
            vt_scr[n] = v_ref[n * blk:(n + 1) * blk, :].T.astype(BF16)
        kb_scr[nb * blk:, :] = jnp.zeros((blk, HEAD_DIM), BF16)
        kmean = kmean_scr[...]
        for qb_i in range(nb):
            gate = lax.dot_general(kmean, q_ref[qb_i * blk:(qb_i + 1) * blk, :],
                                   (((1,), (1,)), ((), ())),
                                   precision=lax.Precision.HIGHEST, preferred_element_type=F32)
            n_idx = lax.broadcasted_iota(jnp.int32, gate.shape, 0)
            sel = _topk_select(gate, n_idx < qb_i, axis=0)
            bias_scr[qb_i, 0:nb, :] = jnp.where(sel | (n_idx == qb_i), 0.0, NEG_INF)
            bias_scr[qb_i, nb:nb + 1, :] = jnp.full((1, blk), NEG_INF, F32)
        kpos = lax.broadcasted_iota(jnp.int32, (blk, blk), 0)
        qpos = lax.broadcasted_iota(jnp.int32, (blk, blk), 1)
        tri_scr[0] = jnp.zeros((blk, blk), F32)
        tri_scr[1] = jnp.where(kpos <= qpos, 0.0, NEG_INF)

    qb = q_ref[pl.ds(pl.multiple_of(qi * blk, blk), blk), :].astype(BF16)
    exp2_scale = HEAD_DIM ** -0.5 * _LOG2_E

    def fold(x, op):
        return op(x.reshape(blk // 8, 8, blk), axis=0)

    n_tiles = qi + 1
    spans = []
    base = 0
    for i, width in enumerate(_TILE_GROUPS):
        trips = (n_tiles - base + (1 if i else 0)) // width
        spans.append((width, base, trips))
        base = base + trips * width

    def pass1(width, base):
        def body(jj, m8):
            j0 = base + width * jj
            keys = kb_scr[pl.ds(pl.multiple_of(j0 * blk, blk), width * blk), :]
            s_all = lax.dot_general(keys, qb, (((1,), (1,)), ((), ())), preferred_element_type=F32)
            for t in range(width):
                j = j0 + t
                bias = bias_scr[qi, pl.ds(j, 1), :]
                s = s_all[t * blk:(t + 1) * blk, :] + bias + tri_scr[(j == qi).astype(jnp.int32)]
                s_scr[j] = s
                m8 = jnp.maximum(m8, fold(s, jnp.max))
            return m8
        return body

    m8 = jnp.full((8, blk), NEG_INF, F32)
    for width, base, trips in spans:
        m8 = lax.fori_loop(0, trips, pass1(width, base), m8)
    m = jnp.max(m8, axis=0, keepdims=True)

    def pass2(width, base):
        def body(jj, carry):
            l8, acc = carry
            for t in range(width):
                j = base + width * jj + t
                p = jnp.exp2((s_scr[j] - m) * exp2_scale)
                l8 = l8 + fold(p, jnp.sum)
                acc = acc + jnp.dot(vt_scr[jnp.minimum(j, nb - 1)], p.astype(BF16),
                                    preferred_element_type=F32)
            return l8, acc
        return body

    l8 = jnp.zeros((8, blk), F32)
    acc = jnp.zeros((HEAD_DIM, blk), F32)
    for width, base, trips in spans:
        l8, acc = lax.fori_loop(0, trips, pass2(width, base), (l8, acc))
    l = jnp.sum(l8, axis=0, keepdims=True)
    o_ref[...] = (acc / l).T.astype(o_ref.dtype)


def _moba_prompt(q, k, v, n_seq, seq):
    nb = seq // MOBA_BLOCK
    blk = MOBA_BLOCK
    kv_spec = pl.BlockSpec((seq, HEAD_DIM), lambda b, h, i: (b, h))
    q_spec = pl.BlockSpec((blk, HEAD_DIM), lambda b, h, i: (b * nb + i, h))
    return pl.pallas_call(
        functools.partial(_moba_prompt_kernel, nb=nb),
        grid=(n_seq, N_HEADS, nb),
        in_specs=[kv_spec, kv_spec, kv_spec],
        out_specs=q_spec,
        out_shape=jax.ShapeDtypeStruct(q.shape, BF16),
        scratch_shapes=[pltpu.VMEM(((nb + 1) * blk, HEAD_DIM), BF16),
                        pltpu.VMEM((nb, HEAD_DIM, blk), BF16),
                        pltpu.VMEM((nb, HEAD_DIM), F32),
                        pltpu.VMEM((nb, nb + 1, blk), F32),
                        pltpu.VMEM((2, blk, blk), F32),
                        pltpu.VMEM((nb + 1, blk, blk), F32)],
        compiler_params=_params("arbitrary", "arbitrary", "arbitrary"),
        name="moba_prompt",
    )(q, k, v)


def _page_copy_groups(cache_hbm, pt_ref, buf, sem, slot, b, first_page, hp, n_pages, ppb, wanted):
    groups = []
    for p0 in range(0, hp, ppb):
        pgs = [pt_ref[b * n_pages + first_page + p0 + i] for i in range(ppb)]
        for h in range(N_HEADS):
            pred = None if wanted is None else wanted(b, (first_page + p0) // ppb, h)
            groups.append((pred, [pltpu.make_async_copy(cache_hbm.at[pgs[i], :, h, :],
                                                        buf.at[slot, p0 + i, h], sem.at[slot])
                                  for i in range(ppb)]))
    return groups


def _for_each_copy(groups, method):
    for pred, copies in groups:
        def run(copies=copies):
            for c in copies:
                getattr(c, method)()
        if pred is None:
            run()
        else:
            pl.when(pred)(run)


def _paged_slots(pt_ref, cache_hbm, buf, sem, hp, n_pages, ppb, compute, wanted=None):
    n_slots = buf.shape[0]
    b, step = pl.program_id(0), pl.program_id(1)
    n_b, n_steps = pl.num_programs(0), pl.num_programs(1)
    is_first = jnp.logical_and(b == 0, step == 0)
    is_last = jnp.logical_and(b == n_b - 1, step == n_steps - 1)
    wrap = step == n_steps - 1
    nxt_b = jnp.where(wrap, b + 1, b)
    nxt_step = jnp.where(wrap, 0, step + 1)

    def groups(bb, ss, slot):
        return _page_copy_groups(cache_hbm, pt_ref, buf, sem, slot, bb, (n_slots * ss + slot) * hp,
                                 hp, n_pages, ppb, wanted)

    @pl.when(is_first)
    def _():
        for slot in range(n_slots):
            _for_each_copy(groups(b, step, slot), "start")

    for slot in range(n_slots):
        _for_each_copy(groups(b, step, slot), "wait")
        compute(slot, (n_slots * step + slot) * hp)

        @pl.when(jnp.logical_not(is_last))
        def _():
            _for_each_copy(groups(nxt_b, nxt_step, slot), "start")


def _moba_sample_k_kernel(pt_ref, q_ref, knew_ref, vnew_ref, cache_hbm, p_ref, oown_ref, linv_ref,
                          used_ref, buf, sem, s_scr, kpart_scr, qbd_scr, *, hp, page, n_blocks):
    step = pl.program_id(1)
    n_steps = pl.num_programs(1)
    dq = q_ref.shape[0]
    rows = N_HEADS * dq
    scale = HEAD_DIM ** -0.5
    pages_per_block = MOBA_BLOCK // page
    n_pages = n_blocks * pages_per_block

    qh = [q_ref[:, h * HEAD_DIM:(h + 1) * HEAD_DIM] for h in range(N_HEADS)]
    qh_bf = [x.astype(BF16) for x in qh]

    @pl.when(step == 0)
    def _():
        kpart_scr[...] = jnp.zeros_like(kpart_scr)
        q_rep = jnp.concatenate([q_ref[...]] * N_HEADS
                                + [jnp.zeros((qbd_scr.shape[1] - rows, ATTN_WIDTH), F32)], axis=0)
        row_head = lax.broadcasted_iota(jnp.int32, q_rep.shape, 0) // dq
        lane_head = lax.broadcasted_iota(jnp.int32, q_rep.shape, 1) // HEAD_DIM
        qbd_scr[...] = jnp.where(row_head == lane_head, q_rep, 0.0).T.astype(BF16)

    def compute(slot, first_page):
        heads = [[buf[slot, p, h] for h in range(N_HEADS)] for p in range(hp)]
        for p in range(hp):
            blk = (first_page + p) // pages_per_block
            for h in range(N_HEADS):
                kpart_scr[blk, h] += jnp.sum(heads[p][h].reshape(page // 8, 8, HEAD_DIM), axis=0)
        for p0 in range(0, hp, hp // 2):
            keys = jnp.concatenate([jnp.concatenate(hs, axis=1) for hs in heads[p0:p0 + hp // 2]],
                                   axis=0)
            s_t = jnp.dot(keys.astype(BF16), qbd_scr[...], preferred_element_type=F32) * scale
            for i in range(hp // 2):
                s_scr[first_page + p0 + i] = s_t[i * page:(i + 1) * page, :].T[:rows, :]

    _paged_slots(pt_ref, cache_hbm, buf, sem, hp, n_pages, pages_per_block, compute)

    @pl.when(step == n_steps - 1)
    def _():
        gates = []
        for h in range(N_HEADS):
            kmean_h = jnp.sum(kpart_scr[:, h], axis=1) * (1.0 / MOBA_BLOCK)
            gates.append(lax.dot_general(qh[h], kmean_h, (((1,), (1,)), ((), ())),
                                         precision=lax.Precision.HIGHEST,
                                         preferred_element_type=F32))
        gate = jnp.concatenate(gates, axis=0)
        sel = _topk_select(gate, jnp.ones(gate.shape, jnp.bool_), axis=1)
        bias = jnp.where(sel, 0.0, NEG_INF)
        used_ref[0] = jnp.concatenate(
            [jnp.max(sel[h * dq:(h + 1) * dq, :].astype(jnp.int32), axis=0, keepdims=True)
             for h in range(N_HEADS)], axis=0)

        s_own = []
        for h in range(N_HEADS):
            kn = knew_ref[:, h * HEAD_DIM:(h + 1) * HEAD_DIM].astype(BF16)
            s_own.append(lax.dot_general(qh_bf[h], kn, (((1,), (1,)), ((), ())),
                                         preferred_element_type=F32) * scale)
        s_own = jnp.concatenate(s_own, axis=0)
        row = lax.broadcasted_iota(jnp.int32, s_own.shape, 0) % dq
        col = lax.broadcasted_iota(jnp.int32, s_own.shape, 1)
        s_own = jnp.where(col <= row, s_own, NEG_INF)

        m_vec = jnp.full((rows, page), NEG_INF, F32)
        for g in range(n_pages):
            n = g // pages_per_block
            m_vec = jnp.maximum(m_vec, s_scr[g] + bias[:, n:n + 1])
        m = jnp.maximum(jnp.max(m_vec, axis=1, keepdims=True), jnp.max(s_own, axis=1, keepdims=True))
        p_own = jnp.exp(s_own - m)
        l_vec = jnp.zeros((rows, page), F32)
        for g in range(n_pages):
            n = g // pages_per_block
            pb = jnp.exp(s_scr[g] + bias[:, n:n + 1] - m)
            l_vec = l_vec + pb
            p_ref[0, :, g * page:(g + 1) * page] = pb
        l = jnp.sum(l_vec, axis=1, keepdims=True) + jnp.sum(p_own, axis=1, keepdims=True)
        linv_ref[0] = jnp.broadcast_to(1.0 / l, (rows, HEAD_DIM))
        for h in range(N_HEADS):
            vn = vnew_ref[:, h * HEAD_DIM:(h + 1) * HEAD_DIM].astype(BF16)
            oown_ref[0, h * dq:(h + 1) * dq, :] = jnp.dot(
                p_own[h * dq:(h + 1) * dq, :].astype(BF16), vn, preferred_element_type=F32)


def _moba_sample_v_kernel(pt_ref, used_ref, p_ref, oown_ref, linv_ref, cache_hbm, o_ref, buf, sem,
                          acc_scr, *, hp, page, n_pages):
    step = pl.program_id(1)
    n_steps = pl.num_programs(1)
    dq = o_ref.shape[0]
    ppb = MOBA_BLOCK // page
    n_blocks = n_pages // ppb

    @pl.when(jnp.logical_and(pl.program_id(0) == 0, step == 0))
    def _():
        buf[...] = jnp.zeros_like(buf)

    def wanted(b, blk, h):
        return used_ref[(b * N_HEADS + h) * n_blocks + blk] != 0

    @pl.when(step == 0)
    def _():
        acc_scr[...] = oown_ref[0]

    def compute(slot, first_page):
        for h in range(N_HEADS):
            tot = jnp.zeros((dq, HEAD_DIM), F32)
            for p in range(hp):
                col0 = (slot * hp + p) * page
                ph = p_ref[0, h * dq:(h + 1) * dq, col0:col0 + page].astype(BF16)
                tot = tot + jnp.dot(ph, buf[slot, p, h].astype(BF16), preferred_element_type=F32)
            acc_scr[h * dq:(h + 1) * dq, :] += tot

    _paged_slots(pt_ref, cache_hbm, buf, sem, hp, n_pages, ppb, compute, wanted)

    @pl.when(step == n_steps - 1)
    def _():
        for h in range(N_HEADS):
            rows_h = slice(h * dq, (h + 1) * dq)
            o_ref[:, h * HEAD_DIM:(h + 1) * HEAD_DIM] = acc_scr[rows_h, :] * linv_ref[0, rows_h, :]


def _moba_sample(q, k_new, v_new, cache_k, cache_v, page_table, hp=4, n_slots=4):
    dec_b, n_pages = page_table.shape
    page = cache_k.shape[1]
    dq = q.shape[0] // dec_b
    past = n_pages * page
    n_blocks = past // MOBA_BLOCK
    while n_pages % (n_slots * hp):
        n_slots //= 2
    n_steps = n_pages // (n_slots * hp)
    rows = N_HEADS * dq
    pt = page_table.reshape(-1)

    new_spec = pl.BlockSpec((dq, ATTN_WIDTH), lambda b, s, pt_ref: (b, 0))
    any_spec = pl.BlockSpec(memory_space=pl.ANY)
    stat_spec = pl.BlockSpec((1, rows, HEAD_DIM), lambda b, s, pt_ref: (b, 0, 0))
    stat_sds = jax.ShapeDtypeStruct((dec_b, rows, HEAD_DIM), F32)
    assert hp % (MOBA_BLOCK // page) == 0, "a page half must hold whole key blocks"
    page_buf = pltpu.VMEM((n_slots, hp, N_HEADS, page, HEAD_DIM), F32)
    probs, o_own, l_inv, used = pl.pallas_call(
        functools.partial(_moba_sample_k_kernel, hp=hp, page=page, n_blocks=n_blocks),
        grid_spec=pltpu.PrefetchScalarGridSpec(
            num_scalar_prefetch=1,
            grid=(dec_b, n_steps),
            in_specs=[new_spec, new_spec, new_spec, any_spec],
            out_specs=[pl.BlockSpec((1, rows, past), lambda b, s, pt_ref: (b, 0, 0)),
                       stat_spec, stat_spec,
                       pl.BlockSpec((1, N_HEADS, n_blocks), lambda b, s, pt_ref: (b, 0, 0))],
            scratch_shapes=[page_buf, pltpu.SemaphoreType.DMA((n_slots,)),
                            pltpu.VMEM((n_pages, rows, page), F32),
                            pltpu.VMEM((n_blocks, N_HEADS, 8, HEAD_DIM), F32),
                            pltpu.VMEM((ATTN_WIDTH, max(rows, HEAD_DIM)), BF16)]),
        out_shape=[jax.ShapeDtypeStruct((dec_b, rows, past), F32), stat_sds, stat_sds,
                   jax.ShapeDtypeStruct((dec_b, N_HEADS, n_blocks), jnp.int32)],
        compiler_params=_params("arbitrary", "arbitrary"),
        name="moba_sample_k",
    )(pt, q, k_new, v_new, cache_k)

    return pl.pallas_call(
        functools.partial(_moba_sample_v_kernel, hp=hp, page=page, n_pages=n_pages),
        grid_spec=pltpu.PrefetchScalarGridSpec(
            num_scalar_prefetch=2,
            grid=(dec_b, n_steps),
            in_specs=[pl.BlockSpec((1, rows, n_slots * hp * page), lambda b, s, *_: (b, 0, s)),
                      pl.BlockSpec((1, rows, HEAD_DIM), lambda b, s, *_: (b, 0, 0)),
                      pl.BlockSpec((1, rows, HEAD_DIM), lambda b, s, *_: (b, 0, 0)),
                      any_spec],
            out_specs=pl.BlockSpec((dq, ATTN_WIDTH), lambda b, s, *_: (b, 0)),
            scratch_shapes=[page_buf, pltpu.SemaphoreType.DMA((n_slots,)),
                            pltpu.VMEM((rows, HEAD_DIM), F32)]),
        out_shape=jax.ShapeDtypeStruct(q.shape, F32),
        compiler_params=_params("arbitrary", "arbitrary"),
        name="moba_sample_v",
    )(pt, used.reshape(-1), probs, o_own, l_inv, cache_v)


def _pool_kernel(u_ref, halo_ref, w_ref, ls_ref, o_ref, ext_scr, *, pos0):
    t = pl.program_id(1)
    gb, r, width = u_ref.shape
    grp = width // len(POOL_WINDOWS)

    @pl.when(t == 0)
    def _():
        ext_scr[:, 0:POOL_HALO, :] = halo_ref[...]

    @pl.when(t > 0)
    def _():
        ext_scr[:, 0:POOL_HALO, :] = ext_scr[:, r:r + POOL_HALO, :]

    ext_scr[:, POOL_HALO:POOL_HALO + r, :] = u_ref[...]

    pos = pos0 + t * r + lax.broadcasted_iota(jnp.int32, (1, r, grp), 1)
    for g, w in enumerate(POOL_WINDOWS):
        cols = slice(g * grp, (g + 1) * grp)
        tot = ext_scr[:, POOL_HALO:POOL_HALO + r, cols]
        for d in range(1, w):
            tot = tot + ext_scr[:, POOL_HALO - d:POOL_HALO - d + r, cols]
        cnt = jnp.minimum(w, pos + 1).astype(F32)
        y = tot / cnt - u_ref[:, :, cols]
        y = jnp.dot(y.reshape(gb * r, grp).astype(BF16), w_ref[g], preferred_element_type=F32)
        o_ref[:, cols] = (y * ls_ref[:, cols]).astype(o_ref.dtype)


def _pool(u, halo, w_pool_bf, ls_pool, n_seq, seq, gb, r, pos0):
    width = u.shape[1]
    n_g, n_t = n_seq // gb, seq // r
    u3 = u.reshape(n_seq, seq, width)
    return pl.pallas_call(
        functools.partial(_pool_kernel, pos0=pos0),
        grid=(n_g, n_t),
        in_specs=[pl.BlockSpec((gb, r, width), lambda g, t: (g, t, 0)),
                  pl.BlockSpec((gb, POOL_HALO, width), lambda g, t: (g, 0, 0)),
                  pl.BlockSpec(w_pool_bf.shape, lambda g, t: (0, 0, 0)),
                  pl.BlockSpec((1, width), lambda g, t: (0, 0))],
        out_specs=pl.BlockSpec((gb * r, width), lambda g, t: (g * n_t + t, 0)),
        out_shape=jax.ShapeDtypeStruct(u.shape, BF16),
        scratch_shapes=[pltpu.VMEM((gb, r + POOL_HALO, width), F32)],
        compiler_params=_params("arbitrary", "arbitrary"),
        name="pool_mixer",
    )(u3, halo, w_pool_bf, ls_pool.reshape(1, width))


def _bf16_tiles(w_refs, copy_refs):
    tiles = []
    for i, w_ref in enumerate(w_refs):
        w = w_ref[...]
        if w.dtype != BF16:
            w = w.astype(BF16)
            copy_refs[i][...] = w
        tiles.append(w)
    return tiles


def _merge_kernel(x_ref, sh_ref, sc_ref, gt_ref, g1_ref, attn_ref, pool_ref,
                  wga_ref, wgb_ref, woa_ref, wob_ref, wout_ref, o_ref, *rest):
    *copy_refs, h_scr, acc_scr = rest
    c = pl.program_id(2)

    @pl.when(c == 0)
    def _():
        _mod_norm_store(x_ref, g1_ref, sc_ref, sh_ref, h_scr)
        acc_scr[...] = jnp.zeros_like(acc_scr)

    wga, wgb, woa, wob, wout = _bf16_tiles((wga_ref, wgb_ref, woa_ref, wob_ref, wout_ref), copy_refs)
    h = h_scr[...]
    g_a = jax.nn.sigmoid(jnp.dot(h, wga, preferred_element_type=F32))
    g_b = jax.nn.sigmoid(jnp.dot(h, wgb, preferred_element_type=F32))
    y_a = jnp.dot(attn_ref[...].astype(BF16), woa, preferred_element_type=F32)
    y_b = jnp.dot(pool_ref[...].astype(BF16), wob, preferred_element_type=F32)
    mix_in = (g_a * y_a + g_b * y_b).astype(BF16)
    acc_scr[...] += jnp.dot(mix_in, wout, preferred_element_type=F32)

    @pl.when(c == pl.num_programs(2) - 1)
    def _():
        o_ref[...] = x_ref[...] + gt_ref[...] * acc_scr[...].reshape(o_ref.shape)


def _column_weight_specs(weights, tn, n_c):
    in_specs, copy_specs, copy_shapes = [], [], []
    for w, axis, first in weights:
        if axis == 1:
            in_specs.append(pl.BlockSpec((w.shape[0], tn), lambda g, t, c, first=first: (0, first + c)))
            copy_specs.append(pl.BlockSpec((w.shape[0], tn), lambda g, t, c: (0, c)))
            copy_shapes.append(jax.ShapeDtypeStruct((w.shape[0], n_c * tn), BF16))
        else:
            in_specs.append(pl.BlockSpec((tn, w.shape[1]), lambda g, t, c, first=first: (first + c, 0)))
            copy_specs.append(pl.BlockSpec((tn, w.shape[1]), lambda g, t, c: (c, 0)))
            copy_shapes.append(jax.ShapeDtypeStruct((n_c * tn, w.shape[1]), BF16))
    if all(w.dtype == BF16 for w, _, _ in weights):
        return in_specs, [], []
    return in_specs, copy_specs, copy_shapes


def _merge(x3, mod, g_norm1, attn, pooled, weights, gb, r, tn):
    n_g, n_t, d = _group_specs(x3, gb, r)
    tm = gb * r
    aw = attn.shape[1]
    pw = pooled.shape[1]
    n_c = d // tn
    w_specs, copy_specs, copy_shapes = _column_weight_specs(weights, tn, n_c)
    if copy_specs:
        assert n_g * n_t == 1, "weight copies are written by a single token tile"
    x_spec = pl.BlockSpec((gb, r, d), lambda g, t, c: (g, t, 0))
    mod_spec = lambda k: pl.BlockSpec((gb, 1, d), lambda g, t, c: (g, 0, k))
    out, *copies = pl.pallas_call(
        _merge_kernel,
        grid=(n_g, n_t, n_c),
        in_specs=[x_spec, mod_spec(0), mod_spec(1), mod_spec(2),
                  pl.BlockSpec((1, d), lambda g, t, c: (0, 0)),
                  pl.BlockSpec((tm, aw), lambda g, t, c: (g * n_t + t, 0)),
                  pl.BlockSpec((tm, pw), lambda g, t, c: (g * n_t + t, 0))] + w_specs,
        out_specs=[x_spec] + copy_specs,
        out_shape=[jax.ShapeDtypeStruct(x3.shape, F32)] + copy_shapes,
        scratch_shapes=[pltpu.VMEM((tm, d), BF16), pltpu.VMEM((tm, d), F32)],
        compiler_params=_params("arbitrary", "arbitrary", "arbitrary"),
        name="merge_out",
    )(x3, mod, mod, mod, g_norm1.reshape(1, d), attn, pooled, *[w for w, _, _ in weights])
    return out, copies


def _ffn_kernel(x_ref, sh_ref, sc_ref, gt_ref, g2_ref, wa_ref, wb_ref, wo_ref, o_ref, *rest):
    *copy_refs, h_scr, acc_scr = rest
    c = pl.program_id(2)

    @pl.when(c == 0)
    def _():
        _mod_norm_store(x_ref, g2_ref, sc_ref, sh_ref, h_scr)
        acc_scr[...] = jnp.zeros_like(acc_scr)

    wa, wb, wo = _bf16_tiles((wa_ref, wb_ref, wo_ref), copy_refs)
    h = h_scr[...]
    a = jnp.dot(h, wa, preferred_element_type=F32)
    b = jnp.dot(h, wb, preferred_element_type=F32)
    hid = (jax.nn.silu(a) * b).astype(BF16)
    acc_scr[...] += jnp.dot(hid, wo, preferred_element_type=F32)

    @pl.when(c == pl.num_programs(2) - 1)
    def _():
        o_ref[...] = x_ref[...] + gt_ref[...] * acc_scr[...].reshape(o_ref.shape)


def _ffn(x3, mod, g_norm2, weights, hidden, gb, r, tn):
    n_g, n_t, d = _group_specs(x3, gb, r)
    tm = gb * r
    n_c = hidden // tn
    w_specs, copy_specs, copy_shapes = _column_weight_specs(weights, tn, n_c)
    if copy_specs:
        assert n_g * n_t == 1, "weight copies are written by a single token tile"
    x_spec = pl.BlockSpec((gb, r, d), lambda g, t, c: (g, t, 0))
    mod_spec = lambda k: pl.BlockSpec((gb, 1, d), lambda g, t, c: (g, 0, k))
    out, *copies = pl.pallas_call(
        _ffn_kernel,
        grid=(n_g, n_t, n_c),
        in_specs=[x_spec, mod_spec(3), mod_spec(4), mod_spec(5),
                  pl.BlockSpec((1, d), lambda g, t, c: (0, 0))] + w_specs,
        out_specs=[x_spec] + copy_specs,
        out_shape=[jax.ShapeDtypeStruct(x3.shape, F32)] + copy_shapes,
        scratch_shapes=[pltpu.VMEM((tm, d), BF16), pltpu.VMEM((tm, d), F32)],
        compiler_params=_params("arbitrary", "arbitrary", "arbitrary"),
        name="ffn_swiglu",
    )(x3, mod, mod, mod, g_norm2.reshape(1, d), *[w for w, _, _ in weights])
    return out, copies


def _rope_tables(pos):
    half = HEAD_DIM // 2
    inv = ROPE_THETA ** (-jnp.arange(half, dtype=F32) / half)
    ang = pos.astype(F32)[:, None] * inv[None, :]
    cos, sin = jnp.cos(ang), jnp.sin(ang)
    return jnp.concatenate([cos, cos], axis=-1), jnp.concatenate([-sin, sin], axis=-1)


def _pick_rows(seq, target=512):
    r = min(seq, target)
    while seq % r:
        r //= 2
    return r


def kernel(x_prompt, x_sample, c_prompt, c_sample, cache_k, cache_v, state_pool, page_table, w_ada, b_ada, g_norm1, g_norm2, w_in, g_qnorm, g_knorm, w_pool, ls_pool, w_o_attn, w_o_pool, w_out, w_ffn_in, w_ffn_out):
    n_p, seq, d = x_prompt.shape
    n_s, dq, _ = x_sample.shape
    past = page_table.shape[1] * cache_k.shape[1]
    pool_w = state_pool.shape[2]
    n_state = state_pool.shape[1]

    w_qkvu_bf = w_in[:, :3 * ATTN_WIDTH + pool_w].astype(BF16)
    w_pool_bf = w_pool.astype(BF16)
    hidden = w_ffn_out.shape[0]

    n_c = n_p + n_s
    c_all = jnp.concatenate([c_prompt, c_sample], axis=0)
    c_all = jnp.pad(c_all, ((0, (-n_c) % 8), (0, 0)))
    mod = _ada(c_all, w_ada, b_ada)
    mod_p = mod[:n_p].reshape(n_p, 1, 6 * d)
    mod_s = mod[n_p:n_c].reshape(n_s, 1, 6 * d)

    cos_s, sin_s = _rope_tables(past + jnp.arange(dq))
    cos_s, sin_s = jnp.tile(cos_s, (n_s, 1)), jnp.tile(sin_s, (n_s, 1))
    q_s, k_s, v_s, u_s = _qkvu(x_sample, mod_s, g_norm1, w_qkvu_bf, g_qnorm, g_knorm, cos_s, sin_s, n_s, dq)
    attn_s = _moba_sample(q_s, k_s, v_s, cache_k, cache_v, page_table)
    halo_s = jnp.pad(state_pool, ((0, 0), (POOL_HALO - n_state, 0), (0, 0)))
    pooled_s = _pool(u_s, halo_s, w_pool_bf, ls_pool, n_s, dq, n_s, dq, past)
    gate0 = (w_in.shape[1] - 2 * d) // _TN_SAMPLE
    x1_s, merge_bf = _merge(
        x_sample, mod_s, g_norm1, attn_s, pooled_s,
        ((w_in, 1, gate0), (w_in, 1, gate0 + d // _TN_SAMPLE), (w_o_attn, 1, 0), (w_o_pool, 1, 0),
         (w_out, 0, 0)), n_s, dq, _TN_SAMPLE)
    y_s, ffn_bf = _ffn(
        x1_s, mod_s, g_norm2,
        ((w_ffn_in, 1, 0), (w_ffn_in, 1, hidden // _TN_SAMPLE), (w_ffn_out, 0, 0)),
        hidden, n_s, dq, _TN_SAMPLE)

    r_p = _pick_rows(seq)
    cos_p, sin_p = _rope_tables(jnp.arange(seq))
    q_p, k_p, v_p, u_p = _qkvu(x_prompt, mod_p, g_norm1, w_qkvu_bf, g_qnorm, g_knorm, cos_p, sin_p, 1, r_p)
    attn_p = _moba_prompt(q_p, k_p, v_p, n_p, seq)
    pooled_p = _pool(u_p, jnp.zeros((n_p, POOL_HALO, pool_w), F32), w_pool_bf, ls_pool,
                     n_p, seq, 1, r_p, 0)
    x1_p, _ = _merge(x_prompt, mod_p, g_norm1, attn_p, pooled_p,
                     tuple((w, axis, 0) for w, axis in zip(merge_bf, (1, 1, 1, 1, 0))), 1, r_p, _TN_PROMPT)
    y_p, _ = _ffn(x1_p, mod_p, g_norm2, tuple((w, axis, 0) for w, axis in zip(ffn_bf, (1, 1, 0))),
                  hidden, 1, r_p, _TN_PROMPT)

    k_prompt = k_p.reshape(n_p, seq, N_HEADS, HEAD_DIM)
    v_prompt = v_p.reshape(n_p, seq, N_HEADS, HEAD_DIM)
    pool_prompt = u_p.reshape(n_p, seq, pool_w)[:, seq - n_state:]
    k_sample = k_s.reshape(n_s, dq, N_HEADS, HEAD_DIM)
    v_sample = v_s.reshape(n_s, dq, N_HEADS, HEAD_DIM)
    pool_sample = jnp.concatenate([state_pool, u_s.reshape(n_s, dq, pool_w)], axis=1)[:, -n_state:]
    return (y_p, y_s, k_prompt, v_prompt, pool_prompt, k_sample, v_sample, pool_sample)
```

```python
import functools

import jax
import jax.numpy as jnp
from jax import lax
from jax.experimental import pallas as pl
from jax.experimental.pallas import tpu as pltpu

N_HEADS = 8
HEAD_DIM = 128
ATTN_WIDTH = N_HEADS * HEAD_DIM
MOBA_BLOCK = 256
MOBA_TOPK = 3
ROPE_THETA = 10000.0
POOL_WINDOWS = (2, 4, 8, 16)
POOL_HALO = 16
RMS_EPS = 1e-6
_TILE_GROUP = 4
_MASKED_MAX = -1e30
_TN_PROMPT = 512
_TN_SAMPLE = 256

F32 = jnp.float32
BF16 = jnp.bfloat16
NEG_INF = float("-inf")
_LOG2_E = 1.4426950408889634

_VMEM_LIMIT = 52 * 1024 * 1024


def _params(*sem):
    return pltpu.CompilerParams(dimension_semantics=sem, vmem_limit_bytes=_VMEM_LIMIT)


_NORM_ROWS = 16


def _mod_norm_store(x_ref, g_ref, sc_ref, sh_ref, h_scr):
    gb, r, d = x_ref.shape
    g = g_ref[...]
    if r >= _NORM_ROWS:
        per_group = r // _NORM_ROWS
        chunks = [(gi, 1, c * _NORM_ROWS, _NORM_ROWS) for gi in range(gb) for c in range(per_group)]
    else:
        n_g = _NORM_ROWS // r
        chunks = [(gi, n_g, 0, r) for gi in range(0, gb, n_g)]
    gains = {}
    for gi, n_g, r0, nr in chunks:
        if gi not in gains:
            gains[gi] = g * (1.0 + sc_ref[gi:gi + n_g])
        x = x_ref[gi:gi + n_g, r0:r0 + nr, :]
        ms = jnp.mean(x * x, axis=-1, keepdims=True)
        h = x * lax.rsqrt(ms + RMS_EPS) * gains[gi] + sh_ref[gi:gi + n_g]
        row0 = gi * r + r0
        h_scr[row0:row0 + n_g * nr, :] = h.reshape(n_g * nr, d).astype(BF16)


def _ada_kernel(c_ref, w_ref, b_ref, o_ref):
    o_ref[...] = jnp.dot(c_ref[...].astype(BF16), w_ref[...].astype(BF16),
                         preferred_element_type=F32) + b_ref[...]


def _ada(c_all, w_ada, b_ada, tn=1024):
    m, d = c_all.shape
    n = w_ada.shape[1]
    return pl.pallas_call(
        _ada_kernel,
        grid=(n // tn,),
        in_specs=[pl.BlockSpec((m, d), lambda j: (0, 0)),
                  pl.BlockSpec((d, tn), lambda j: (0, j)),
                  pl.BlockSpec((1, tn), lambda j: (0, j))],
        out_specs=pl.BlockSpec((m, tn), lambda j: (0, j)),
        out_shape=jax.ShapeDtypeStruct((m, n), F32),
        compiler_params=_params("arbitrary"),
        name="ada_mod",
    )(c_all, w_ada, b_ada.reshape(1, n))


def _head_norm_rope(th, g, cos, sin_signed):
    ms = jnp.mean(th * th, axis=-1, keepdims=True)
    y = th * lax.rsqrt(ms + RMS_EPS) * g
    return y * cos + pltpu.roll(y, HEAD_DIM // 2, 1) * sin_signed


_PROJ_CHUNK = 2 * HEAD_DIM


def _qkvu_kernel(x_ref, sh_ref, sc_ref, g1_ref, w_ref, gq_ref, gk_ref, cos_ref, sin_ref,
                 q_ref, k_ref, v_ref, u_ref, h_scr):
    _mod_norm_store(x_ref, g1_ref, sc_ref, sh_ref, h_scr)

    width = q_ref.shape[1]
    outs = ((q_ref, gq_ref), (k_ref, gk_ref), (v_ref, None), (u_ref, None))
    for i, (o_ref, g_ref) in enumerate(outs):
        for c in range(0, width, _PROJ_CHUNK):
            col = i * width + c
            acc = jnp.dot(h_scr[...], w_ref[:, col:col + _PROJ_CHUNK], preferred_element_type=F32)
            if g_ref is None:
                o_ref[:, c:c + _PROJ_CHUNK] = acc
            else:
                for hh in range(0, _PROJ_CHUNK, HEAD_DIM):
                    o_ref[:, c + hh:c + hh + HEAD_DIM] = _head_norm_rope(
                        acc[:, hh:hh + HEAD_DIM], g_ref[...], cos_ref[...], sin_ref[...])


def _group_specs(x3, gb, r):
    g_total, s, d = x3.shape
    n_g, n_t = g_total // gb, s // r
    return n_g, n_t, d


def _qkvu(x3, mod, g_norm1, w_in_bf, g_q, g_k, cos, sin_signed, gb, r):
    n_g, n_t, d = _group_specs(x3, gb, r)
    tm = gb * r
    tokens = x3.shape[0] * x3.shape[1]
    wq = ATTN_WIDTH
    x_spec = pl.BlockSpec((gb, r, d), lambda g, t: (g, t, 0))
    mod_spec = lambda k: pl.BlockSpec((gb, 1, d), lambda g, t: (g, 0, k))
    row_spec = pl.BlockSpec((1, d), lambda g, t: (0, 0))
    head_spec = pl.BlockSpec((1, HEAD_DIM), lambda g, t: (0, 0))
    tab_spec = pl.BlockSpec((tm, HEAD_DIM), lambda g, t: (t, 0))
    out_spec = pl.BlockSpec((tm, wq), lambda g, t: (g * n_t + t, 0))
    out_sds = jax.ShapeDtypeStruct((tokens, wq), F32)
    w_spec = pl.BlockSpec((d, 4 * wq), lambda g, t: (0, 0), pipeline_mode=pl.Buffered(1))
    return pl.pallas_call(
        _qkvu_kernel,
        grid=(n_g, n_t),
        in_specs=[x_spec, mod_spec(0), mod_spec(1), row_spec, w_spec,
                  head_spec, head_spec, tab_spec, tab_spec],
        out_specs=[out_spec] * 4,
        out_shape=[out_sds] * 4,
        scratch_shapes=[pltpu.VMEM((tm, d), BF16)],
        compiler_params=_params("arbitrary", "arbitrary"),
        name="qkvu_proj",
    )(x3, mod, mod, g_norm1.reshape(1, d), w_in_bf, g_q.reshape(1, HEAD_DIM),
      g_k.reshape(1, HEAD_DIM), cos, sin_signed)


def _topk_select(gate, valid, axis):
    n = gate.shape[axis]
    g = jnp.where(valid, gate, NEG_INF)
    idx = lax.broadcasted_iota(jnp.int32, gate.shape, axis)
    rank = jnp.zeros(gate.shape, jnp.int32)
    for m in range(n):
        gm = g[m:m + 1, :] if axis == 0 else g[:, m:m + 1]
        beats = (gm > g) | ((gm == g) & (m < idx))
        rank = rank + beats.astype(jnp.int32)
    return valid & (rank < MOBA_TOPK) & (jnp.abs(g) < float("inf"))


def _moba_prompt_kernel(q_ref, k_ref, v_ref, o_ref, kb_scr, vt_scr, kmean_scr, bias_scr, tri_scr, s_scr,
                        *, nb):
    qi = pl.program_id(2)
    blk = MOBA_BLOCK

    @pl.when(qi == 0)
    def _():
        for n in range(nb):
            kn = k_ref[n * blk:(n + 1) * blk, :]
            kb_scr[n * blk:(n + 1) * blk, :] = kn.astype(BF16)
            kmean_scr[n:n + 1, :] = jnp.mean(kn, axis=0, keepdims=True)
            vt_scr[n] = v_ref[n * blk:(n + 1) * blk, :].T.astype(BF16)
        kb_scr[nb * blk:, :] = jnp.zeros((kb_scr.shape[0] - nb * blk, HEAD_DIM), BF16)
        kmean = kmean_scr[...]
        for qb_i in range(nb):
            gate = lax.dot_general(kmean, q_ref[qb_i * blk:(qb_i + 1) * blk, :],
                                   (((1,), (1,)), ((), ())),
                                   precision=lax.Precision.HIGHEST, preferred_element_type=F32)
            n_idx = lax.broadcasted_iota(jnp.int32, gate.shape, 0)
            sel = _topk_select(gate, n_idx < qb_i, axis=0)
            bias_scr[qb_i, 0:nb, :] = jnp.where(sel | (n_idx == qb_i), 0.0, NEG_INF)
            bias_scr[qb_i, nb:, :] = jnp.full((bias_scr.shape[1] - nb, blk), NEG_INF, F32)
        kpos = lax.broadcasted_iota(jnp.int32, (blk, blk), 0)
        qpos = lax.broadcasted_iota(jnp.int32, (blk, blk), 1)
        tri_scr[0] = jnp.zeros((blk, blk), F32)
        tri_scr[1] = jnp.where(kpos <= qpos, 0.0, NEG_INF)

    qb = q_ref[pl.ds(pl.multiple_of(qi * blk, blk), blk), :].astype(BF16)
    exp2_scale = HEAD_DIM ** -0.5 * _LOG2_E
    width = _TILE_GROUP

    def fold(x, op):
        return op(x.reshape(blk // 8, 8, blk), axis=0)

    n_groups = (qi + width) // width

    def score_group(g):
        j0 = g * width
        keys = kb_scr[pl.ds(pl.multiple_of(j0 * blk, blk), width * blk), :]
        s_all = lax.dot_general(keys, qb, (((1,), (1,)), ((), ())), preferred_element_type=F32)
        m8 = jnp.full((8, blk), NEG_INF, F32)
        for t in range(width):
            j = j0 + t
            s = (s_all[t * blk:(t + 1) * blk, :] + bias_scr[qi, pl.ds(j, 1), :]
                 + tri_scr[(j == qi).astype(jnp.int32)])
            s_scr[g % 2, t] = s
            m8 = jnp.maximum(m8, fold(s, jnp.max))
        return m8

    def absorb_group(g, m_run, l8, acc, m8):
        m_new = jnp.maximum(m_run, jnp.max(m8, axis=0, keepdims=True))
        alpha = jnp.exp2((m_run - m_new) * exp2_scale)
        l8 = l8 * alpha
        acc = acc * alpha
        for t in range(width):
            p = jnp.exp2((s_scr[g % 2, t] - m_new) * exp2_scale)
            l8 = l8 + fold(p, jnp.sum)
            acc = acc + jnp.dot(vt_scr[jnp.minimum(g * width + t, nb - 1)], p.astype(BF16),
                                preferred_element_type=F32)
        return m_new, l8, acc

    def body(g, carry):
        m_run, l8, acc, m8 = carry
        m_run, l8, acc = absorb_group(g, m_run, l8, acc, m8)
        return m_run, l8, acc, score_group(g + 1)

    carry = (jnp.full((1, blk), _MASKED_MAX, F32), jnp.zeros((8, blk), F32),
             jnp.zeros((HEAD_DIM, blk), F32), score_group(0))
    carry = lax.fori_loop(0, n_groups - 1, body, carry)
    _, l8, acc = absorb_group(n_groups - 1, *carry)
    l = jnp.sum(l8, axis=0, keepdims=True)
    o_ref[...] = (acc / l).T.astype(o_ref.dtype)


def _moba_prompt(q, k, v, n_seq, seq):
    nb = seq // MOBA_BLOCK
    nb_pad = -(-nb // _TILE_GROUP) * _TILE_GROUP + _TILE_GROUP
    blk = MOBA_BLOCK
    kv_spec = pl.BlockSpec((seq, HEAD_DIM), lambda b, h, i: (b, h))
    q_spec = pl.BlockSpec((blk, HEAD_DIM), lambda b, h, i: (b * nb + i, h))
    return pl.pallas_call(
        functools.partial(_moba_prompt_kernel, nb=nb),
        grid=(n_seq, N_HEADS, nb),
        in_specs=[kv_spec, kv_spec, kv_spec],
        out_specs=q_spec,
        out_shape=jax.ShapeDtypeStruct(q.shape, BF16),
        scratch_shapes=[pltpu.VMEM((nb_pad * blk, HEAD_DIM), BF16),
                        pltpu.VMEM((nb, HEAD_DIM, blk), BF16),
                        pltpu.VMEM((nb, HEAD_DIM), F32),
                        pltpu.VMEM((nb, nb_pad, blk), F32),
                        pltpu.VMEM((2, blk, blk), F32),
                        pltpu.VMEM((2, _TILE_GROUP, blk, blk), F32)],
        compiler_params=_params("arbitrary", "arbitrary", "arbitrary"),
        name="moba_prompt",
    )(q, k, v)


def _page_copy_groups(cache_hbm, pt_ref, buf, sem, slot, b, first_page, hp, n_pages, ppb, wanted):
    groups = []
    for p0 in range(0, hp, ppb):
        pgs = [pt_ref[b * n_pages + first_page + p0 + i] for i in range(ppb)]
        for h in range(N_HEADS):
            pred = None if wanted is None else wanted(b, (first_page + p0) // ppb, h)
            groups.append((pred, [pltpu.make_async_copy(cache_hbm.at[pgs[i], :, h, :],
                                                        buf.at[slot, p0 + i, h], sem.at[slot])
                                  for i in range(ppb)]))
    return groups


def _for_each_copy(groups, method):
    for pred, copies in groups:
        def run(copies=copies):
            for c in copies:
                getattr(c, method)()
        if pred is None:
            run()
        else:
            pl.when(pred)(run)


def _paged_slots(pt_ref, cache_hbm, buf, sem, hp, n_pages, ppb, compute, wanted=None):
    n_slots = buf.shape[0]
    b, step = pl.program_id(0), pl.program_id(1)
    n_b, n_steps = pl.num_programs(0), pl.num_programs(1)
    is_first = jnp.logical_and(b == 0, step == 0)
    is_last = jnp.logical_and(b == n_b - 1, step == n_steps - 1)
    wrap = step == n_steps - 1
    nxt_b = jnp.where(wrap, b + 1, b)
    nxt_step = jnp.where(wrap, 0, step + 1)

    def groups(bb, ss, slot):
        return _page_copy_groups(cache_hbm, pt_ref, buf, sem, slot, bb, (n_slots * ss + slot) * hp,
                                 hp, n_pages, ppb, wanted)

    @pl.when(is_first)
    def _():
        for slot in range(n_slots):
            _for_each_copy(groups(b, step, slot), "start")

    for slot in range(n_slots):
        _for_each_copy(groups(b, step, slot), "wait")
        compute(slot, (n_slots * step + slot) * hp)

        @pl.when(jnp.logical_not(is_last))
        def _():
            _for_each_copy(groups(nxt_b, nxt_step, slot), "start")


def _moba_sample_k_kernel(pt_ref, q_ref, knew_ref, vnew_ref, cache_hbm, p_ref, oown_ref, linv_ref,
                          used_ref, buf, sem, s_scr, kpart_scr, qbd_scr, *, hp, page, n_blocks):
    step = pl.program_id(1)
    n_steps = pl.num_programs(1)
    dq = q_ref.shape[0]
    rows = N_HEADS * dq
    scale = HEAD_DIM ** -0.5
    pages_per_block = MOBA_BLOCK // page
    n_pages = n_blocks * pages_per_block

    qh = [q_ref[:, h * HEAD_DIM:(h + 1) * HEAD_DIM] for h in range(N_HEADS)]
    qh_bf = [x.astype(BF16) for x in qh]

    @pl.when(step == 0)
    def _():
        kpart_scr[...] = jnp.zeros_like(kpart_scr)
        q_rep = jnp.concatenate([q_ref[...]] * N_HEADS
                                + [jnp.zeros((qbd_scr.shape[1] - rows, ATTN_WIDTH), F32)], axis=0)
        row_head = lax.broadcasted_iota(jnp.int32, q_rep.shape, 0) // dq
        lane_head = lax.broadcasted_iota(jnp.int32, q_rep.shape, 1) // HEAD_DIM
        qbd_scr[...] = jnp.where(row_head == lane_head, q_rep, 0.0).T.astype(BF16)

    def compute(slot, first_page):
        heads = [[buf[slot, p, h] for h in range(N_HEADS)] for p in range(hp)]
        for p in range(hp):
            blk = (first_page + p) // pages_per_block
            for h in range(N_HEADS):
                kpart_scr[blk, h] += jnp.sum(heads[p][h].reshape(page // 8, 8, HEAD_DIM), axis=0)
        for p0 in range(0, hp, hp // 2):
            keys = jnp.concatenate([jnp.concatenate(hs, axis=1) for hs in heads[p0:p0 + hp // 2]],
                                   axis=0)
            s_t = jnp.dot(keys.astype(BF16), qbd_scr[...], preferred_element_type=F32) * scale
            for i in range(hp // 2):
                s_scr[first_page + p0 + i] = s_t[i * page:(i + 1) * page, :].T[:rows, :]

    _paged_slots(pt_ref, cache_hbm, buf, sem, hp, n_pages, pages_per_block, compute)

    @pl.when(step == n_steps - 1)
    def _():
        gates = []
        for h in range(N_HEADS):
            kmean_h = jnp.sum(kpart_scr[:, h], axis=1) * (1.0 / MOBA_BLOCK)
            gates.append(lax.dot_general(qh[h], kmean_h, (((1,), (1,)), ((), ())),
                                         precision=lax.Precision.HIGHEST,
                                         preferred_element_type=F32))
        gate = jnp.concatenate(gates, axis=0)
        sel = _topk_select(gate, jnp.ones(gate.shape, jnp.bool_), axis=1)
        bias = jnp.where(sel, 0.0, NEG_INF)
        used_ref[0] = jnp.concatenate(
            [jnp.max(sel[h * dq:(h + 1) * dq, :].astype(jnp.int32), axis=0, keepdims=True)
             for h in range(N_HEADS)], axis=0)

        s_own = []
        for h in range(N_HEADS):
            kn = knew_ref[:, h * HEAD_DIM:(h + 1) * HEAD_DIM].astype(BF16)
            s_own.append(lax.dot_general(qh_bf[h], kn, (((1,), (1,)), ((), ())),
                                         preferred_element_type=F32) * scale)
        s_own = jnp.concatenate(s_own, axis=0)
        row = lax.broadcasted_iota(jnp.int32, s_own.shape, 0) % dq
        col = lax.broadcasted_iota(jnp.int32, s_own.shape, 1)
        s_own = jnp.where(col <= row, s_own, NEG_INF)

        m_vec = jnp.full((rows, page), NEG_INF, F32)
        for g in range(n_pages):
            n = g // pages_per_block
            m_vec = jnp.maximum(m_vec, s_scr[g] + bias[:, n:n + 1])
        m = jnp.maximum(jnp.max(m_vec, axis=1, keepdims=True), jnp.max(s_own, axis=1, keepdims=True))
        p_own = jnp.exp(s_own - m)
        l_vec = jnp.zeros((rows, page), F32)
        for g in range(n_pages):
            n = g // pages_per_block
            pb = jnp.exp(s_scr[g] + bias[:, n:n + 1] - m)
            l_vec = l_vec + pb
            p_ref[0, :, g * page:(g + 1) * page] = pb
        l = jnp.sum(l_vec, axis=1, keepdims=True) + jnp.sum(p_own, axis=1, keepdims=True)
        linv_ref[0] = jnp.broadcast_to(1.0 / l, (rows, HEAD_DIM))
        for h in range(N_HEADS):
            vn = vnew_ref[:, h * HEAD_DIM:(h + 1) * HEAD_DIM].astype(BF16)
            oown_ref[0, h * dq:(h + 1) * dq, :] = jnp.dot(
                p_own[h * dq:(h + 1) * dq, :].astype(BF16), vn, preferred_element_type=F32)


def _moba_sample_v_kernel(pt_ref, used_ref, p_ref, oown_ref, linv_ref, cache_hbm, o_ref, buf, sem,
                          acc_scr, *, hp, page, n_pages):
    step = pl.program_id(1)
    n_steps = pl.num_programs(1)
    dq = o_ref.shape[0]
    ppb = MOBA_BLOCK // page
    n_blocks = n_pages // ppb

    @pl.when(jnp.logical_and(pl.program_id(0) == 0, step == 0))
    def _():
        buf[...] = jnp.zeros_like(buf)

    def wanted(b, blk, h):
        return used_ref[(b * N_HEADS + h) * n_blocks + blk] != 0

    @pl.when(step == 0)
    def _():
        acc_scr[...] = oown_ref[0]

    def compute(slot, first_page):
        for h in range(N_HEADS):
            tot = jnp.zeros((dq, HEAD_DIM), F32)
            for p in range(hp):
                col0 = (slot * hp + p) * page
                ph = p_ref[0, h * dq:(h + 1) * dq, col0:col0 + page].astype(BF16)
                tot = tot + jnp.dot(ph, buf[slot, p, h].astype(BF16), preferred_element_type=F32)
            acc_scr[h * dq:(h + 1) * dq, :] += tot

    _paged_slots(pt_ref, cache_hbm, buf, sem, hp, n_pages, ppb, compute, wanted)

    @pl.when(step == n_steps - 1)
    def _():
        for h in range(N_HEADS):
            rows_h = slice(h * dq, (h + 1) * dq)
            o_ref[:, h * HEAD_DIM:(h + 1) * HEAD_DIM] = acc_scr[rows_h, :] * linv_ref[0, rows_h, :]


def _moba_sample(q, k_new, v_new, cache_k, cache_v, page_table, hp=4, n_slots=4):
    dec_b, n_pages = page_table.shape
    page = cache_k.shape[1]
    dq = q.shape[0] // dec_b
    past = n_pages * page
    n_blocks = past // MOBA_BLOCK
    while n_pages % (n_slots * hp):
        n_slots //= 2
    n_steps = n_pages // (n_slots * hp)
    rows = N_HEADS * dq
    pt = page_table.reshape(-1)

    new_spec = pl.BlockSpec((dq, ATTN_WIDTH), lambda b, s, pt_ref: (b, 0))
    any_spec = pl.BlockSpec(memory_space=pl.ANY)
    stat_spec = pl.BlockSpec((1, rows, HEAD_DIM), lambda b, s, pt_ref: (b, 0, 0))
    stat_sds = jax.ShapeDtypeStruct((dec_b, rows, HEAD_DIM), F32)
    assert hp % (MOBA_BLOCK // page) == 0, "a page half must hold whole key blocks"
    page_buf = pltpu.VMEM((n_slots, hp, N_HEADS, page, HEAD_DIM), F32)
    probs, o_own, l_inv, used = pl.pallas_call(
        functools.partial(_moba_sample_k_kernel, hp=hp, page=page, n_blocks=n_blocks),
        grid_spec=pltpu.PrefetchScalarGridSpec(
            num_scalar_prefetch=1,
            grid=(dec_b, n_steps),
            in_specs=[new_spec, new_spec, new_spec, any_spec],
            out_specs=[pl.BlockSpec((1, rows, past), lambda b, s, pt_ref: (b, 0, 0)),
                       stat_spec, stat_spec,
                       pl.BlockSpec((1, N_HEADS, n_blocks), lambda b, s, pt_ref: (b, 0, 0))],
            scratch_shapes=[page_buf, pltpu.SemaphoreType.DMA((n_slots,)),
                            pltpu.VMEM((n_pages, rows, page), F32),
                            pltpu.VMEM((n_blocks, N_HEADS, 8, HEAD_DIM), F32),
                            pltpu.VMEM((ATTN_WIDTH, max(rows, HEAD_DIM)), BF16)]),
        out_shape=[jax.ShapeDtypeStruct((dec_b, rows, past), F32), stat_sds, stat_sds,
                   jax.ShapeDtypeStruct((dec_b, N_HEADS, n_blocks), jnp.int32)],
        compiler_params=_params("arbitrary", "arbitrary"),
        name="moba_sample_k",
    )(pt, q, k_new, v_new, cache_k)

    return pl.pallas_call(
        functools.partial(_moba_sample_v_kernel, hp=hp, page=page, n_pages=n_pages),
        grid_spec=pltpu.PrefetchScalarGridSpec(
            num_scalar_prefetch=2,
            grid=(dec_b, n_steps),
            in_specs=[pl.BlockSpec((1, rows, n_slots * hp * page), lambda b, s, *_: (b, 0, s)),
                      pl.BlockSpec((1, rows, HEAD_DIM), lambda b, s, *_: (b, 0, 0)),
                      pl.BlockSpec((1, rows, HEAD_DIM), lambda b, s, *_: (b, 0, 0)),
                      any_spec],
            out_specs=pl.BlockSpec((dq, ATTN_WIDTH), lambda b, s, *_: (b, 0)),
            scratch_shapes=[page_buf, pltpu.SemaphoreType.DMA((n_slots,)),
                            pltpu.VMEM((rows, HEAD_DIM), F32)]),
        out_shape=jax.ShapeDtypeStruct(q.shape, F32),
        compiler_params=_params("arbitrary", "arbitrary"),
        name="moba_sample_v",
    )(pt, used.reshape(-1), probs, o_own, l_inv, cache_v)


def _pool_kernel(u_ref, halo_ref, w_ref, ls_ref, o_ref, ext_scr, *, pos0):
    t = pl.program_id(1)
    gb, r, width = u_ref.shape
    grp = width // len(POOL_WINDOWS)

    @pl.when(t == 0)
    def _():
        ext_scr[:, 0:POOL_HALO, :] = halo_ref[...]

    @pl.when(t > 0)
    def _():
        ext_scr[:, 0:POOL_HALO, :] = ext_scr[:, r:r + POOL_HALO, :]

    ext_scr[:, POOL_HALO:POOL_HALO + r, :] = u_ref[...]

    pos = pos0 + t * r + lax.broadcasted_iota(jnp.int32, (1, r, grp), 1)
    for g, w in enumerate(POOL_WINDOWS):
        cols = slice(g * grp, (g + 1) * grp)
        tot = ext_scr[:, POOL_HALO:POOL_HALO + r, cols]
        for d in range(1, w):
            tot = tot + ext_scr[:, POOL_HALO - d:POOL_HALO - d + r, cols]
        cnt = jnp.minimum(w, pos + 1).astype(F32)
        y = tot / cnt - u_ref[:, :, cols]
        y = jnp.dot(y.reshape(gb * r, grp).astype(BF16), w_ref[g], preferred_element_type=F32)
        o_ref[:, cols] = (y * ls_ref[:, cols]).astype(o_ref.dtype)


def _pool(u, halo, w_pool_bf, ls_pool, n_seq, seq, gb, r, pos0):
    width = u.shape[1]
    n_g, n_t = n_seq // gb, seq // r
    u3 = u.reshape(n_seq, seq, width)
    return pl.pallas_call(
        functools.partial(_pool_kernel, pos0=pos0),
        grid=(n_g, n_t),
        in_specs=[pl.BlockSpec((gb, r, width), lambda g, t: (g, t, 0)),
                  pl.BlockSpec((gb, POOL_HALO, width), lambda g, t: (g, 0, 0)),
                  pl.BlockSpec(w_pool_bf.shape, lambda g, t: (0, 0, 0)),
                  pl.BlockSpec((1, width), lambda g, t: (0, 0))],
        out_specs=pl.BlockSpec((gb * r, width), lambda g, t: (g * n_t + t, 0)),
        out_shape=jax.ShapeDtypeStruct(u.shape, BF16),
        scratch_shapes=[pltpu.VMEM((gb, r + POOL_HALO, width), F32)],
        compiler_params=_params("arbitrary", "arbitrary"),
        name="pool_mixer",
    )(u3, halo, w_pool_bf, ls_pool.reshape(1, width))


def _bf16_tiles(w_refs, copy_refs):
    tiles = []
    for i, w_ref in enumerate(w_refs):
        w = w_ref[...]
        if w.dtype != BF16:
            w = w.astype(BF16)
            copy_refs[i][...] = w
        tiles.append(w)
    return tiles


def _merge_kernel(x_ref, sh_ref, sc_ref, gt_ref, g1_ref, attn_ref, pool_ref,
                  wga_ref, wgb_ref, woa_ref, wob_ref, wout_ref, o_ref, *rest):
    *copy_refs, h_scr = rest
    c = pl.program_id(2)

    @pl.when(c == 0)
    def _():
        _mod_norm_store(x_ref, g1_ref, sc_ref, sh_ref, h_scr)
        o_ref[...] = jnp.zeros_like(o_ref)

    wga, wgb, woa, wob, wout = _bf16_tiles((wga_ref, wgb_ref, woa_ref, wob_ref, wout_ref), copy_refs)
    h = h_scr[...]
    g_a = jax.nn.sigmoid(jnp.dot(h, wga, preferred_element_type=F32))
    g_b = jax.nn.sigmoid(jnp.dot(h, wgb, preferred_element_type=F32))
    y_a = jnp.dot(attn_ref[...].astype(BF16), woa, preferred_element_type=F32)
    y_b = jnp.dot(pool_ref[...].astype(BF16), wob, preferred_element_type=F32)
    mix_in = (g_a * y_a + g_b * y_b).astype(BF16)
    o_ref[...] += jnp.dot(mix_in, wout, preferred_element_type=F32).reshape(o_ref.shape)

    @pl.when(c == pl.num_programs(2) - 1)
    def _():
        o_ref[...] = x_ref[...] + gt_ref[...] * o_ref[...]


def _column_weight_specs(weights, tn, n_c):
    in_specs, copy_specs, copy_shapes = [], [], []
    for w, axis, first in weights:
        if axis == 1:
            in_specs.append(pl.BlockSpec((w.shape[0], tn), lambda g, t, c, first=first: (0, first + c)))
            copy_specs.append(pl.BlockSpec((w.shape[0], tn), lambda g, t, c: (0, c)))
            copy_shapes.append(jax.ShapeDtypeStruct((w.shape[0], n_c * tn), BF16))
        else:
            in_specs.append(pl.BlockSpec((tn, w.shape[1]), lambda g, t, c, first=first: (first + c, 0)))
            copy_specs.append(pl.BlockSpec((tn, w.shape[1]), lambda g, t, c: (c, 0)))
            copy_shapes.append(jax.ShapeDtypeStruct((n_c * tn, w.shape[1]), BF16))
    if all(w.dtype == BF16 for w, _, _ in weights):
        return in_specs, [], []
    return in_specs, copy_specs, copy_shapes


def _merge(x3, mod, g_norm1, attn, pooled, weights, gb, r, tn):
    n_g, n_t, d = _group_specs(x3, gb, r)
    tm = gb * r
    aw = attn.shape[1]
    pw = pooled.shape[1]
    n_c = d // tn
    w_specs, copy_specs, copy_shapes = _column_weight_specs(weights, tn, n_c)
    if copy_specs:
        assert n_g * n_t == 1, "weight copies are written by a single token tile"
    x_spec = pl.BlockSpec((gb, r, d), lambda g, t, c: (g, t, 0))
    mod_spec = lambda k: pl.BlockSpec((gb, 1, d), lambda g, t, c: (g, 0, k))
    out, *copies = pl.pallas_call(
        _merge_kernel,
        grid=(n_g, n_t, n_c),
        in_specs=[x_spec, mod_spec(0), mod_spec(1), mod_spec(2),
                  pl.BlockSpec((1, d), lambda g, t, c: (0, 0)),
                  pl.BlockSpec((tm, aw), lambda g, t, c: (g * n_t + t, 0)),
                  pl.BlockSpec((tm, pw), lambda g, t, c: (g * n_t + t, 0))] + w_specs,
        out_specs=[x_spec] + copy_specs,
        out_shape=[jax.ShapeDtypeStruct(x3.shape, F32)] + copy_shapes,
        scratch_shapes=[pltpu.VMEM((tm, d), BF16)],
        compiler_params=_params("arbitrary", "arbitrary", "arbitrary"),
        name="merge_out",
    )(x3, mod, mod, mod, g_norm1.reshape(1, d), attn, pooled, *[w for w, _, _ in weights])
    return out, copies


def _ffn_kernel(x_ref, sh_ref, sc_ref, gt_ref, g2_ref, wa_ref, wb_ref, wo_ref, o_ref, *rest):
    *copy_refs, h_scr = rest
    c = pl.program_id(2)

    @pl.when(c == 0)
    def _():
        _mod_norm_store(x_ref, g2_ref, sc_ref, sh_ref, h_scr)
        o_ref[...] = jnp.zeros_like(o_ref)

    wa, wb, wo = _bf16_tiles((wa_ref, wb_ref, wo_ref), copy_refs)
    h = h_scr[...]
    a = jnp.dot(h, wa, preferred_element_type=F32)
    b = jnp.dot(h, wb, preferred_element_type=F32)
    hid = (jax.nn.silu(a) * b).astype(BF16)
    o_ref[...] += jnp.dot(hid, wo, preferred_element_type=F32).reshape(o_ref.shape)

    @pl.when(c == pl.num_programs(2) - 1)
    def _():
        o_ref[...] = x_ref[...] + gt_ref[...] * o_ref[...]


def _ffn(x3, mod, g_norm2, weights, hidden, gb, r, tn):
    n_g, n_t, d = _group_specs(x3, gb, r)
    tm = gb * r
    n_c = hidden // tn
    w_specs, copy_specs, copy_shapes = _column_weight_specs(weights, tn, n_c)
    if copy_specs:
        assert n_g * n_t == 1, "weight copies are written by a single token tile"
    x_spec = pl.BlockSpec((gb, r, d), lambda g, t, c: (g, t, 0))
    mod_spec = lambda k: pl.BlockSpec((gb, 1, d), lambda g, t, c: (g, 0, k))
    out, *copies = pl.pallas_call(
        _ffn_kernel,
        grid=(n_g, n_t, n_c),
        in_specs=[x_spec, mod_spec(3), mod_spec(4), mod_spec(5),
                  pl.BlockSpec((1, d), lambda g, t, c: (0, 0))] + w_specs,
        out_specs=[x_spec] + copy_specs,
        out_shape=[jax.ShapeDtypeStruct(x3.shape, F32)] + copy_shapes,
        scratch_shapes=[pltpu.VMEM((tm, d), BF16)],
        compiler_params=_params("arbitrary", "arbitrary", "arbitrary"),
        name="ffn_swiglu",
    )(x3, mod, mod, mod, g_norm2.reshape(1, d), *[w for w, _, _ in weights])
    return out, copies


def _rope_tables(pos):
    half = HEAD_DIM // 2
    inv = ROPE_THETA ** (-jnp.arange(half, dtype=F32) / half)
    ang = pos.astype(F32)[:, None] * inv[None, :]
    cos, sin = jnp.cos(ang), jnp.sin(ang)
    return jnp.concatenate([cos, cos], axis=-1), jnp.concatenate([-sin, sin], axis=-1)


def _pick_rows(seq, target=512):
    r = min(seq, target)
    while seq % r:
        r //= 2
    return r


def kernel(x_prompt, x_sample, c_prompt, c_sample, cache_k, cache_v, state_pool, page_table, w_ada, b_ada, g_norm1, g_norm2, w_in, g_qnorm, g_knorm, w_pool, ls_pool, w_o_attn, w_o_pool, w_out, w_ffn_in, w_ffn_out):
    n_p, seq, d = x_prompt.shape
    n_s, dq, _ = x_sample.shape
    past = page_table.shape[1] * cache_k.shape[1]
    pool_w = state_pool.shape[2]
    n_state = state_pool.shape[1]

    w_qkvu_bf = w_in[:, :3 * ATTN_WIDTH + pool_w].astype(BF16)
    w_pool_bf = w_pool.astype(BF16)
    hidden = w_ffn_out.shape[0]

    n_c = n_p + n_s
    c_all = jnp.concatenate([c_prompt, c_sample], axis=0)
    c_all = jnp.pad(c_all, ((0, (-n_c) % 8), (0, 0)))
    mod = _ada(c_all, w_ada, b_ada)
    mod_p = mod[:n_p].reshape(n_p, 1, 6 * d)
    mod_s = mod[n_p:n_c].reshape(n_s, 1, 6 * d)

    cos_s, sin_s = _rope_tables(past + jnp.arange(dq))
    cos_s, sin_s = jnp.tile(cos_s, (n_s, 1)), jnp.tile(sin_s, (n_s, 1))
    q_s, k_s, v_s, u_s = _qkvu(x_sample, mod_s, g_norm1, w_qkvu_bf, g_qnorm, g_knorm, cos_s, sin_s, n_s, dq)
    attn_s = _moba_sample(q_s, k_s, v_s, cache_k, cache_v, page_table)
    halo_s = jnp.pad(state_pool, ((0, 0), (POOL_HALO - n_state, 0), (0, 0)))
    pooled_s = _pool(u_s, halo_s, w_pool_bf, ls_pool, n_s, dq, n_s, dq, past)
    gate0 = (w_in.shape[1] - 2 * d) // _TN_SAMPLE
    x1_s, merge_bf = _merge(
        x_sample, mod_s, g_norm1, attn_s, pooled_s,
        ((w_in, 1, gate0), (w_in, 1, gate0 + d // _TN_SAMPLE), (w_o_attn, 1, 0), (w_o_pool, 1, 0),
         (w_out, 0, 0)), n_s, dq, _TN_SAMPLE)
    y_s, ffn_bf = _ffn(
        x1_s, mod_s, g_norm2,
        ((w_ffn_in, 1, 0), (w_ffn_in, 1, hidden // _TN_SAMPLE), (w_ffn_out, 0, 0)),
        hidden, n_s, dq, _TN_SAMPLE)

    r_p = _pick_rows(seq)
    cos_p, sin_p = _rope_tables(jnp.arange(seq))
    q_p, k_p, v_p, u_p = _qkvu(x_prompt, mod_p, g_norm1, w_qkvu_bf, g_qnorm, g_knorm, cos_p, sin_p, 1, r_p)
    attn_p = _moba_prompt(q_p, k_p, v_p, n_p, seq)
    pooled_p = _pool(u_p, jnp.zeros((n_p, POOL_HALO, pool_w), F32), w_pool_bf, ls_pool,
                     n_p, seq, 1, r_p, 0)
    x1_p, _ = _merge(x_prompt, mod_p, g_norm1, attn_p, pooled_p,
                     tuple((w, axis, 0) for w, axis in zip(merge_bf, (1, 1, 1, 1, 0))), 1, r_p, _TN_PROMPT)
    y_p, _ = _ffn(x1_p, mod_p, g_norm2, tuple((w, axis, 0) for w, axis in zip(ffn_bf, (1, 1, 0))),
                  hidden, 1, r_p, _TN_PROMPT)

    k_prompt = k_p.reshape(n_p, seq, N_HEADS, HEAD_DIM)
    v_prompt = v_p.reshape(n_p, seq, N_HEADS, HEAD_DIM)
    pool_prompt = u_p.reshape(n_p, seq, pool_w)[:, seq - n_state:]
    k_sample = k_s.reshape(n_s, dq, N_HEADS, HEAD_DIM)
    v_sample = v_s.reshape(n_s, dq, N_HEADS, HEAD_DIM)
    pool_sample = jnp.concatenate([state_pool, u_s.reshape(n_s, dq, pool_w)], axis=1)[:, -n_state:]
    return (y_p, y_s, k_prompt, v_prompt, pool_prompt, k_sample, v_sample, pool_sample)
```

```python
import functools

import jax
import jax.numpy as jnp
from jax import lax
from jax.experimental import pallas as pl
from jax.experimental.pallas import tpu as pltpu

N_HEADS = 8
HEAD_DIM = 128
ATTN_WIDTH = N_HEADS * HEAD_DIM
MOBA_BLOCK = 256
MOBA_TOPK = 3
ROPE_THETA = 10000.0
POOL_WINDOWS = (2, 4, 8, 16)
POOL_HALO = 16
RMS_EPS = 1e-6
_TILE_GROUP = 4
_MASKED_MAX = -1e30
_TN_PROMPT = 512
_TN_SAMPLE = 256

F32 = jnp.float32
BF16 = jnp.bfloat16
NEG_INF = float("-inf")
_LOG2_E = 1.4426950408889634

_VMEM_LIMIT = 52 * 1024 * 1024


def _params(*sem):
    return pltpu.CompilerParams(dimension_semantics=sem, vmem_limit_bytes=_VMEM_LIMIT)


_NORM_ROWS = 16


def _mod_norm_store(x_ref, g_ref, sc_ref, sh_ref, h_scr):
    gb, r, d = x_ref.shape
    g = g_ref[...]
    if r >= _NORM_ROWS:
        per_group = r // _NORM_ROWS
        chunks = [(gi, 1, c * _NORM_ROWS, _NORM_ROWS) for gi in range(gb) for c in range(per_group)]
    else:
        n_g = _NORM_ROWS // r
        chunks = [(gi, n_g, 0, r) for gi in range(0, gb, n_g)]
    gains = {}
    for gi, n_g, r0, nr in chunks:
        if gi not in gains:
            gains[gi] = g * (1.0 + sc_ref[gi:gi + n_g])
        x = x_ref[gi:gi + n_g, r0:r0 + nr, :]
        ms = jnp.mean(x * x, axis=-1, keepdims=True)
        h = x * lax.rsqrt(ms + RMS_EPS) * gains[gi] + sh_ref[gi:gi + n_g]
        row0 = gi * r + r0
        h_scr[row0:row0 + n_g * nr, :] = h.reshape(n_g * nr, d).astype(BF16)


def _ada_kernel(c_ref, w_ref, b_ref, o_ref):
    o_ref[...] = jnp.dot(c_ref[...].astype(BF16), w_ref[...].astype(BF16),
                         preferred_element_type=F32) + b_ref[...]


def _ada(c_all, w_ada, b_ada, tn=1024):
    m, d = c_all.shape
    n = w_ada.shape[1]
    return pl.pallas_call(
        _ada_kernel,
        grid=(n // tn,),
        in_specs=[pl.BlockSpec((m, d), lambda j: (0, 0)),
                  pl.BlockSpec((d, tn), lambda j: (0, j)),
                  pl.BlockSpec((1, tn), lambda j: (0, j))],
        out_specs=pl.BlockSpec((m, tn), lambda j: (0, j)),
        out_shape=jax.ShapeDtypeStruct((m, n), F32),
        compiler_params=_params("arbitrary"),
        name="ada_mod",
    )(c_all, w_ada, b_ada.reshape(1, n))


def _head_norm_rope(th, g, cos, sin_signed):
    ms = jnp.mean(th * th, axis=-1, keepdims=True)
    y = th * lax.rsqrt(ms + RMS_EPS) * g
    return y * cos + pltpu.roll(y, HEAD_DIM // 2, 1) * sin_signed


_PROJ_CHUNK = 2 * HEAD_DIM


def _qkvu_kernel(x_ref, sh_ref, sc_ref, g1_ref, w_ref, gq_ref, gk_ref, cos_ref, sin_ref,
                 q_ref, k_ref, v_ref, u_ref, h_scr):
    _mod_norm_store(x_ref, g1_ref, sc_ref, sh_ref, h_scr)

    width = q_ref.shape[1]
    outs = ((q_ref, gq_ref), (k_ref, gk_ref), (v_ref, None), (u_ref, None))
    for i, (o_ref, g_ref) in enumerate(outs):
        for c in range(0, width, _PROJ_CHUNK):
            col = i * width + c
            acc = jnp.dot(h_scr[...], w_ref[:, col:col + _PROJ_CHUNK], preferred_element_type=F32)
            if g_ref is None:
                o_ref[:, c:c + _PROJ_CHUNK] = acc
            else:
                for hh in range(0, _PROJ_CHUNK, HEAD_DIM):
                    o_ref[:, c + hh:c + hh + HEAD_DIM] = _head_norm_rope(
                        acc[:, hh:hh + HEAD_DIM], g_ref[...], cos_ref[...], sin_ref[...])


def _group_specs(x3, gb, r):
    g_total, s, d = x3.shape
    n_g, n_t = g_total // gb, s // r
    return n_g, n_t, d


def _qkvu(x3, mod, g_norm1, w_in_bf, g_q, g_k, cos, sin_signed, gb, r):
    n_g, n_t, d = _group_specs(x3, gb, r)
    tm = gb * r
    tokens = x3.shape[0] * x3.shape[1]
    wq = ATTN_WIDTH
    x_spec = pl.BlockSpec((gb, r, d), lambda g, t: (g, t, 0))
    mod_spec = lambda k: pl.BlockSpec((gb, 1, d), lambda g, t: (g, 0, k))
    row_spec = pl.BlockSpec((1, d), lambda g, t: (0, 0))
    head_spec = pl.BlockSpec((1, HEAD_DIM), lambda g, t: (0, 0))
    tab_spec = pl.BlockSpec((tm, HEAD_DIM), lambda g, t: (t, 0))
    out_spec = pl.BlockSpec((tm, wq), lambda g, t: (g * n_t + t, 0))
    out_sds = jax.ShapeDtypeStruct((tokens, wq), F32)
    w_spec = pl.BlockSpec((d, 4 * wq), lambda g, t: (0, 0), pipeline_mode=pl.Buffered(1))
    return pl.pallas_call(
        _qkvu_kernel,
        grid=(n_g, n_t),
        in_specs=[x_spec, mod_spec(0), mod_spec(1), row_spec, w_spec,
                  head_spec, head_spec, tab_spec, tab_spec],
        out_specs=[out_spec] * 4,
        out_shape=[out_sds] * 4,
        scratch_shapes=[pltpu.VMEM((tm, d), BF16)],
        compiler_params=_params("arbitrary", "arbitrary"),
        name="qkvu_proj",
    )(x3, mod, mod, g_norm1.reshape(1, d), w_in_bf, g_q.reshape(1, HEAD_DIM),
      g_k.reshape(1, HEAD_DIM), cos, sin_signed)


def _topk_select(gate, valid, axis):
    n = gate.shape[axis]
    g = jnp.where(valid, gate, NEG_INF)
    idx = lax.broadcasted_iota(jnp.int32, gate.shape, axis)
    rank = jnp.zeros(gate.shape, jnp.int32)
    for m in range(n):
        gm = g[m:m + 1, :] if axis == 0 else g[:, m:m + 1]
        beats = (gm > g) | ((gm == g) & (m < idx))
        rank = rank + beats.astype(jnp.int32)
    return valid & (rank < MOBA_TOPK) & (jnp.abs(g) < float("inf"))


def _moba_prompt_kernel(q_ref, k_ref, v_ref, o_ref, kb_scr, vt_scr, kmean_scr, bias_scr, tri_scr, s_scr,
                        *, nb):
    blk = MOBA_BLOCK

    def prepare():
        for n in range(nb):
            kn = k_ref[n * blk:(n + 1) * blk, :]
            kb_scr[n * blk:(n + 1) * blk, :] = kn.astype(BF16)
            kmean_scr[n:n + 1, :] = jnp.mean(kn, axis=0, keepdims=True)
            vt_scr[n] = v_ref[n * blk:(n + 1) * blk, :].T.astype(BF16)
        kb_scr[nb * blk:, :] = jnp.zeros((kb_scr.shape[0] - nb * blk, HEAD_DIM), BF16)
        kmean = kmean_scr[...]
        for qb_i in range(nb):
            gate = lax.dot_general(kmean, q_ref[qb_i * blk:(qb_i + 1) * blk, :],
                                   (((1,), (1,)), ((), ())),
                                   precision=lax.Precision.HIGHEST, preferred_element_type=F32)
            n_idx = lax.broadcasted_iota(jnp.int32, gate.shape, 0)
            sel = _topk_select(gate, n_idx < qb_i, axis=0)
            bias_scr[qb_i, 0:nb, :] = jnp.where(sel | (n_idx == qb_i), 0.0, NEG_INF)
            bias_scr[qb_i, nb:, :] = jnp.full((bias_scr.shape[1] - nb, blk), NEG_INF, F32)
        kpos = lax.broadcasted_iota(jnp.int32, (blk, blk), 0)
        qpos = lax.broadcasted_iota(jnp.int32, (blk, blk), 1)
        tri_scr[0] = jnp.zeros((blk, blk), F32)
        tri_scr[1] = jnp.where(kpos <= qpos, 0.0, NEG_INF)

    prepare()

    exp2_scale = HEAD_DIM ** -0.5 * _LOG2_E
    width = _TILE_GROUP

    def fold(x, op):
        return op(x.reshape(blk // 8, 8, blk), axis=0)

    def queries(qi):
        return q_ref[pl.ds(pl.multiple_of(qi * blk, blk), blk), :].astype(BF16)

    def score_group(qi, qb, g, slot):
        j0 = g * width
        keys = kb_scr[pl.ds(pl.multiple_of(j0 * blk, blk), width * blk), :]
        s_all = lax.dot_general(keys, qb, (((1,), (1,)), ((), ())), preferred_element_type=F32)
        m8 = jnp.full((8, blk), NEG_INF, F32)
        for t in range(width):
            j = j0 + t
            s = (s_all[t * blk:(t + 1) * blk, :] + bias_scr[qi, pl.ds(j, 1), :]
                 + tri_scr[jnp.asarray(j == qi, jnp.int32)])
            s_scr[slot, t] = s
            m8 = jnp.maximum(m8, fold(s, jnp.max))
        return m8

    def absorb_group(g, slot, m_run, l8, acc, m8):
        m_new = jnp.maximum(m_run, jnp.max(m8, axis=0, keepdims=True))
        alpha = jnp.exp2((m_run - m_new) * exp2_scale)
        l8 = l8 * alpha
        acc = acc * alpha
        for t in range(width):
            p = jnp.exp2((s_scr[slot, t] - m_new) * exp2_scale)
            l8 = l8 + fold(p, jnp.sum)
            acc = acc + jnp.dot(vt_scr[jnp.minimum(g * width + t, nb - 1)], p.astype(BF16),
                                preferred_element_type=F32)
        return m_new, l8, acc

    def one_query_block(qi, state):
        m8_first, first_slot = state
        qb = queries(qi)
        n_groups = (qi + width) // width

        def body(g, carry):
            m_run, l8, acc, m8 = carry
            m_run, l8, acc = absorb_group(g, (first_slot + g) % 2, m_run, l8, acc, m8)
            return m_run, l8, acc, score_group(qi, qb, g + 1, (first_slot + g + 1) % 2)

        carry = (jnp.full((1, blk), _MASKED_MAX, F32), jnp.zeros((8, blk), F32),
                 jnp.zeros((HEAD_DIM, blk), F32), m8_first)
        m_run, l8, acc, m8 = lax.fori_loop(0, n_groups - 1, body, carry)
        last_slot = (first_slot + n_groups - 1) % 2
        _, l8, acc = absorb_group(n_groups - 1, last_slot, m_run, l8, acc, m8)
        nxt = jnp.minimum(qi + 1, nb - 1)
        m8_next = score_group(nxt, queries(nxt), 0, 1 - last_slot)
        l = jnp.sum(l8, axis=0, keepdims=True)
        o_ref[pl.ds(pl.multiple_of(qi * blk, blk), blk), :] = (acc / l).T.astype(o_ref.dtype)
        return m8_next, 1 - last_slot

    lax.fori_loop(0, nb, one_query_block, (score_group(0, queries(0), 0, 0), jnp.int32(0)))


def _moba_prompt(q, k, v, n_seq, seq):
    nb = seq // MOBA_BLOCK
    nb_pad = -(-nb // _TILE_GROUP) * _TILE_GROUP + _TILE_GROUP
    blk = MOBA_BLOCK
    head_spec = pl.BlockSpec((seq, HEAD_DIM), lambda b, h: (b, h))
    return pl.pallas_call(
        functools.partial(_moba_prompt_kernel, nb=nb),
        grid=(n_seq, N_HEADS),
        in_specs=[head_spec, head_spec, head_spec],
        out_specs=head_spec,
        out_shape=jax.ShapeDtypeStruct(q.shape, BF16),
        scratch_shapes=[pltpu.VMEM((nb_pad * blk, HEAD_DIM), BF16),
                        pltpu.VMEM((nb, HEAD_DIM, blk), BF16),
                        pltpu.VMEM((nb, HEAD_DIM), F32),
                        pltpu.VMEM((nb, nb_pad, blk), F32),
                        pltpu.VMEM((2, blk, blk), F32),
                        pltpu.VMEM((2, _TILE_GROUP, blk, blk), F32)],
        compiler_params=_params("arbitrary", "arbitrary"),
        name="moba_prompt",
    )(q, k, v)


def _page_copy_groups(cache_hbm, pt_ref, buf, sem, slot, b, first_page, hp, n_pages, ppb, wanted):
    groups = []
    for p0 in range(0, hp, ppb):
        pgs = [pt_ref[b * n_pages + first_page + p0 + i] for i in range(ppb)]
        for h in range(N_HEADS):
            pred = None if wanted is None else wanted(b, (first_page + p0) // ppb, h)
            groups.append((pred, [pltpu.make_async_copy(cache_hbm.at[pgs[i], :, h, :],
                                                        buf.at[slot, p0 + i, h], sem.at[slot])
                                  for i in range(ppb)]))
    return groups


def _for_each_copy(groups, method):
    for pred, copies in groups:
        def run(copies=copies):
            for c in copies:
                getattr(c, method)()
        if pred is None:
            run()
        else:
            pl.when(pred)(run)


def _paged_slots(pt_ref, cache_hbm, buf, sem, hp, n_pages, ppb, compute, wanted=None):
    n_slots = buf.shape[0]
    b, step = pl.program_id(0), pl.program_id(1)
    n_b, n_steps = pl.num_programs(0), pl.num_programs(1)
    is_first = jnp.logical_and(b == 0, step == 0)
    is_last = jnp.logical_and(b == n_b - 1, step == n_steps - 1)
    wrap = step == n_steps - 1
    nxt_b = jnp.where(wrap, b + 1, b)
    nxt_step = jnp.where(wrap, 0, step + 1)

    def groups(bb, ss, slot):
        return _page_copy_groups(cache_hbm, pt_ref, buf, sem, slot, bb, (n_slots * ss + slot) * hp,
                                 hp, n_pages, ppb, wanted)

    @pl.when(is_first)
    def _():
        for slot in range(n_slots):
            _for_each_copy(groups(b, step, slot), "start")

    for slot in range(n_slots):
        _for_each_copy(groups(b, step, slot), "wait")
        compute(slot, (n_slots * step + slot) * hp)

        @pl.when(jnp.logical_not(is_last))
        def _():
            _for_each_copy(groups(nxt_b, nxt_step, slot), "start")


def _moba_sample_k_kernel(pt_ref, q_ref, knew_ref, vnew_ref, cache_hbm, p_ref, oown_ref, linv_ref,
                          used_ref, buf, sem, s_scr, kpart_scr, qbd_scr, *, hp, page, n_blocks):
    step = pl.program_id(1)
    n_steps = pl.num_programs(1)
    dq = q_ref.shape[0]
    rows = N_HEADS * dq
    scale = HEAD_DIM ** -0.5
    pages_per_block = MOBA_BLOCK // page
    n_pages = n_blocks * pages_per_block

    qh = [q_ref[:, h * HEAD_DIM:(h + 1) * HEAD_DIM] for h in range(N_HEADS)]
    qh_bf = [x.astype(BF16) for x in qh]

    @pl.when(step == 0)
    def _():
        kpart_scr[...] = jnp.zeros_like(kpart_scr)
        q_rep = jnp.concatenate([q_ref[...]] * N_HEADS
                                + [jnp.zeros((qbd_scr.shape[1] - rows, ATTN_WIDTH), F32)], axis=0)
        row_head = lax.broadcasted_iota(jnp.int32, q_rep.shape, 0) // dq
        lane_head = lax.broadcasted_iota(jnp.int32, q_rep.shape, 1) // HEAD_DIM
        qbd_scr[...] = jnp.where(row_head == lane_head, q_rep, 0.0).T.astype(BF16)

    def compute(slot, first_page):
        heads = [[buf[slot, p, h] for h in range(N_HEADS)] for p in range(hp)]
        for p in range(hp):
            blk = (first_page + p) // pages_per_block
            for h in range(N_HEADS):
                kpart_scr[blk, h] += jnp.sum(heads[p][h].reshape(page // 8, 8, HEAD_DIM), axis=0)
        for p0 in range(0, hp, hp // 2):
            keys = jnp.concatenate([jnp.concatenate(hs, axis=1) for hs in heads[p0:p0 + hp // 2]],
                                   axis=0)
            s_t = jnp.dot(keys.astype(BF16), qbd_scr[...], preferred_element_type=F32) * scale
            for i in range(hp // 2):
                s_scr[first_page + p0 + i] = s_t[i * page:(i + 1) * page, :].T[:rows, :]

    _paged_slots(pt_ref, cache_hbm, buf, sem, hp, n_pages, pages_per_block, compute)

    @pl.when(step == n_steps - 1)
    def _():
        gates = []
        for h in range(N_HEADS):
            kmean_h = jnp.sum(kpart_scr[:, h], axis=1) * (1.0 / MOBA_BLOCK)
            gates.append(lax.dot_general(qh[h], kmean_h, (((1,), (1,)), ((), ())),
                                         precision=lax.Precision.HIGHEST,
                                         preferred_element_type=F32))
        gate = jnp.concatenate(gates, axis=0)
        sel = _topk_select(gate, jnp.ones(gate.shape, jnp.bool_), axis=1)
        bias = jnp.where(sel, 0.0, NEG_INF)
        used_ref[0] = jnp.concatenate(
            [jnp.max(sel[h * dq:(h + 1) * dq, :].astype(jnp.int32), axis=0, keepdims=True)
             for h in range(N_HEADS)], axis=0)

        s_own = []
        for h in range(N_HEADS):
            kn = knew_ref[:, h * HEAD_DIM:(h + 1) * HEAD_DIM].astype(BF16)
            s_own.append(lax.dot_general(qh_bf[h], kn, (((1,), (1,)), ((), ())),
                                         preferred_element_type=F32) * scale)
        s_own = jnp.concatenate(s_own, axis=0)
        row = lax.broadcasted_iota(jnp.int32, s_own.shape, 0) % dq
        col = lax.broadcasted_iota(jnp.int32, s_own.shape, 1)
        s_own = jnp.where(col <= row, s_own, NEG_INF)

        m_vec = jnp.full((rows, page), NEG_INF, F32)
        for g in range(n_pages):
            n = g // pages_per_block
            m_vec = jnp.maximum(m_vec, s_scr[g] + bias[:, n:n + 1])
        m = jnp.maximum(jnp.max(m_vec, axis=1, keepdims=True), jnp.max(s_own, axis=1, keepdims=True))
        p_own = jnp.exp(s_own - m)
        l_vec = jnp.zeros((rows, page), F32)
        for g in range(n_pages):
            n = g // pages_per_block
            pb = jnp.exp(s_scr[g] + bias[:, n:n + 1] - m)
            l_vec = l_vec + pb
            p_ref[0, :, g * page:(g + 1) * page] = pb
        l = jnp.sum(l_vec, axis=1, keepdims=True) + jnp.sum(p_own, axis=1, keepdims=True)
        linv_ref[0] = jnp.broadcast_to(1.0 / l, (rows, HEAD_DIM))
        for h in range(N_HEADS):
            vn = vnew_ref[:, h * HEAD_DIM:(h + 1) * HEAD_DIM].astype(BF16)
            oown_ref[0, h * dq:(h + 1) * dq, :] = jnp.dot(
                p_own[h * dq:(h + 1) * dq, :].astype(BF16), vn, preferred_element_type=F32)


def _moba_sample_v_kernel(pt_ref, used_ref, p_ref, oown_ref, linv_ref, cache_hbm, o_ref, buf, sem,
                          acc_scr, *, hp, page, n_pages):
    step = pl.program_id(1)
    n_steps = pl.num_programs(1)
    dq = o_ref.shape[0]
    ppb = MOBA_BLOCK // page
    n_blocks = n_pages // ppb

    @pl.when(jnp.logical_and(pl.program_id(0) == 0, step == 0))
    def _():
        buf[...] = jnp.zeros_like(buf)

    def wanted(b, blk, h):
        return used_ref[(b * N_HEADS + h) * n_blocks + blk] != 0

    @pl.when(step == 0)
    def _():
        acc_scr[...] = oown_ref[0]

    def compute(slot, first_page):
        for h in range(N_HEADS):
            tot = jnp.zeros((dq, HEAD_DIM), F32)
            for p in range(hp):
                col0 = (slot * hp + p) * page
                ph = p_ref[0, h * dq:(h + 1) * dq, col0:col0 + page].astype(BF16)
                tot = tot + jnp.dot(ph, buf[slot, p, h].astype(BF16), preferred_element_type=F32)
            acc_scr[h * dq:(h + 1) * dq, :] += tot

    _paged_slots(pt_ref, cache_hbm, buf, sem, hp, n_pages, ppb, compute, wanted)

    @pl.when(step == n_steps - 1)
    def _():
        for h in range(N_HEADS):
            rows_h = slice(h * dq, (h + 1) * dq)
            o_ref[:, h * HEAD_DIM:(h + 1) * HEAD_DIM] = acc_scr[rows_h, :] * linv_ref[0, rows_h, :]


def _moba_sample(q, k_new, v_new, cache_k, cache_v, page_table, hp=4, n_slots=4):
    dec_b, n_pages = page_table.shape
    page = cache_k.shape[1]
    dq = q.shape[0] // dec_b
    past = n_pages * page
    n_blocks = past // MOBA_BLOCK
    while n_pages % (n_slots * hp):
        n_slots //= 2
    n_steps = n_pages // (n_slots * hp)
    rows = N_HEADS * dq
    pt = page_table.reshape(-1)

    new_spec = pl.BlockSpec((dq, ATTN_WIDTH), lambda b, s, pt_ref: (b, 0))
    any_spec = pl.BlockSpec(memory_space=pl.ANY)
    stat_spec = pl.BlockSpec((1, rows, HEAD_DIM), lambda b, s, pt_ref: (b, 0, 0))
    stat_sds = jax.ShapeDtypeStruct((dec_b, rows, HEAD_DIM), F32)
    assert hp % (MOBA_BLOCK // page) == 0, "a page half must hold whole key blocks"
    page_buf = pltpu.VMEM((n_slots, hp, N_HEADS, page, HEAD_DIM), F32)
    probs, o_own, l_inv, used = pl.pallas_call(
        functools.partial(_moba_sample_k_kernel, hp=hp, page=page, n_blocks=n_blocks),
        grid_spec=pltpu.PrefetchScalarGridSpec(
            num_scalar_prefetch=1,
            grid=(dec_b, n_steps),
            in_specs=[new_spec, new_spec, new_spec, any_spec],
            out_specs=[pl.BlockSpec((1, rows, past), lambda b, s, pt_ref: (b, 0, 0)),
                       stat_spec, stat_spec,
                       pl.BlockSpec((1, N_HEADS, n_blocks), lambda b, s, pt_ref: (b, 0, 0))],
            scratch_shapes=[page_buf, pltpu.SemaphoreType.DMA((n_slots,)),
                            pltpu.VMEM((n_pages, rows, page), F32),
                            pltpu.VMEM((n_blocks, N_HEADS, 8, HEAD_DIM), F32),
                            pltpu.VMEM((ATTN_WIDTH, max(rows, HEAD_DIM)), BF16)]),
        out_shape=[jax.ShapeDtypeStruct((dec_b, rows, past), F32), stat_sds, stat_sds,
                   jax.ShapeDtypeStruct((dec_b, N_HEADS, n_blocks), jnp.int32)],
        compiler_params=_params("arbitrary", "arbitrary"),
        name="moba_sample_k",
    )(pt, q, k_new, v_new, cache_k)

    return pl.pallas_call(
        functools.partial(_moba_sample_v_kernel, hp=hp, page=page, n_pages=n_pages),
        grid_spec=pltpu.PrefetchScalarGridSpec(
            num_scalar_prefetch=2,
            grid=(dec_b, n_steps),
            in_specs=[pl.BlockSpec((1, rows, n_slots * hp * page), lambda b, s, *_: (b, 0, s)),
                      pl.BlockSpec((1, rows, HEAD_DIM), lambda b, s, *_: (b, 0, 0)),
                      pl.BlockSpec((1, rows, HEAD_DIM), lambda b, s, *_: (b, 0, 0)),
                      any_spec],
            out_specs=pl.BlockSpec((dq, ATTN_WIDTH), lambda b, s, *_: (b, 0)),
            scratch_shapes=[page_buf, pltpu.SemaphoreType.DMA((n_slots,)),
                            pltpu.VMEM((rows, HEAD_DIM), F32)]),
        out_shape=jax.ShapeDtypeStruct(q.shape, F32),
        compiler_params=_params("arbitrary", "arbitrary"),
        name="moba_sample_v",
    )(pt, used.reshape(-1), probs, o_own, l_inv, cache_v)


def _pool_kernel(u_ref, halo_ref, w_ref, ls_ref, o_ref, ext_scr, *, pos0):
    t = pl.program_id(1)
    gb, r, width = u_ref.shape
    grp = width // len(POOL_WINDOWS)

    @pl.when(t == 0)
    def _():
        ext_scr[:, 0:POOL_HALO, :] = halo_ref[...]

    @pl.when(t > 0)
    def _():
        ext_scr[:, 0:POOL_HALO, :] = ext_scr[:, r:r + POOL_HALO, :]

    ext_scr[:, POOL_HALO:POOL_HALO + r, :] = u_ref[...]

    pos = pos0 + t * r + lax.broadcasted_iota(jnp.int32, (1, r, grp), 1)
    for g, w in enumerate(POOL_WINDOWS):
        cols = slice(g * grp, (g + 1) * grp)
        tot = ext_scr[:, POOL_HALO:POOL_HALO + r, cols]
        for d in range(1, w):
            tot = tot + ext_scr[:, POOL_HALO - d:POOL_HALO - d + r, cols]
        cnt = jnp.minimum(w, pos + 1).astype(F32)
        y = tot / cnt - u_ref[:, :, cols]
        y = jnp.dot(y.reshape(gb * r, grp).astype(BF16), w_ref[g], preferred_element_type=F32)
        o_ref[:, cols] = (y * ls_ref[:, cols]).astype(o_ref.dtype)


def _pool(u, halo, w_pool_bf, ls_pool, n_seq, seq, gb, r, pos0):
    width = u.shape[1]
    n_g, n_t = n_seq // gb, seq // r
    u3 = u.reshape(n_seq, seq, width)
    return pl.pallas_call(
        functools.partial(_pool_kernel, pos0=pos0),
        grid=(n_g, n_t),
        in_specs=[pl.BlockSpec((gb, r, width), lambda g, t: (g, t, 0)),
                  pl.BlockSpec((gb, POOL_HALO, width), lambda g, t: (g, 0, 0)),
                  pl.BlockSpec(w_pool_bf.shape, lambda g, t: (0, 0, 0)),
                  pl.BlockSpec((1, width), lambda g, t: (0, 0))],
        out_specs=pl.BlockSpec((gb * r, width), lambda g, t: (g * n_t + t, 0)),
        out_shape=jax.ShapeDtypeStruct(u.shape, BF16),
        scratch_shapes=[pltpu.VMEM((gb, r + POOL_HALO, width), F32)],
        compiler_params=_params("arbitrary", "arbitrary"),
        name="pool_mixer",
    )(u3, halo, w_pool_bf, ls_pool.reshape(1, width))


def _bf16_tiles(w_refs, copy_refs):
    tiles = []
    for i, w_ref in enumerate(w_refs):
        w = w_ref[...]
        if w.dtype != BF16:
            w = w.astype(BF16)
            copy_refs[i][...] = w
        tiles.append(w)
    return tiles


def _merge_kernel(x_ref, sh_ref, sc_ref, gt_ref, g1_ref, attn_ref, pool_ref,
                  wga_ref, wgb_ref, woa_ref, wob_ref, wout_ref, o_ref, *rest):
    *copy_refs, h_scr = rest
    c = pl.program_id(2)

    @pl.when(c == 0)
    def _():
        _mod_norm_store(x_ref, g1_ref, sc_ref, sh_ref, h_scr)
        o_ref[...] = jnp.zeros_like(o_ref)

    wga, wgb, woa, wob, wout = _bf16_tiles((wga_ref, wgb_ref, woa_ref, wob_ref, wout_ref), copy_refs)
    h = h_scr[...]
    g_a = jax.nn.sigmoid(jnp.dot(h, wga, preferred_element_type=F32))
    g_b = jax.nn.sigmoid(jnp.dot(h, wgb, preferred_element_type=F32))
    y_a = jnp.dot(attn_ref[...].astype(BF16), woa, preferred_element_type=F32)
    y_b = jnp.dot(pool_ref[...].astype(BF16), wob, preferred_element_type=F32)
    mix_in = (g_a * y_a + g_b * y_b).astype(BF16)
    o_ref[...] += jnp.dot(mix_in, wout, preferred_element_type=F32).reshape(o_ref.shape)

    @pl.when(c == pl.num_programs(2) - 1)
    def _():
        o_ref[...] = x_ref[...] + gt_ref[...] * o_ref[...]


def _column_weight_specs(weights, tn, n_c):
    in_specs, copy_specs, copy_shapes = [], [], []
    for w, axis, first in weights:
        if axis == 1:
            in_specs.append(pl.BlockSpec((w.shape[0], tn), lambda g, t, c, first=first: (0, first + c)))
            copy_specs.append(pl.BlockSpec((w.shape[0], tn), lambda g, t, c: (0, c)))
            copy_shapes.append(jax.ShapeDtypeStruct((w.shape[0], n_c * tn), BF16))
        else:
            in_specs.append(pl.BlockSpec((tn, w.shape[1]), lambda g, t, c, first=first: (first + c, 0)))
            copy_specs.append(pl.BlockSpec((tn, w.shape[1]), lambda g, t, c: (c, 0)))
            copy_shapes.append(jax.ShapeDtypeStruct((n_c * tn, w.shape[1]), BF16))
    if all(w.dtype == BF16 for w, _, _ in weights):
        return in_specs, [], []
    return in_specs, copy_specs, copy_shapes


def _merge(x3, mod, g_norm1, attn, pooled, weights, gb, r, tn):
    n_g, n_t, d = _group_specs(x3, gb, r)
    tm = gb * r
    aw = attn.shape[1]
    pw = pooled.shape[1]
    n_c = d // tn
    w_specs, copy_specs, copy_shapes = _column_weight_specs(weights, tn, n_c)
    if copy_specs:
        assert n_g * n_t == 1, "weight copies are written by a single token tile"
    x_spec = pl.BlockSpec((gb, r, d), lambda g, t, c: (g, t, 0))
    mod_spec = lambda k: pl.BlockSpec((gb, 1, d), lambda g, t, c: (g, 0, k))
    out, *copies = pl.pallas_call(
        _merge_kernel,
        grid=(n_g, n_t, n_c),
        in_specs=[x_spec, mod_spec(0), mod_spec(1), mod_spec(2),
                  pl.BlockSpec((1, d), lambda g, t, c: (0, 0)),
                  pl.BlockSpec((tm, aw), lambda g, t, c: (g * n_t + t, 0)),
                  pl.BlockSpec((tm, pw), lambda g, t, c: (g * n_t + t, 0))] + w_specs,
        out_specs=[x_spec] + copy_specs,
        out_shape=[jax.ShapeDtypeStruct(x3.shape, F32)] + copy_shapes,
        scratch_shapes=[pltpu.VMEM((tm, d), BF16)],
        compiler_params=_params("arbitrary", "arbitrary", "arbitrary"),
        name="merge_out",
    )(x3, mod, mod, mod, g_norm1.reshape(1, d), attn, pooled, *[w for w, _, _ in weights])
    return out, copies


def _ffn_kernel(x_ref, sh_ref, sc_ref, gt_ref, g2_ref, wa_ref, wb_ref, wo_ref, o_ref, *rest):
    *copy_refs, h_scr = rest
    c = pl.program_id(2)

    @pl.when(c == 0)
    def _():
        _mod_norm_store(x_ref, g2_ref, sc_ref, sh_ref, h_scr)
        o_ref[...] = jnp.zeros_like(o_ref)

    wa, wb, wo = _bf16_tiles((wa_ref, wb_ref, wo_ref), copy_refs)
    h = h_scr[...]
    a = jnp.dot(h, wa, preferred_element_type=F32)
    b = jnp.dot(h, wb, preferred_element_type=F32)
    hid = (jax.nn.silu(a) * b).astype(BF16)
    o_ref[...] += jnp.dot(hid, wo, preferred_element_type=F32).reshape(o_ref.shape)

    @pl.when(c == pl.num_programs(2) - 1)
    def _():
        o_ref[...] = x_ref[...] + gt_ref[...] * o_ref[...]


def _ffn(x3, mod, g_norm2, weights, hidden, gb, r, tn):
    n_g, n_t, d = _group_specs(x3, gb, r)
    tm = gb * r
    n_c = hidden // tn
    w_specs, copy_specs, copy_shapes = _column_weight_specs(weights, tn, n_c)
    if copy_specs:
        assert n_g * n_t == 1, "weight copies are written by a single token tile"
    x_spec = pl.BlockSpec((gb, r, d), lambda g, t, c: (g, t, 0))
    mod_spec = lambda k: pl.BlockSpec((gb, 1, d), lambda g, t, c: (g, 0, k))
    out, *copies = pl.pallas_call(
        _ffn_kernel,
        grid=(n_g, n_t, n_c),
        in_specs=[x_spec, mod_spec(3), mod_spec(4), mod_spec(5),
                  pl.BlockSpec((1, d), lambda g, t, c: (0, 0))] + w_specs,
        out_specs=[x_spec] + copy_specs,
        out_shape=[jax.ShapeDtypeStruct(x3.shape, F32)] + copy_shapes,
        scratch_shapes=[pltpu.VMEM((tm, d), BF16)],
        compiler_params=_params("arbitrary", "arbitrary", "arbitrary"),
        name="ffn_swiglu",
    )(x3, mod, mod, mod, g_norm2.reshape(1, d), *[w for w, _, _ in weights])
    return out, copies


def _rope_tables(pos):
    half = HEAD_DIM // 2
    inv = ROPE_THETA ** (-jnp.arange(half, dtype=F32) / half)
    ang = pos.astype(F32)[:, None] * inv[None, :]
    cos, sin = jnp.cos(ang), jnp.sin(ang)
    return jnp.concatenate([cos, cos], axis=-1), jnp.concatenate([-sin, sin], axis=-1)


def _pick_rows(seq, target=512):
    r = min(seq, target)
    while seq % r:
        r //= 2
    return r


def kernel(x_prompt, x_sample, c_prompt, c_sample, cache_k, cache_v, state_pool, page_table, w_ada, b_ada, g_norm1, g_norm2, w_in, g_qnorm, g_knorm, w_pool, ls_pool, w_o_attn, w_o_pool, w_out, w_ffn_in, w_ffn_out):
    n_p, seq, d = x_prompt.shape
    n_s, dq, _ = x_sample.shape
    past = page_table.shape[1] * cache_k.shape[1]
    pool_w = state_pool.shape[2]
    n_state = state_pool.shape[1]

    w_qkvu_bf = w_in[:, :3 * ATTN_WIDTH + pool_w].astype(BF16)
    w_pool_bf = w_pool.astype(BF16)
    hidden = w_ffn_out.shape[0]

    n_c = n_p + n_s
    c_all = jnp.concatenate([c_prompt, c_sample], axis=0)
    c_all = jnp.pad(c_all, ((0, (-n_c) % 8), (0, 0)))
    mod = _ada(c_all, w_ada, b_ada)
    mod_p = mod[:n_p].reshape(n_p, 1, 6 * d)
    mod_s = mod[n_p:n_c].reshape(n_s, 1, 6 * d)

    cos_s, sin_s = _rope_tables(past + jnp.arange(dq))
    cos_s, sin_s = jnp.tile(cos_s, (n_s, 1)), jnp.tile(sin_s, (n_s, 1))
    q_s, k_s, v_s, u_s = _qkvu(x_sample, mod_s, g_norm1, w_qkvu_bf, g_qnorm, g_knorm, cos_s, sin_s, n_s, dq)
    attn_s = _moba_sample(q_s, k_s, v_s, cache_k, cache_v, page_table)
    halo_s = jnp.pad(state_pool, ((0, 0), (POOL_HALO - n_state, 0), (0, 0)))
    pooled_s = _pool(u_s, halo_s, w_pool_bf, ls_pool, n_s, dq, n_s, dq, past)
    gate0 = (w_in.shape[1] - 2 * d) // _TN_SAMPLE
    x1_s, merge_bf = _merge(
        x_sample, mod_s, g_norm1, attn_s, pooled_s,
        ((w_in, 1, gate0), (w_in, 1, gate0 + d // _TN_SAMPLE), (w_o_attn, 1, 0), (w_o_pool, 1, 0),
         (w_out, 0, 0)), n_s, dq, _TN_SAMPLE)
    y_s, ffn_bf = _ffn(
        x1_s, mod_s, g_norm2,
        ((w_ffn_in, 1, 0), (w_ffn_in, 1, hidden // _TN_SAMPLE), (w_ffn_out, 0, 0)),
        hidden, n_s, dq, _TN_SAMPLE)

    r_p = _pick_rows(seq)
    cos_p, sin_p = _rope_tables(jnp.arange(seq))
    q_p, k_p, v_p, u_p = _qkvu(x_prompt, mod_p, g_norm1, w_qkvu_bf, g_qnorm, g_knorm, cos_p, sin_p, 1, r_p)
    attn_p = _moba_prompt(q_p, k_p, v_p, n_p, seq)
    pooled_p = _pool(u_p, jnp.zeros((n_p, POOL_HALO, pool_w), F32), w_pool_bf, ls_pool,
                     n_p, seq, 1, r_p, 0)
    x1_p, _ = _merge(x_prompt, mod_p, g_norm1, attn_p, pooled_p,
                     tuple((w, axis, 0) for w, axis in zip(merge_bf, (1, 1, 1, 1, 0))), 1, r_p, _TN_PROMPT)
    y_p, _ = _ffn(x1_p, mod_p, g_norm2, tuple((w, axis, 0) for w, axis in zip(ffn_bf, (1, 1, 0))),
                  hidden, 1, r_p, _TN_PROMPT)

    k_prompt = k_p.reshape(n_p, seq, N_HEADS, HEAD_DIM)
    v_prompt = v_p.reshape(n_p, seq, N_HEADS, HEAD_DIM)
    pool_prompt = u_p.reshape(n_p, seq, pool_w)[:, seq - n_state:]
    k_sample = k_s.reshape(n_s, dq, N_HEADS, HEAD_DIM)
    v_sample = v_s.reshape(n_s, dq, N_HEADS, HEAD_DIM)
    pool_sample = jnp.concatenate([state_pool, u_s.reshape(n_s, dq, pool_w)], axis=1)[:, -n_state:]
    return (y_p, y_s, k_prompt, v_prompt, pool_prompt, k_sample, v_sample, pool_sample)
```

```python
import functools

import jax
import jax.numpy as jnp
from jax import lax
from jax.experimental import pallas as pl
from jax.experimental.pallas import tpu as pltpu

N_HEADS = 8
HEAD_DIM = 128
ATTN_WIDTH = N_HEADS * HEAD_DIM
MOBA_BLOCK = 256
MOBA_TOPK = 3
ROPE_THETA = 10000.0
POOL_WINDOWS = (2, 4, 8, 16)
POOL_HALO = 16
RMS_EPS = 1e-6
_TILE_GROUP = 4
_MASKED_MAX = -1e30
_FFN_ROWS = 1024
_TN_PROMPT = 512
_TN_SAMPLE = 256

F32 = jnp.float32
BF16 = jnp.bfloat16
NEG_INF = float("-inf")
_LOG2_E = 1.4426950408889634

_VMEM_LIMIT = 52 * 1024 * 1024


def _params(*sem):
    return pltpu.CompilerParams(dimension_semantics=sem, vmem_limit_bytes=_VMEM_LIMIT)


_NORM_ROWS = 16


def _mod_norm_store(x_ref, g_ref, sc_ref, sh_ref, h_scr):
    gb, r, d = x_ref.shape
    g = g_ref[...]
    if r >= _NORM_ROWS:
        per_group = r // _NORM_ROWS
        chunks = [(gi, 1, c * _NORM_ROWS, _NORM_ROWS) for gi in range(gb) for c in range(per_group)]
    else:
        n_g = _NORM_ROWS // r
        chunks = [(gi, n_g, 0, r) for gi in range(0, gb, n_g)]
    gains = {}
    for gi, n_g, r0, nr in chunks:
        if gi not in gains:
            gains[gi] = g * (1.0 + sc_ref[gi:gi + n_g])
        x = x_ref[gi:gi + n_g, r0:r0 + nr, :]
        ms = jnp.mean(x * x, axis=-1, keepdims=True)
        h = x * lax.rsqrt(ms + RMS_EPS) * gains[gi] + sh_ref[gi:gi + n_g]
        row0 = gi * r + r0
        h_scr[row0:row0 + n_g * nr, :] = h.reshape(n_g * nr, d).astype(BF16)


def _ada_kernel(c_ref, w_ref, b_ref, o_ref):
    o_ref[...] = jnp.dot(c_ref[...].astype(BF16), w_ref[...].astype(BF16),
                         preferred_element_type=F32) + b_ref[...]


def _ada(c_all, w_ada, b_ada, tn=1024):
    m, d = c_all.shape
    n = w_ada.shape[1]
    return pl.pallas_call(
        _ada_kernel,
        grid=(n // tn,),
        in_specs=[pl.BlockSpec((m, d), lambda j: (0, 0)),
                  pl.BlockSpec((d, tn), lambda j: (0, j)),
                  pl.BlockSpec((1, tn), lambda j: (0, j))],
        out_specs=pl.BlockSpec((m, tn), lambda j: (0, j)),
        out_shape=jax.ShapeDtypeStruct((m, n), F32),
        compiler_params=_params("arbitrary"),
        name="ada_mod",
    )(c_all, w_ada, b_ada.reshape(1, n))


def _head_norm_rope(th, g, cos, sin_signed):
    ms = jnp.mean(th * th, axis=-1, keepdims=True)
    y = th * lax.rsqrt(ms + RMS_EPS) * g
    return y * cos + pltpu.roll(y, HEAD_DIM // 2, 1) * sin_signed


_PROJ_CHUNK = 2 * HEAD_DIM


def _qkvu_kernel(x_ref, sh_ref, sc_ref, g1_ref, w_ref, gq_ref, gk_ref, cos_ref, sin_ref,
                 q_ref, k_ref, v_ref, u_ref, h_scr):
    _mod_norm_store(x_ref, g1_ref, sc_ref, sh_ref, h_scr)

    width = q_ref.shape[1]
    outs = ((q_ref, gq_ref), (k_ref, gk_ref), (v_ref, None), (u_ref, None))
    for i, (o_ref, g_ref) in enumerate(outs):
        for c in range(0, width, _PROJ_CHUNK):
            col = i * width + c
            acc = jnp.dot(h_scr[...], w_ref[:, col:col + _PROJ_CHUNK], preferred_element_type=F32)
            if g_ref is None:
                o_ref[:, c:c + _PROJ_CHUNK] = acc
            else:
                for hh in range(0, _PROJ_CHUNK, HEAD_DIM):
                    o_ref[:, c + hh:c + hh + HEAD_DIM] = _head_norm_rope(
                        acc[:, hh:hh + HEAD_DIM], g_ref[...], cos_ref[...], sin_ref[...])


def _group_specs(x3, gb, r):
    g_total, s, d = x3.shape
    n_g, n_t = g_total // gb, s // r
    return n_g, n_t, d


def _qkvu(x3, mod, g_norm1, w_in_bf, g_q, g_k, cos, sin_signed, gb, r):
    n_g, n_t, d = _group_specs(x3, gb, r)
    tm = gb * r
    tokens = x3.shape[0] * x3.shape[1]
    wq = ATTN_WIDTH
    x_spec = pl.BlockSpec((gb, r, d), lambda g, t: (g, t, 0))
    mod_spec = lambda k: pl.BlockSpec((gb, 1, d), lambda g, t: (g, 0, k))
    row_spec = pl.BlockSpec((1, d), lambda g, t: (0, 0))
    head_spec = pl.BlockSpec((1, HEAD_DIM), lambda g, t: (0, 0))
    tab_spec = pl.BlockSpec((tm, HEAD_DIM), lambda g, t: (t, 0))
    out_spec = pl.BlockSpec((tm, wq), lambda g, t: (g * n_t + t, 0))
    out_sds = jax.ShapeDtypeStruct((tokens, wq), F32)
    w_spec = pl.BlockSpec((d, 4 * wq), lambda g, t: (0, 0), pipeline_mode=pl.Buffered(1))
    return pl.pallas_call(
        _qkvu_kernel,
        grid=(n_g, n_t),
        in_specs=[x_spec, mod_spec(0), mod_spec(1), row_spec, w_spec,
                  head_spec, head_spec, tab_spec, tab_spec],
        out_specs=[out_spec] * 4,
        out_shape=[out_sds] * 4,
        scratch_shapes=[pltpu.VMEM((tm, d), BF16)],
        compiler_params=_params("arbitrary", "arbitrary"),
        name="qkvu_proj",
    )(x3, mod, mod, g_norm1.reshape(1, d), w_in_bf, g_q.reshape(1, HEAD_DIM),
      g_k.reshape(1, HEAD_DIM), cos, sin_signed)


def _topk_select(gate, valid, axis):
    n = gate.shape[axis]
    g = jnp.where(valid, gate, NEG_INF)
    idx = lax.broadcasted_iota(jnp.int32, gate.shape, axis)
    rank = jnp.zeros(gate.shape, jnp.int32)
    for m in range(n):
        gm = g[m:m + 1, :] if axis == 0 else g[:, m:m + 1]
        beats = (gm > g) | ((gm == g) & (m < idx))
        rank = rank + beats.astype(jnp.int32)
    return valid & (rank < MOBA_TOPK) & (jnp.abs(g) < float("inf"))


def _moba_prompt_kernel(q_ref, k_ref, v_ref, o_ref, kb_scr, vt_scr, kmean_scr, bias_scr, tri_scr, s_scr,
                        *, nb):
    blk = MOBA_BLOCK

    def prepare():
        for n in range(nb):
            kn = k_ref[n * blk:(n + 1) * blk, :]
            kb_scr[n * blk:(n + 1) * blk, :] = kn.astype(BF16)
            kmean_scr[n:n + 1, :] = jnp.mean(kn, axis=0, keepdims=True)
            vt_scr[n] = v_ref[n * blk:(n + 1) * blk, :].T.astype(BF16)
        kb_scr[nb * blk:, :] = jnp.zeros((kb_scr.shape[0] - nb * blk, HEAD_DIM), BF16)
        kmean = kmean_scr[...]
        for qb_i in range(nb):
            gate = lax.dot_general(kmean, q_ref[qb_i * blk:(qb_i + 1) * blk, :],
                                   (((1,), (1,)), ((), ())),
                                   precision=lax.Precision.HIGHEST, preferred_element_type=F32)
            n_idx = lax.broadcasted_iota(jnp.int32, gate.shape, 0)
            sel = _topk_select(gate, n_idx < qb_i, axis=0)
            bias_scr[qb_i, 0:nb, :] = jnp.where(sel | (n_idx == qb_i), 0.0, NEG_INF)
            bias_scr[qb_i, nb:, :] = jnp.full((bias_scr.shape[1] - nb, blk), NEG_INF, F32)
        kpos = lax.broadcasted_iota(jnp.int32, (blk, blk), 0)
        qpos = lax.broadcasted_iota(jnp.int32, (blk, blk), 1)
        tri_scr[0] = jnp.zeros((blk, blk), F32)
        tri_scr[1] = jnp.where(kpos <= qpos, 0.0, NEG_INF)

    prepare()

    exp2_scale = HEAD_DIM ** -0.5 * _LOG2_E
    width = _TILE_GROUP

    def fold(x, op):
        return op(x.reshape(blk // 8, 8, blk), axis=0)

    def queries(qi):
        return q_ref[pl.ds(pl.multiple_of(qi * blk, blk), blk), :].astype(BF16)

    def score_group(qi, qb, g, slot):
        j0 = g * width
        keys = kb_scr[pl.ds(pl.multiple_of(j0 * blk, blk), width * blk), :]
        s_all = lax.dot_general(keys, qb, (((1,), (1,)), ((), ())), preferred_element_type=F32)
        m8 = jnp.full((8, blk), NEG_INF, F32)
        for t in range(width):
            j = j0 + t
            s = (s_all[t * blk:(t + 1) * blk, :] + bias_scr[qi, pl.ds(j, 1), :]
                 + tri_scr[jnp.asarray(j == qi, jnp.int32)])
            s_scr[slot, t] = s
            m8 = jnp.maximum(m8, fold(s, jnp.max))
        return m8

    def absorb_group(g, slot, m_run, l8, acc, m8):
        m_new = jnp.maximum(m_run, jnp.max(m8, axis=0, keepdims=True))
        alpha = jnp.exp2((m_run - m_new) * exp2_scale)
        l8 = l8 * alpha
        acc = acc * alpha
        for t in range(width):
            p = jnp.exp2((s_scr[slot, t] - m_new) * exp2_scale)
            l8 = l8 + fold(p, jnp.sum)
            acc = acc + jnp.dot(vt_scr[jnp.minimum(g * width + t, nb - 1)], p.astype(BF16),
                                preferred_element_type=F32)
        return m_new, l8, acc

    def one_query_block(qi, state):
        m8_first, first_slot = state
        qb = queries(qi)
        n_groups = (qi + width) // width

        def body(g, carry):
            m_run, l8, acc, m8 = carry
            m_run, l8, acc = absorb_group(g, (first_slot + g) % 2, m_run, l8, acc, m8)
            return m_run, l8, acc, score_group(qi, qb, g + 1, (first_slot + g + 1) % 2)

        carry = (jnp.full((1, blk), _MASKED_MAX, F32), jnp.zeros((8, blk), F32),
                 jnp.zeros((HEAD_DIM, blk), F32), m8_first)
        m_run, l8, acc, m8 = lax.fori_loop(0, n_groups - 1, body, carry)
        last_slot = (first_slot + n_groups - 1) % 2
        _, l8, acc = absorb_group(n_groups - 1, last_slot, m_run, l8, acc, m8)
        nxt = jnp.minimum(qi + 1, nb - 1)
        m8_next = score_group(nxt, queries(nxt), 0, 1 - last_slot)
        l = jnp.sum(l8, axis=0, keepdims=True)
        o_ref[pl.ds(pl.multiple_of(qi * blk, blk), blk), :] = (acc / l).T.astype(o_ref.dtype)
        return m8_next, 1 - last_slot

    lax.fori_loop(0, nb, one_query_block, (score_group(0, queries(0), 0, 0), jnp.int32(0)))


def _moba_prompt(q, k, v, n_seq, seq):
    nb = seq // MOBA_BLOCK
    nb_pad = -(-nb // _TILE_GROUP) * _TILE_GROUP + _TILE_GROUP
    blk = MOBA_BLOCK
    head_spec = pl.BlockSpec((seq, HEAD_DIM), lambda b, h: (b, h))
    return pl.pallas_call(
        functools.partial(_moba_prompt_kernel, nb=nb),
        grid=(n_seq, N_HEADS),
        in_specs=[head_spec, head_spec, head_spec],
        out_specs=head_spec,
        out_shape=jax.ShapeDtypeStruct(q.shape, BF16),
        scratch_shapes=[pltpu.VMEM((nb_pad * blk, HEAD_DIM), BF16),
                        pltpu.VMEM((nb, HEAD_DIM, blk), BF16),
                        pltpu.VMEM((nb, HEAD_DIM), F32),
                        pltpu.VMEM((nb, nb_pad, blk), F32),
                        pltpu.VMEM((2, blk, blk), F32),
                        pltpu.VMEM((2, _TILE_GROUP, blk, blk), F32)],
        compiler_params=_params("arbitrary", "arbitrary"),
        name="moba_prompt",
    )(q, k, v)


def _page_copy_groups(cache_hbm, pt_ref, buf, sem, slot, b, first_page, hp, n_pages, ppb, wanted):
    groups = []
    for p0 in range(0, hp, ppb):
        pgs = [pt_ref[b * n_pages + first_page + p0 + i] for i in range(ppb)]
        for h in range(N_HEADS):
            pred = None if wanted is None else wanted(b, (first_page + p0) // ppb, h)
            groups.append((pred, [pltpu.make_async_copy(cache_hbm.at[pgs[i], :, h, :],
                                                        buf.at[slot, p0 + i, h], sem.at[slot])
                                  for i in range(ppb)]))
    return groups


def _for_each_copy(groups, method):
    for pred, copies in groups:
        def run(copies=copies):
            for c in copies:
                getattr(c, method)()
        if pred is None:
            run()
        else:
            pl.when(pred)(run)


def _paged_slots(pt_ref, cache_hbm, buf, sem, hp, n_pages, ppb, compute, b, step, n_b, n_steps,
                 wanted=None):
    n_slots = buf.shape[0]
    is_first = jnp.logical_and(b == 0, step == 0)
    is_last = jnp.logical_and(b == n_b - 1, step == n_steps - 1)
    wrap = step == n_steps - 1
    nxt_b = jnp.where(wrap, b + 1, b)
    nxt_step = jnp.where(wrap, 0, step + 1)

    def groups(bb, ss, slot):
        return _page_copy_groups(cache_hbm, pt_ref, buf, sem, slot, bb, (n_slots * ss + slot) * hp,
                                 hp, n_pages, ppb, wanted)

    @pl.when(is_first)
    def _():
        for slot in range(n_slots):
            _for_each_copy(groups(b, step, slot), "start")

    for slot in range(n_slots):
        _for_each_copy(groups(b, step, slot), "wait")
        compute(slot, (n_slots * step + slot) * hp)

        @pl.when(jnp.logical_not(is_last))
        def _():
            _for_each_copy(groups(nxt_b, nxt_step, slot), "start")


def _moba_sample_k_kernel(pt_ref, q_ref, knew_ref, vnew_ref, cache_hbm, p_ref, oown_ref, linv_ref,
                          used_ref, buf, sem, s_scr, kpart_scr, qbd_scr, *, hp, page, n_blocks):
    step = pl.program_id(1)
    n_steps = pl.num_programs(1)
    dq = q_ref.shape[0]
    rows = N_HEADS * dq
    scale = HEAD_DIM ** -0.5
    pages_per_block = MOBA_BLOCK // page
    n_pages = n_blocks * pages_per_block

    qh = [q_ref[:, h * HEAD_DIM:(h + 1) * HEAD_DIM] for h in range(N_HEADS)]
    qh_bf = [x.astype(BF16) for x in qh]

    @pl.when(step == 0)
    def _():
        kpart_scr[...] = jnp.zeros_like(kpart_scr)
        q_rep = jnp.concatenate([q_ref[...]] * N_HEADS
                                + [jnp.zeros((qbd_scr.shape[1] - rows, ATTN_WIDTH), F32)], axis=0)
        row_head = lax.broadcasted_iota(jnp.int32, q_rep.shape, 0) // dq
        lane_head = lax.broadcasted_iota(jnp.int32, q_rep.shape, 1) // HEAD_DIM
        qbd_scr[...] = jnp.where(row_head == lane_head, q_rep, 0.0).T.astype(BF16)

    def compute(slot, first_page):
        heads = [[buf[slot, p, h] for h in range(N_HEADS)] for p in range(hp)]
        for p in range(hp):
            blk = (first_page + p) // pages_per_block
            for h in range(N_HEADS):
                kpart_scr[blk, h] += jnp.sum(heads[p][h].reshape(page // 8, 8, HEAD_DIM), axis=0)
        for p0 in range(0, hp, hp // 2):
            keys = jnp.concatenate([jnp.concatenate(hs, axis=1) for hs in heads[p0:p0 + hp // 2]],
                                   axis=0)
            s_t = jnp.dot(keys.astype(BF16), qbd_scr[...], preferred_element_type=F32) * scale
            for i in range(hp // 2):
                s_scr[first_page + p0 + i] = s_t[i * page:(i + 1) * page, :].T[:rows, :]

    _paged_slots(pt_ref, cache_hbm, buf, sem, hp, n_pages, pages_per_block, compute,
                 pl.program_id(0), step, pl.num_programs(0), n_steps)

    @pl.when(step == n_steps - 1)
    def _():
        gates = []
        for h in range(N_HEADS):
            kmean_h = jnp.sum(kpart_scr[:, h], axis=1) * (1.0 / MOBA_BLOCK)
            gates.append(lax.dot_general(qh[h], kmean_h, (((1,), (1,)), ((), ())),
                                         precision=lax.Precision.HIGHEST,
                                         preferred_element_type=F32))
        gate = jnp.concatenate(gates, axis=0)
        sel = _topk_select(gate, jnp.ones(gate.shape, jnp.bool_), axis=1)
        bias = jnp.where(sel, 0.0, NEG_INF)
        used_ref[0] = jnp.concatenate(
            [jnp.max(sel[h * dq:(h + 1) * dq, :].astype(jnp.int32), axis=0, keepdims=True)
             for h in range(N_HEADS)], axis=0)

        s_own = []
        for h in range(N_HEADS):
            kn = knew_ref[:, h * HEAD_DIM:(h + 1) * HEAD_DIM].astype(BF16)
            s_own.append(lax.dot_general(qh_bf[h], kn, (((1,), (1,)), ((), ())),
                                         preferred_element_type=F32) * scale)
        s_own = jnp.concatenate(s_own, axis=0)
        row = lax.broadcasted_iota(jnp.int32, s_own.shape, 0) % dq
        col = lax.broadcasted_iota(jnp.int32, s_own.shape, 1)
        s_own = jnp.where(col <= row, s_own, NEG_INF)

        m_vec = jnp.full((rows, page), NEG_INF, F32)
        for g in range(n_pages):
            n = g // pages_per_block
            m_vec = jnp.maximum(m_vec, s_scr[g] + bias[:, n:n + 1])
        m = jnp.maximum(jnp.max(m_vec, axis=1, keepdims=True), jnp.max(s_own, axis=1, keepdims=True))
        p_own = jnp.exp(s_own - m)
        l_vec = jnp.zeros((rows, page), F32)
        pps = buf.shape[0] * hp
        for g in range(n_pages):
            n = g // pages_per_block
            pb = jnp.exp(s_scr[g] + bias[:, n:n + 1] - m)
            l_vec = l_vec + pb
            p_ref[0, g // pps, :, (g % pps) * page:(g % pps + 1) * page] = pb
        l = jnp.sum(l_vec, axis=1, keepdims=True) + jnp.sum(p_own, axis=1, keepdims=True)
        linv_ref[0] = jnp.broadcast_to(1.0 / l, (rows, HEAD_DIM))
        for h in range(N_HEADS):
            vn = vnew_ref[:, h * HEAD_DIM:(h + 1) * HEAD_DIM].astype(BF16)
            oown_ref[0, h * dq:(h + 1) * dq, :] = jnp.dot(
                p_own[h * dq:(h + 1) * dq, :].astype(BF16), vn, preferred_element_type=F32)


def _moba_sample_v_kernel(pt_ref, used_ref, p_ref, oown_ref, linv_ref, cache_hbm, o_ref, buf, sem,
                          acc_scr, *, hp, page, n_pages):
    b, n_b = pl.program_id(0), pl.num_programs(0)
    dq = o_ref.shape[0]
    ppb = MOBA_BLOCK // page
    n_blocks = n_pages // ppb
    n_steps = n_pages // (buf.shape[0] * hp)

    @pl.when(b == 0)
    def _():
        buf[...] = jnp.zeros_like(buf)

    def wanted(bb, blk, h):
        return used_ref[(bb * N_HEADS + h) * n_blocks + blk] != 0

    acc_scr[...] = oown_ref[0]

    def one_step(step, token):
        def compute(slot, first_page):
            for h in range(N_HEADS):
                tot = jnp.zeros((dq, HEAD_DIM), F32)
                for p in range(hp):
                    col0 = (slot * hp + p) * page
                    ph = p_ref[0, step, h * dq:(h + 1) * dq, col0:col0 + page].astype(BF16)
                    tot = tot + jnp.dot(ph, buf[slot, p, h].astype(BF16), preferred_element_type=F32)
                acc_scr[h * dq:(h + 1) * dq, :] += tot

        _paged_slots(pt_ref, cache_hbm, buf, sem, hp, n_pages, ppb, compute, b, step, n_b, n_steps,
                     wanted)
        return token

    lax.fori_loop(0, n_steps, one_step, 0)

    for h in range(N_HEADS):
        rows_h = slice(h * dq, (h + 1) * dq)
        o_ref[:, h * HEAD_DIM:(h + 1) * HEAD_DIM] = acc_scr[rows_h, :] * linv_ref[0, rows_h, :]


def _moba_sample(q, k_new, v_new, cache_k, cache_v, page_table, hp=4, n_slots=4):
    dec_b, n_pages = page_table.shape
    page = cache_k.shape[1]
    dq = q.shape[0] // dec_b
    past = n_pages * page
    n_blocks = past // MOBA_BLOCK
    while n_pages % (n_slots * hp):
        n_slots //= 2
    n_steps = n_pages // (n_slots * hp)
    rows = N_HEADS * dq
    step_keys = n_slots * hp * page
    pt = page_table.reshape(-1)

    new_spec = pl.BlockSpec((dq, ATTN_WIDTH), lambda b, s, pt_ref: (b, 0))
    any_spec = pl.BlockSpec(memory_space=pl.ANY)
    stat_spec = pl.BlockSpec((1, rows, HEAD_DIM), lambda b, s, pt_ref: (b, 0, 0))
    stat_sds = jax.ShapeDtypeStruct((dec_b, rows, HEAD_DIM), F32)
    assert hp % (MOBA_BLOCK // page) == 0, "a page half must hold whole key blocks"
    page_buf = pltpu.VMEM((n_slots, hp, N_HEADS, page, HEAD_DIM), F32)
    probs, o_own, l_inv, used = pl.pallas_call(
        functools.partial(_moba_sample_k_kernel, hp=hp, page=page, n_blocks=n_blocks),
        grid_spec=pltpu.PrefetchScalarGridSpec(
            num_scalar_prefetch=1,
            grid=(dec_b, n_steps),
            in_specs=[new_spec, new_spec, new_spec, any_spec],
            out_specs=[pl.BlockSpec((1, n_steps, rows, step_keys), lambda b, s, pt_ref: (b, 0, 0, 0)),
                       stat_spec, stat_spec,
                       pl.BlockSpec((1, N_HEADS, n_blocks), lambda b, s, pt_ref: (b, 0, 0))],
            scratch_shapes=[page_buf, pltpu.SemaphoreType.DMA((n_slots,)),
                            pltpu.VMEM((n_pages, rows, page), F32),
                            pltpu.VMEM((n_blocks, N_HEADS, 8, HEAD_DIM), F32),
                            pltpu.VMEM((ATTN_WIDTH, max(rows, HEAD_DIM)), BF16)]),
        out_shape=[jax.ShapeDtypeStruct((dec_b, n_steps, rows, step_keys), F32), stat_sds, stat_sds,
                   jax.ShapeDtypeStruct((dec_b, N_HEADS, n_blocks), jnp.int32)],
        compiler_params=_params("arbitrary", "arbitrary"),
        name="moba_sample_k",
    )(pt, q, k_new, v_new, cache_k)

    return pl.pallas_call(
        functools.partial(_moba_sample_v_kernel, hp=hp, page=page, n_pages=n_pages),
        grid_spec=pltpu.PrefetchScalarGridSpec(
            num_scalar_prefetch=2,
            grid=(dec_b,),
            in_specs=[pl.BlockSpec((1, n_steps, rows, step_keys), lambda b, *_: (b, 0, 0, 0)),
                      pl.BlockSpec((1, rows, HEAD_DIM), lambda b, *_: (b, 0, 0)),
                      pl.BlockSpec((1, rows, HEAD_DIM), lambda b, *_: (b, 0, 0)),
                      any_spec],
            out_specs=pl.BlockSpec((dq, ATTN_WIDTH), lambda b, *_: (b, 0)),
            scratch_shapes=[page_buf, pltpu.SemaphoreType.DMA((n_slots,)),
                            pltpu.VMEM((rows, HEAD_DIM), F32)]),
        out_shape=jax.ShapeDtypeStruct(q.shape, F32),
        compiler_params=_params("arbitrary"),
        name="moba_sample_v",
    )(pt, used.reshape(-1), probs, o_own, l_inv, cache_v)


def _pool_kernel(u_ref, halo_ref, w_ref, ls_ref, o_ref, ext_scr, *, pos0):
    t = pl.program_id(1)
    gb, r, width = u_ref.shape
    grp = width // len(POOL_WINDOWS)

    @pl.when(t == 0)
    def _():
        ext_scr[:, 0:POOL_HALO, :] = halo_ref[...]

    @pl.when(t > 0)
    def _():
        ext_scr[:, 0:POOL_HALO, :] = ext_scr[:, r:r + POOL_HALO, :]

    ext_scr[:, POOL_HALO:POOL_HALO + r, :] = u_ref[...]

    pos = pos0 + t * r + lax.broadcasted_iota(jnp.int32, (1, r, grp), 1)
    for g, w in enumerate(POOL_WINDOWS):
        cols = slice(g * grp, (g + 1) * grp)
        tot = ext_scr[:, POOL_HALO:POOL_HALO + r, cols]
        for d in range(1, w):
            tot = tot + ext_scr[:, POOL_HALO - d:POOL_HALO - d + r, cols]
        cnt = jnp.minimum(w, pos + 1).astype(F32)
        y = tot / cnt - u_ref[:, :, cols]
        y = jnp.dot(y.reshape(gb * r, grp).astype(BF16), w_ref[g], preferred_element_type=F32)
        o_ref[:, cols] = (y * ls_ref[:, cols]).astype(o_ref.dtype)


def _pool(u, halo, w_pool_bf, ls_pool, n_seq, seq, gb, r, pos0):
    width = u.shape[1]
    n_g, n_t = n_seq // gb, seq // r
    u3 = u.reshape(n_seq, seq, width)
    return pl.pallas_call(
        functools.partial(_pool_kernel, pos0=pos0),
        grid=(n_g, n_t),
        in_specs=[pl.BlockSpec((gb, r, width), lambda g, t: (g, t, 0)),
                  pl.BlockSpec((gb, POOL_HALO, width), lambda g, t: (g, 0, 0)),
                  pl.BlockSpec(w_pool_bf.shape, lambda g, t: (0, 0, 0)),
                  pl.BlockSpec((1, width), lambda g, t: (0, 0))],
        out_specs=pl.BlockSpec((gb * r, width), lambda g, t: (g * n_t + t, 0)),
        out_shape=jax.ShapeDtypeStruct(u.shape, BF16),
        scratch_shapes=[pltpu.VMEM((gb, r + POOL_HALO, width), F32)],
        compiler_params=_params("arbitrary", "arbitrary"),
        name="pool_mixer",
    )(u3, halo, w_pool_bf, ls_pool.reshape(1, width))


def _bf16_tiles(w_refs, copy_refs):
    tiles = []
    for i, w_ref in enumerate(w_refs):
        w = w_ref[...]
        if w.dtype != BF16:
            w = w.astype(BF16)
            copy_refs[i][...] = w
        tiles.append(w)
    return tiles


def _merge_kernel(x_ref, sh_ref, sc_ref, gt_ref, g1_ref, attn_ref, pool_ref,
                  wga_ref, wgb_ref, woa_ref, wob_ref, wout_ref, o_ref, *rest):
    *copy_refs, h_scr = rest
    c = pl.program_id(2)

    @pl.when(c == 0)
    def _():
        _mod_norm_store(x_ref, g1_ref, sc_ref, sh_ref, h_scr)
        o_ref[...] = jnp.zeros_like(o_ref)

    wga, wgb, woa, wob, wout = _bf16_tiles((wga_ref, wgb_ref, woa_ref, wob_ref, wout_ref), copy_refs)
    h = h_scr[...]
    g_a = jax.nn.sigmoid(jnp.dot(h, wga, preferred_element_type=F32))
    g_b = jax.nn.sigmoid(jnp.dot(h, wgb, preferred_element_type=F32))
    y_a = jnp.dot(attn_ref[...].astype(BF16), woa, preferred_element_type=F32)
    y_b = jnp.dot(pool_ref[...].astype(BF16), wob, preferred_element_type=F32)
    mix_in = (g_a * y_a + g_b * y_b).astype(BF16)
    o_ref[...] += jnp.dot(mix_in, wout, preferred_element_type=F32).reshape(o_ref.shape)

    @pl.when(c == pl.num_programs(2) - 1)
    def _():
        o_ref[...] = x_ref[...] + gt_ref[...] * o_ref[...]


def _column_weight_specs(weights, tn, n_c):
    in_specs, copy_specs, copy_shapes = [], [], []
    for w, axis, first in weights:
        if axis == 1:
            in_specs.append(pl.BlockSpec((w.shape[0], tn), lambda g, t, c, first=first: (0, first + c)))
            copy_specs.append(pl.BlockSpec((w.shape[0], tn), lambda g, t, c: (0, c)))
            copy_shapes.append(jax.ShapeDtypeStruct((w.shape[0], n_c * tn), BF16))
        else:
            in_specs.append(pl.BlockSpec((tn, w.shape[1]), lambda g, t, c, first=first: (first + c, 0)))
            copy_specs.append(pl.BlockSpec((tn, w.shape[1]), lambda g, t, c: (c, 0)))
            copy_shapes.append(jax.ShapeDtypeStruct((n_c * tn, w.shape[1]), BF16))
    if all(w.dtype == BF16 for w, _, _ in weights):
        return in_specs, [], []
    return in_specs, copy_specs, copy_shapes


def _merge(x3, mod, g_norm1, attn, pooled, weights, gb, r, tn):
    n_g, n_t, d = _group_specs(x3, gb, r)
    tm = gb * r
    aw = attn.shape[1]
    pw = pooled.shape[1]
    n_c = d // tn
    w_specs, copy_specs, copy_shapes = _column_weight_specs(weights, tn, n_c)
    if copy_specs:
        assert n_g * n_t == 1, "weight copies are written by a single token tile"
    x_spec = pl.BlockSpec((gb, r, d), lambda g, t, c: (g, t, 0))
    mod_spec = lambda k: pl.BlockSpec((gb, 1, d), lambda g, t, c: (g, 0, k))
    out, *copies = pl.pallas_call(
        _merge_kernel,
        grid=(n_g, n_t, n_c),
        in_specs=[x_spec, mod_spec(0), mod_spec(1), mod_spec(2),
                  pl.BlockSpec((1, d), lambda g, t, c: (0, 0)),
                  pl.BlockSpec((tm, aw), lambda g, t, c: (g * n_t + t, 0)),
                  pl.BlockSpec((tm, pw), lambda g, t, c: (g * n_t + t, 0))] + w_specs,
        out_specs=[x_spec] + copy_specs,
        out_shape=[jax.ShapeDtypeStruct(x3.shape, F32)] + copy_shapes,
        scratch_shapes=[pltpu.VMEM((tm, d), BF16)],
        compiler_params=_params("arbitrary", "arbitrary", "arbitrary"),
        name="merge_out",
    )(x3, mod, mod, mod, g_norm1.reshape(1, d), attn, pooled, *[w for w, _, _ in weights])
    return out, copies


def _ffn_kernel(x_ref, sh_ref, sc_ref, gt_ref, g2_ref, wa_ref, wb_ref, wo_ref, o_ref, *rest):
    *copy_refs, h_scr = rest
    c = pl.program_id(2)

    @pl.when(c == 0)
    def _():
        _mod_norm_store(x_ref, g2_ref, sc_ref, sh_ref, h_scr)
        o_ref[...] = jnp.zeros_like(o_ref)

    wa, wb, wo = _bf16_tiles((wa_ref, wb_ref, wo_ref), copy_refs)
    h = h_scr[...]
    a = jnp.dot(h, wa, preferred_element_type=F32)
    b = jnp.dot(h, wb, preferred_element_type=F32)
    hid = (jax.nn.silu(a) * b).astype(BF16)
    o_ref[...] += jnp.dot(hid, wo, preferred_element_type=F32).reshape(o_ref.shape)

    @pl.when(c == pl.num_programs(2) - 1)
    def _():
        o_ref[...] = x_ref[...] + gt_ref[...] * o_ref[...]


def _ffn(x3, mod, g_norm2, weights, hidden, gb, r, tn):
    n_g, n_t, d = _group_specs(x3, gb, r)
    tm = gb * r
    n_c = hidden // tn
    w_specs, copy_specs, copy_shapes = _column_weight_specs(weights, tn, n_c)
    if copy_specs:
        assert n_g * n_t == 1, "weight copies are written by a single token tile"
    x_spec = pl.BlockSpec((gb, r, d), lambda g, t, c: (g, t, 0))
    x_in_spec = pl.BlockSpec((gb, r, d), lambda g, t, c: (g, t, 0), pipeline_mode=pl.Buffered(1))
    mod_spec = lambda k: pl.BlockSpec((gb, 1, d), lambda g, t, c: (g, 0, k))
    out, *copies = pl.pallas_call(
        _ffn_kernel,
        grid=(n_g, n_t, n_c),
        in_specs=[x_in_spec, mod_spec(3), mod_spec(4), mod_spec(5),
                  pl.BlockSpec((1, d), lambda g, t, c: (0, 0))] + w_specs,
        out_specs=[x_spec] + copy_specs,
        out_shape=[jax.ShapeDtypeStruct(x3.shape, F32)] + copy_shapes,
        scratch_shapes=[pltpu.VMEM((tm, d), BF16)],
        compiler_params=_params("arbitrary", "arbitrary", "arbitrary"),
        name="ffn_swiglu",
    )(x3, mod, mod, mod, g_norm2.reshape(1, d), *[w for w, _, _ in weights])
    return out, copies


def _rope_tables(pos):
    half = HEAD_DIM // 2
    inv = ROPE_THETA ** (-jnp.arange(half, dtype=F32) / half)
    ang = pos.astype(F32)[:, None] * inv[None, :]
    cos, sin = jnp.cos(ang), jnp.sin(ang)
    return jnp.concatenate([cos, cos], axis=-1), jnp.concatenate([-sin, sin], axis=-1)


def _pick_rows(seq, target=512):
    r = min(seq, target)
    while seq % r:
        r //= 2
    return r


def kernel(x_prompt, x_sample, c_prompt, c_sample, cache_k, cache_v, state_pool, page_table, w_ada, b_ada, g_norm1, g_norm2, w_in, g_qnorm, g_knorm, w_pool, ls_pool, w_o_attn, w_o_pool, w_out, w_ffn_in, w_ffn_out):
    n_p, seq, d = x_prompt.shape
    n_s, dq, _ = x_sample.shape
    past = page_table.shape[1] * cache_k.shape[1]
    pool_w = state_pool.shape[2]
    n_state = state_pool.shape[1]

    w_qkvu_bf = w_in[:, :3 * ATTN_WIDTH + pool_w].astype(BF16)
    w_pool_bf = w_pool.astype(BF16)
    hidden = w_ffn_out.shape[0]

    n_c = n_p + n_s
    c_all = jnp.concatenate([c_prompt, c_sample], axis=0)
    c_all = jnp.pad(c_all, ((0, (-n_c) % 8), (0, 0)))
    mod = _ada(c_all, w_ada, b_ada)
    mod_p = mod[:n_p].reshape(n_p, 1, 6 * d)
    mod_s = mod[n_p:n_c].reshape(n_s, 1, 6 * d)

    cos_s, sin_s = _rope_tables(past + jnp.arange(dq))
    cos_s, sin_s = jnp.tile(cos_s, (n_s, 1)), jnp.tile(sin_s, (n_s, 1))
    q_s, k_s, v_s, u_s = _qkvu(x_sample, mod_s, g_norm1, w_qkvu_bf, g_qnorm, g_knorm, cos_s, sin_s, n_s, dq)
    attn_s = _moba_sample(q_s, k_s, v_s, cache_k, cache_v, page_table)
    halo_s = jnp.pad(state_pool, ((0, 0), (POOL_HALO - n_state, 0), (0, 0)))
    pooled_s = _pool(u_s, halo_s, w_pool_bf, ls_pool, n_s, dq, n_s, dq, past)
    gate0 = (w_in.shape[1] - 2 * d) // _TN_SAMPLE
    x1_s, merge_bf = _merge(
        x_sample, mod_s, g_norm1, attn_s, pooled_s,
        ((w_in, 1, gate0), (w_in, 1, gate0 + d // _TN_SAMPLE), (w_o_attn, 1, 0), (w_o_pool, 1, 0),
         (w_out, 0, 0)), n_s, dq, _TN_SAMPLE)
    y_s, ffn_bf = _ffn(
        x1_s, mod_s, g_norm2,
        ((w_ffn_in, 1, 0), (w_ffn_in, 1, hidden // _TN_PROMPT), (w_ffn_out, 0, 0)),
        hidden, n_s, dq, _TN_PROMPT)

    r_p = _pick_rows(seq)
    cos_p, sin_p = _rope_tables(jnp.arange(seq))
    q_p, k_p, v_p, u_p = _qkvu(x_prompt, mod_p, g_norm1, w_qkvu_bf, g_qnorm, g_knorm, cos_p, sin_p, 1, r_p)
    attn_p = _moba_prompt(q_p, k_p, v_p, n_p, seq)
    pooled_p = _pool(u_p, jnp.zeros((n_p, POOL_HALO, pool_w), F32), w_pool_bf, ls_pool,
                     n_p, seq, 1, r_p, 0)
    x1_p, _ = _merge(x_prompt, mod_p, g_norm1, attn_p, pooled_p,
                     tuple((w, axis, 0) for w, axis in zip(merge_bf, (1, 1, 1, 1, 0))), 1, r_p, _TN_PROMPT)
    y_p, _ = _ffn(x1_p, mod_p, g_norm2, tuple((w, axis, 0) for w, axis in zip(ffn_bf, (1, 1, 0))),
                  hidden, 1, _pick_rows(seq, _FFN_ROWS), _TN_PROMPT)

    k_prompt = k_p.reshape(n_p, seq, N_HEADS, HEAD_DIM)
    v_prompt = v_p.reshape(n_p, seq, N_HEADS, HEAD_DIM)
    pool_prompt = u_p.reshape(n_p, seq, pool_w)[:, seq - n_state:]
    k_sample = k_s.reshape(n_s, dq, N_HEADS, HEAD_DIM)
    v_sample = v_s.reshape(n_s, dq, N_HEADS, HEAD_DIM)
    pool_sample = jnp.concatenate([state_pool, u_s.reshape(n_s, dq, pool_w)], axis=1)[:, -n_state:]
    return (y_p, y_s, k_prompt, v_prompt, pool_prompt, k_sample, v_sample, pool_sample)
```

```python
import functools

import jax
import jax.numpy as jnp
from jax import lax
from jax.experimental import pallas as pl
from jax.experimental.pallas import tpu as pltpu

N_HEADS = 8
HEAD_DIM = 128
ATTN_WIDTH = N_HEADS * HEAD_DIM
MOBA_BLOCK = 256
MOBA_TOPK = 3
ROPE_THETA = 10000.0
POOL_WINDOWS = (2, 4, 8, 16)
POOL_HALO = 16
RMS_EPS = 1e-6
_TILE_GROUP = 4
_MASKED_MAX = -1e30
_TN_PROMPT = 512
_TN_SAMPLE = 256

F32 = jnp.float32
BF16 = jnp.bfloat16
NEG_INF = float("-inf")
_LOG2_E = 1.4426950408889634

_VMEM_LIMIT = 52 * 1024 * 1024


def _params(*sem):
    return pltpu.CompilerParams(dimension_semantics=sem, vmem_limit_bytes=_VMEM_LIMIT)


_NORM_ROWS = 16


def _mod_norm_store(x_ref, g_ref, sc_ref, sh_ref, h_scr):
    gb, r, d = x_ref.shape
    g = g_ref[...]
    if r >= _NORM_ROWS:
        per_group = r // _NORM_ROWS
        chunks = [(gi, 1, c * _NORM_ROWS, _NORM_ROWS) for gi in range(gb) for c in range(per_group)]
    else:
        n_g = _NORM_ROWS // r
        chunks = [(gi, n_g, 0, r) for gi in range(0, gb, n_g)]
    gains = {}
    for gi, n_g, r0, nr in chunks:
        if gi not in gains:
            gains[gi] = g * (1.0 + sc_ref[gi:gi + n_g])
        x = x_ref[gi:gi + n_g, r0:r0 + nr, :]
        ms = jnp.mean(x * x, axis=-1, keepdims=True)
        h = x * lax.rsqrt(ms + RMS_EPS) * gains[gi] + sh_ref[gi:gi + n_g]
        row0 = gi * r + r0
        h_scr[row0:row0 + n_g * nr, :] = h.reshape(n_g * nr, d).astype(BF16)


def _ada_kernel(c_ref, w_ref, b_ref, o_ref):
    o_ref[...] = jnp.dot(c_ref[...].astype(BF16), w_ref[...].astype(BF16),
                         preferred_element_type=F32) + b_ref[...]


def _ada(c_all, w_ada, b_ada, tn=1024):
    m, d = c_all.shape
    n = w_ada.shape[1]
    return pl.pallas_call(
        _ada_kernel,
        grid=(n // tn,),
        in_specs=[pl.BlockSpec((m, d), lambda j: (0, 0)),
                  pl.BlockSpec((d, tn), lambda j: (0, j)),
                  pl.BlockSpec((1, tn), lambda j: (0, j))],
        out_specs=pl.BlockSpec((m, tn), lambda j: (0, j)),
        out_shape=jax.ShapeDtypeStruct((m, n), F32),
        compiler_params=_params("arbitrary"),
        name="ada_mod",
    )(c_all, w_ada, b_ada.reshape(1, n))


def _head_norm_rope(th, g, cos, sin_signed):
    ms = jnp.mean(th * th, axis=-1, keepdims=True)
    y = th * lax.rsqrt(ms + RMS_EPS) * g
    return y * cos + pltpu.roll(y, HEAD_DIM // 2, 1) * sin_signed


_PROJ_CHUNK = 2 * HEAD_DIM


def _qkvu_kernel(x_ref, sh_ref, sc_ref, g1_ref, w_ref, gq_ref, gk_ref, cos_ref, sin_ref,
                 q_ref, k_ref, v_ref, u_ref, h_scr):
    _mod_norm_store(x_ref, g1_ref, sc_ref, sh_ref, h_scr)

    width = q_ref.shape[1]
    outs = ((q_ref, gq_ref), (k_ref, gk_ref), (v_ref, None), (u_ref, None))
    for i, (o_ref, g_ref) in enumerate(outs):
        for c in range(0, width, _PROJ_CHUNK):
            col = i * width + c
            acc = jnp.dot(h_scr[...], w_ref[:, col:col + _PROJ_CHUNK], preferred_element_type=F32)
            if g_ref is None:
                o_ref[:, c:c + _PROJ_CHUNK] = acc
            else:
                for hh in range(0, _PROJ_CHUNK, HEAD_DIM):
                    o_ref[:, c + hh:c + hh + HEAD_DIM] = _head_norm_rope(
                        acc[:, hh:hh + HEAD_DIM], g_ref[...], cos_ref[...], sin_ref[...])


def _group_specs(x3, gb, r):
    g_total, s, d = x3.shape
    n_g, n_t = g_total // gb, s // r
    return n_g, n_t, d


def _qkvu(x3, mod, g_norm1, w_in_bf, g_q, g_k, cos, sin_signed, gb, r):
    n_g, n_t, d = _group_specs(x3, gb, r)
    tm = gb * r
    tokens = x3.shape[0] * x3.shape[1]
    wq = ATTN_WIDTH
    x_spec = pl.BlockSpec((gb, r, d), lambda g, t: (g, t, 0))
    mod_spec = lambda k: pl.BlockSpec((gb, 1, d), lambda g, t: (g, 0, k))
    row_spec = pl.BlockSpec((1, d), lambda g, t: (0, 0))
    head_spec = pl.BlockSpec((1, HEAD_DIM), lambda g, t: (0, 0))
    tab_spec = pl.BlockSpec((tm, HEAD_DIM), lambda g, t: (t, 0))
    out_spec = pl.BlockSpec((tm, wq), lambda g, t: (g * n_t + t, 0))
    out_sds = jax.ShapeDtypeStruct((tokens, wq), F32)
    w_spec = pl.BlockSpec((d, 4 * wq), lambda g, t: (0, 0), pipeline_mode=pl.Buffered(1))
    return pl.pallas_call(
        _qkvu_kernel,
        grid=(n_g, n_t),
        in_specs=[x_spec, mod_spec(0), mod_spec(1), row_spec, w_spec,
                  head_spec, head_spec, tab_spec, tab_spec],
        out_specs=[out_spec] * 4,
        out_shape=[out_sds] * 4,
        scratch_shapes=[pltpu.VMEM((tm, d), BF16)],
        compiler_params=_params("arbitrary", "arbitrary"),
        name="qkvu_proj",
    )(x3, mod, mod, g_norm1.reshape(1, d), w_in_bf, g_q.reshape(1, HEAD_DIM),
      g_k.reshape(1, HEAD_DIM), cos, sin_signed)


def _topk_select(gate, valid, axis):
    n = gate.shape[axis]
    g = jnp.where(valid, gate, NEG_INF)
    idx = lax.broadcasted_iota(jnp.int32, gate.shape, axis)
    rank = jnp.zeros(gate.shape, jnp.int32)
    for m in range(n):
        gm = g[m:m + 1, :] if axis == 0 else g[:, m:m + 1]
        beats = (gm > g) | ((gm == g) & (m < idx))
        rank = rank + beats.astype(jnp.int32)
    return valid & (rank < MOBA_TOPK) & (jnp.abs(g) < float("inf"))


def _moba_prompt_kernel(q_ref, k_ref, v_ref, o_ref, kb_scr, vt_scr, kmean_scr, bias_scr, tri_scr, s_scr,
                        *, nb):
    blk = MOBA_BLOCK

    def prepare():
        for n in range(nb):
            kn = k_ref[n * blk:(n + 1) * blk, :]
            kb_scr[n * blk:(n + 1) * blk, :] = kn.astype(BF16)
            kmean_scr[n:n + 1, :] = jnp.mean(kn, axis=0, keepdims=True)
            vt_scr[n] = v_ref[n * blk:(n + 1) * blk, :].T.astype(BF16)
        kb_scr[nb * blk:, :] = jnp.zeros((kb_scr.shape[0] - nb * blk, HEAD_DIM), BF16)
        kmean = kmean_scr[...]
        for qb_i in range(nb):
            gate = lax.dot_general(kmean, q_ref[qb_i * blk:(qb_i + 1) * blk, :],
                                   (((1,), (1,)), ((), ())),
                                   precision=lax.Precision.HIGHEST, preferred_element_type=F32)
            n_idx = lax.broadcasted_iota(jnp.int32, gate.shape, 0)
            sel = _topk_select(gate, n_idx < qb_i, axis=0)
            bias_scr[qb_i, 0:nb, :] = jnp.where(sel | (n_idx == qb_i), 0.0, NEG_INF)
            bias_scr[qb_i, nb:, :] = jnp.full((bias_scr.shape[1] - nb, blk), NEG_INF, F32)
        kpos = lax.broadcasted_iota(jnp.int32, (blk, blk), 0)
        qpos = lax.broadcasted_iota(jnp.int32, (blk, blk), 1)
        tri_scr[0] = jnp.zeros((blk, blk), F32)
        tri_scr[1] = jnp.where(kpos <= qpos, 0.0, NEG_INF)

    prepare()

    exp2_scale = HEAD_DIM ** -0.5 * _LOG2_E
    width = _TILE_GROUP

    def fold(x, op):
        return op(x.reshape(blk // 8, 8, blk), axis=0)

    def queries(qi):
        return q_ref[pl.ds(pl.multiple_of(qi * blk, blk), blk), :].astype(BF16)

    def score_group(qi, qb, g, slot):
        j0 = g * width
        keys = kb_scr[pl.ds(pl.multiple_of(j0 * blk, blk), width * blk), :]
        s_all = lax.dot_general(keys, qb, (((1,), (1,)), ((), ())), preferred_element_type=F32)
        m8 = jnp.full((8, blk), NEG_INF, F32)
        for t in range(width):
            j = j0 + t
            s = (s_all[t * blk:(t + 1) * blk, :] + bias_scr[qi, pl.ds(j, 1), :]
                 + tri_scr[jnp.asarray(j == qi, jnp.int32)])
            s_scr[slot, t] = s
            m8 = jnp.maximum(m8, fold(s, jnp.max))
        return m8

    def absorb_group(g, slot, m_run, l8, acc, m8):
        m_new = jnp.maximum(m_run, jnp.max(m8, axis=0, keepdims=True))
        alpha = jnp.exp2((m_run - m_new) * exp2_scale)
        l8 = l8 * alpha
        acc = acc * alpha
        for t in range(width):
            p = jnp.exp2((s_scr[slot, t] - m_new) * exp2_scale)
            l8 = l8 + fold(p, jnp.sum)
            acc = acc + jnp.dot(vt_scr[jnp.minimum(g * width + t, nb - 1)], p.astype(BF16),
                                preferred_element_type=F32)
        return m_new, l8, acc

    def one_query_block(qi, state):
        m8_first, first_slot = state
        qb = queries(qi)
        n_groups = (qi + width) // width

        def body(g, carry):
            m_run, l8, acc, m8 = carry
            m_run, l8, acc = absorb_group(g, (first_slot + g) % 2, m_run, l8, acc, m8)
            return m_run, l8, acc, score_group(qi, qb, g + 1, (first_slot + g + 1) % 2)

        carry = (jnp.full((1, blk), _MASKED_MAX, F32), jnp.zeros((8, blk), F32),
                 jnp.zeros((HEAD_DIM, blk), F32), m8_first)
        m_run, l8, acc, m8 = lax.fori_loop(0, n_groups - 1, body, carry)
        last_slot = (first_slot + n_groups - 1) % 2
        _, l8, acc = absorb_group(n_groups - 1, last_slot, m_run, l8, acc, m8)
        nxt = jnp.minimum(qi + 1, nb - 1)
        m8_next = score_group(nxt, queries(nxt), 0, 1 - last_slot)
        l = jnp.sum(l8, axis=0, keepdims=True)
        o_ref[pl.ds(pl.multiple_of(qi * blk, blk), blk), :] = (acc / l).T.astype(o_ref.dtype)
        return m8_next, 1 - last_slot

    lax.fori_loop(0, nb, one_query_block, (score_group(0, queries(0), 0, 0), jnp.int32(0)))


def _moba_prompt(q, k, v, n_seq, seq):
    nb = seq // MOBA_BLOCK
    nb_pad = -(-nb // _TILE_GROUP) * _TILE_GROUP + _TILE_GROUP
    blk = MOBA_BLOCK
    head_spec = pl.BlockSpec((seq, HEAD_DIM), lambda b, h: (b, h))
    return pl.pallas_call(
        functools.partial(_moba_prompt_kernel, nb=nb),
        grid=(n_seq, N_HEADS),
        in_specs=[head_spec, head_spec, head_spec],
        out_specs=head_spec,
        out_shape=jax.ShapeDtypeStruct(q.shape, BF16),
        scratch_shapes=[pltpu.VMEM((nb_pad * blk, HEAD_DIM), BF16),
                        pltpu.VMEM((nb, HEAD_DIM, blk), BF16),
                        pltpu.VMEM((nb, HEAD_DIM), F32),
                        pltpu.VMEM((nb, nb_pad, blk), F32),
                        pltpu.VMEM((2, blk, blk), F32),
                        pltpu.VMEM((2, _TILE_GROUP, blk, blk), F32)],
        compiler_params=_params("arbitrary", "arbitrary"),
        name="moba_prompt",
    )(q, k, v)


def _page_copy_groups(cache_hbm, pt_ref, buf, sem, slot, b, first_page, hp, n_pages, ppb, wanted):
    if len(buf.shape) == 4:
        return [(None, [pltpu.make_async_copy(cache_hbm.at[pt_ref[b * n_pages + first_page + p]],
                                              buf.at[slot, p], sem.at[slot])])
                for p in range(hp)]
    groups = []
    for p0 in range(0, hp, ppb):
        pgs = [pt_ref[b * n_pages + first_page + p0 + i] for i in range(ppb)]
        for h in range(N_HEADS):
            pred = None if wanted is None else wanted(b, (first_page + p0) // ppb, h)
            groups.append((pred, [pltpu.make_async_copy(cache_hbm.at[pgs[i], :, h, :],
                                                        buf.at[slot, p0 + i, h], sem.at[slot])
                                  for i in range(ppb)]))
    return groups


def _for_each_copy(groups, method):
    for pred, copies in groups:
        def run(copies=copies):
            for c in copies:
                getattr(c, method)()
        if pred is None:
            run()
        else:
            pl.when(pred)(run)


def _paged_slots(pt_ref, cache_hbm, buf, sem, hp, n_pages, ppb, compute, wanted=None):
    n_slots = buf.shape[0]
    b, step = pl.program_id(0), pl.program_id(1)
    n_b, n_steps = pl.num_programs(0), pl.num_programs(1)
    is_first = jnp.logical_and(b == 0, step == 0)
    is_last = jnp.logical_and(b == n_b - 1, step == n_steps - 1)
    wrap = step == n_steps - 1
    nxt_b = jnp.where(wrap, b + 1, b)
    nxt_step = jnp.where(wrap, 0, step + 1)

    def groups(bb, ss, slot):
        return _page_copy_groups(cache_hbm, pt_ref, buf, sem, slot, bb, (n_slots * ss + slot) * hp,
                                 hp, n_pages, ppb, wanted)

    @pl.when(is_first)
    def _():
        for slot in range(n_slots):
            _for_each_copy(groups(b, step, slot), "start")

    for slot in range(n_slots):
        _for_each_copy(groups(b, step, slot), "wait")
        compute(slot, (n_slots * step + slot) * hp)

        @pl.when(jnp.logical_not(is_last))
        def _():
            _for_each_copy(groups(nxt_b, nxt_step, slot), "start")


def _moba_sample_k_kernel(pt_ref, q_ref, knew_ref, vnew_ref, cache_hbm, p_ref, oown_ref, linv_ref,
                          used_ref, buf, sem, s_scr, kpart_scr, qbd_scr, *, hp, page, n_blocks):
    step = pl.program_id(1)
    n_steps = pl.num_programs(1)
    dq = q_ref.shape[0]
    rows = N_HEADS * dq
    scale = HEAD_DIM ** -0.5
    pages_per_block = MOBA_BLOCK // page
    n_pages = n_blocks * pages_per_block

    qh = [q_ref[:, h * HEAD_DIM:(h + 1) * HEAD_DIM] for h in range(N_HEADS)]
    qh_bf = [x.astype(BF16) for x in qh]

    @pl.when(step == 0)
    def _():
        kpart_scr[...] = jnp.zeros_like(kpart_scr)
        q_rep = jnp.concatenate([q_ref[...]] * N_HEADS
                                + [jnp.zeros((qbd_scr.shape[1] - rows, ATTN_WIDTH), F32)], axis=0)
        row_head = lax.broadcasted_iota(jnp.int32, q_rep.shape, 0) // dq
        lane_head = lax.broadcasted_iota(jnp.int32, q_rep.shape, 1) // HEAD_DIM
        qbd_scr[...] = jnp.where(row_head == lane_head, q_rep, 0.0).T.astype(BF16)

    def compute(slot, first_page):
        heads = [[buf[slot, p, pl.ds(h, page, stride=N_HEADS), :] for h in range(N_HEADS)]
                 for p in range(hp)]
        for p in range(hp):
            blk = (first_page + p) // pages_per_block
            for h in range(N_HEADS):
                kpart_scr[blk, h] += jnp.sum(heads[p][h].reshape(page // 8, 8, HEAD_DIM), axis=0)
        for p0 in range(0, hp, hp // 2):
            keys = jnp.concatenate([jnp.concatenate(hs, axis=1) for hs in heads[p0:p0 + hp // 2]],
                                   axis=0)
            s_t = jnp.dot(keys.astype(BF16), qbd_scr[...], preferred_element_type=F32) * scale
            for i in range(hp // 2):
                s_scr[first_page + p0 + i] = s_t[i * page:(i + 1) * page, :].T[:rows, :]

    _paged_slots(pt_ref, cache_hbm, buf, sem, hp, n_pages, pages_per_block, compute)

    @pl.when(step == n_steps - 1)
    def _():
        gates = []
        for h in range(N_HEADS):
            kmean_h = jnp.sum(kpart_scr[:, h], axis=1) * (1.0 / MOBA_BLOCK)
            gates.append(lax.dot_general(qh[h], kmean_h, (((1,), (1,)), ((), ())),
                                         precision=lax.Precision.HIGHEST,
                                         preferred_element_type=F32))
        gate = jnp.concatenate(gates, axis=0)
        sel = _topk_select(gate, jnp.ones(gate.shape, jnp.bool_), axis=1)
        bias = jnp.where(sel, 0.0, NEG_INF)
        used_ref[0] = jnp.concatenate(
            [jnp.max(sel[h * dq:(h + 1) * dq, :].astype(jnp.int32), axis=0, keepdims=True)
             for h in range(N_HEADS)], axis=0)

        s_own = []
        for h in range(N_HEADS):
            kn = knew_ref[:, h * HEAD_DIM:(h + 1) * HEAD_DIM].astype(BF16)
            s_own.append(lax.dot_general(qh_bf[h], kn, (((1,), (1,)), ((), ())),
                                         preferred_element_type=F32) * scale)
        s_own = jnp.concatenate(s_own, axis=0)
        row = lax.broadcasted_iota(jnp.int32, s_own.shape, 0) % dq
        col = lax.broadcasted_iota(jnp.int32, s_own.shape, 1)
        s_own = jnp.where(col <= row, s_own, NEG_INF)

        m_vec = jnp.full((rows, page), NEG_INF, F32)
        for g in range(n_pages):
            n = g // pages_per_block
            m_vec = jnp.maximum(m_vec, s_scr[g] + bias[:, n:n + 1])
        m = jnp.maximum(jnp.max(m_vec, axis=1, keepdims=True), jnp.max(s_own, axis=1, keepdims=True))
        p_own = jnp.exp(s_own - m)
        l_vec = jnp.zeros((rows, page), F32)
        for g in range(n_pages):
            n = g // pages_per_block
            pb = jnp.exp(s_scr[g] + bias[:, n:n + 1] - m)
            l_vec = l_vec + pb
            p_ref[0, :, g * page:(g + 1) * page] = pb
        l = jnp.sum(l_vec, axis=1, keepdims=True) + jnp.sum(p_own, axis=1, keepdims=True)
        linv_ref[0] = jnp.broadcast_to(1.0 / l, (rows, HEAD_DIM))
        for h in range(N_HEADS):
            vn = vnew_ref[:, h * HEAD_DIM:(h + 1) * HEAD_DIM].astype(BF16)
            oown_ref[0, h * dq:(h + 1) * dq, :] = jnp.dot(
                p_own[h * dq:(h + 1) * dq, :].astype(BF16), vn, preferred_element_type=F32)


def _moba_sample_v_kernel(pt_ref, used_ref, p_ref, oown_ref, linv_ref, cache_hbm, o_ref, buf, sem,
                          acc_scr, *, hp, page, n_pages):
    step = pl.program_id(1)
    n_steps = pl.num_programs(1)
    dq = o_ref.shape[0]
    ppb = MOBA_BLOCK // page
    n_blocks = n_pages // ppb

    @pl.when(jnp.logical_and(pl.program_id(0) == 0, step == 0))
    def _():
        buf[...] = jnp.zeros_like(buf)

    def wanted(b, blk, h):
        return used_ref[(b * N_HEADS + h) * n_blocks + blk] != 0

    @pl.when(step == 0)
    def _():
        acc_scr[...] = oown_ref[0]

    def compute(slot, first_page):
        for h in range(N_HEADS):
            tot = jnp.zeros((dq, HEAD_DIM), F32)
            for p in range(hp):
                col0 = (slot * hp + p) * page
                ph = p_ref[0, h * dq:(h + 1) * dq, col0:col0 + page].astype(BF16)
                tot = tot + jnp.dot(ph, buf[slot, p, h].astype(BF16), preferred_element_type=F32)
            acc_scr[h * dq:(h + 1) * dq, :] += tot

    _paged_slots(pt_ref, cache_hbm, buf, sem, hp, n_pages, ppb, compute, wanted)

    @pl.when(step == n_steps - 1)
    def _():
        for h in range(N_HEADS):
            rows_h = slice(h * dq, (h + 1) * dq)
            o_ref[:, h * HEAD_DIM:(h + 1) * HEAD_DIM] = acc_scr[rows_h, :] * linv_ref[0, rows_h, :]


def _moba_sample(q, k_new, v_new, cache_k, cache_v, page_table, hp=4, n_slots=4):
    dec_b, n_pages = page_table.shape
    page = cache_k.shape[1]
    dq = q.shape[0] // dec_b
    past = n_pages * page
    n_blocks = past // MOBA_BLOCK
    while n_pages % (n_slots * hp):
        n_slots //= 2
    n_steps = n_pages // (n_slots * hp)
    rows = N_HEADS * dq
    pt = page_table.reshape(-1)

    new_spec = pl.BlockSpec((dq, ATTN_WIDTH), lambda b, s, pt_ref: (b, 0))
    any_spec = pl.BlockSpec(memory_space=pl.ANY)
    stat_spec = pl.BlockSpec((1, rows, HEAD_DIM), lambda b, s, pt_ref: (b, 0, 0))
    stat_sds = jax.ShapeDtypeStruct((dec_b, rows, HEAD_DIM), F32)
    assert hp % (MOBA_BLOCK // page) == 0, "a page half must hold whole key blocks"
    page_buf = pltpu.VMEM((n_slots, hp, N_HEADS, page, HEAD_DIM), F32)
    page_rows_buf = pltpu.VMEM((n_slots, hp, page * N_HEADS, HEAD_DIM), F32)
    probs, o_own, l_inv, used = pl.pallas_call(
        functools.partial(_moba_sample_k_kernel, hp=hp, page=page, n_blocks=n_blocks),
        grid_spec=pltpu.PrefetchScalarGridSpec(
            num_scalar_prefetch=1,
            grid=(dec_b, n_steps),
            in_specs=[new_spec, new_spec, new_spec, any_spec],
            out_specs=[pl.BlockSpec((1, rows, past), lambda b, s, pt_ref: (b, 0, 0)),
                       stat_spec, stat_spec,
                       pl.BlockSpec((1, N_HEADS, n_blocks), lambda b, s, pt_ref: (b, 0, 0))],
            scratch_shapes=[page_rows_buf, pltpu.SemaphoreType.DMA((n_slots,)),
                            pltpu.VMEM((n_pages, rows, page), F32),
                            pltpu.VMEM((n_blocks, N_HEADS, 8, HEAD_DIM), F32),
                            pltpu.VMEM((ATTN_WIDTH, max(rows, HEAD_DIM)), BF16)]),
        out_shape=[jax.ShapeDtypeStruct((dec_b, rows, past), F32), stat_sds, stat_sds,
                   jax.ShapeDtypeStruct((dec_b, N_HEADS, n_blocks), jnp.int32)],
        compiler_params=_params("arbitrary", "arbitrary"),
        name="moba_sample_k",
    )(pt, q, k_new, v_new, cache_k.reshape(cache_k.shape[0], page * N_HEADS, HEAD_DIM))

    return pl.pallas_call(
        functools.partial(_moba_sample_v_kernel, hp=hp, page=page, n_pages=n_pages),
        grid_spec=pltpu.PrefetchScalarGridSpec(
            num_scalar_prefetch=2,
            grid=(dec_b, n_steps),
            in_specs=[pl.BlockSpec((1, rows, n_slots * hp * page), lambda b, s, *_: (b, 0, s)),
                      pl.BlockSpec((1, rows, HEAD_DIM), lambda b, s, *_: (b, 0, 0)),
                      pl.BlockSpec((1, rows, HEAD_DIM), lambda b, s, *_: (b, 0, 0)),
                      any_spec],
            out_specs=pl.BlockSpec((dq, ATTN_WIDTH), lambda b, s, *_: (b, 0)),
            scratch_shapes=[page_buf, pltpu.SemaphoreType.DMA((n_slots,)),
                            pltpu.VMEM((rows, HEAD_DIM), F32)]),
        out_shape=jax.ShapeDtypeStruct(q.shape, F32),
        compiler_params=_params("arbitrary", "arbitrary"),
        name="moba_sample_v",
    )(pt, used.reshape(-1), probs, o_own, l_inv, cache_v)


def _pool_kernel(u_ref, halo_ref, w_ref, ls_ref, o_ref, ext_scr, *, pos0):
    t = pl.program_id(1)
    gb, r, width = u_ref.shape
    grp = width // len(POOL_WINDOWS)

    @pl.when(t == 0)
    def _():
        ext_scr[:, 0:POOL_HALO, :] = halo_ref[...]

    @pl.when(t > 0)
    def _():
        ext_scr[:, 0:POOL_HALO, :] = ext_scr[:, r:r + POOL_HALO, :]

    ext_scr[:, POOL_HALO:POOL_HALO + r, :] = u_ref[...]

    pos = pos0 + t * r + lax.broadcasted_iota(jnp.int32, (1, r, grp), 1)
    for g, w in enumerate(POOL_WINDOWS):
        cols = slice(g * grp, (g + 1) * grp)
        tot = ext_scr[:, POOL_HALO:POOL_HALO + r, cols]
        for d in range(1, w):
            tot = tot + ext_scr[:, POOL_HALO - d:POOL_HALO - d + r, cols]
        cnt = jnp.minimum(w, pos + 1).astype(F32)
        y = tot / cnt - u_ref[:, :, cols]
        y = jnp.dot(y.reshape(gb * r, grp).astype(BF16), w_ref[g], preferred_element_type=F32)
        o_ref[:, cols] = (y * ls_ref[:, cols]).astype(o_ref.dtype)


def _pool(u, halo, w_pool_bf, ls_pool, n_seq, seq, gb, r, pos0):
    width = u.shape[1]
    n_g, n_t = n_seq // gb, seq // r
    u3 = u.reshape(n_seq, seq, width)
    return pl.pallas_call(
        functools.partial(_pool_kernel, pos0=pos0),
        grid=(n_g, n_t),
        in_specs=[pl.BlockSpec((gb, r, width), lambda g, t: (g, t, 0)),
                  pl.BlockSpec((gb, POOL_HALO, width), lambda g, t: (g, 0, 0)),
                  pl.BlockSpec(w_pool_bf.shape, lambda g, t: (0, 0, 0)),
                  pl.BlockSpec((1, width), lambda g, t: (0, 0))],
        out_specs=pl.BlockSpec((gb * r, width), lambda g, t: (g * n_t + t, 0)),
        out_shape=jax.ShapeDtypeStruct(u.shape, BF16),
        scratch_shapes=[pltpu.VMEM((gb, r + POOL_HALO, width), F32)],
        compiler_params=_params("arbitrary", "arbitrary"),
        name="pool_mixer",
    )(u3, halo, w_pool_bf, ls_pool.reshape(1, width))


def _bf16_tiles(w_refs, copy_refs):
    tiles = []
    for i, w_ref in enumerate(w_refs):
        w = w_ref[...]
        if w.dtype != BF16:
            w = w.astype(BF16)
            copy_refs[i][...] = w
        tiles.append(w)
    return tiles


def _merge_kernel(x_ref, sh_ref, sc_ref, gt_ref, g1_ref, attn_ref, pool_ref,
                  wga_ref, wgb_ref, woa_ref, wob_ref, wout_ref, o_ref, *rest):
    *copy_refs, h_scr = rest
    c = pl.program_id(2)

    @pl.when(c == 0)
    def _():
        _mod_norm_store(x_ref, g1_ref, sc_ref, sh_ref, h_scr)
        o_ref[...] = jnp.zeros_like(o_ref)

    wga, wgb, woa, wob, wout = _bf16_tiles((wga_ref, wgb_ref, woa_ref, wob_ref, wout_ref), copy_refs)
    h = h_scr[...]
    g_a = jax.nn.sigmoid(jnp.dot(h, wga, preferred_element_type=F32))
    g_b = jax.nn.sigmoid(jnp.dot(h, wgb, preferred_element_type=F32))
    y_a = jnp.dot(attn_ref[...].astype(BF16), woa, preferred_element_type=F32)
    y_b = jnp.dot(pool_ref[...].astype(BF16), wob, preferred_element_type=F32)
    mix_in = (g_a * y_a + g_b * y_b).astype(BF16)
    o_ref[...] += jnp.dot(mix_in, wout, preferred_element_type=F32).reshape(o_ref.shape)

    @pl.when(c == pl.num_programs(2) - 1)
    def _():
        o_ref[...] = x_ref[...] + gt_ref[...] * o_ref[...]


def _column_weight_specs(weights, tn, n_c):
    in_specs, copy_specs, copy_shapes = [], [], []
    for w, axis, first in weights:
        if axis == 1:
            in_specs.append(pl.BlockSpec((w.shape[0], tn), lambda g, t, c, first=first: (0, first + c)))
            copy_specs.append(pl.BlockSpec((w.shape[0], tn), lambda g, t, c: (0, c)))
            copy_shapes.append(jax.ShapeDtypeStruct((w.shape[0], n_c * tn), BF16))
        else:
            in_specs.append(pl.BlockSpec((tn, w.shape[1]), lambda g, t, c, first=first: (first + c, 0)))
            copy_specs.append(pl.BlockSpec((tn, w.shape[1]), lambda g, t, c: (c, 0)))
            copy_shapes.append(jax.ShapeDtypeStruct((n_c * tn, w.shape[1]), BF16))
    if all(w.dtype == BF16 for w, _, _ in weights):
        return in_specs, [], []
    return in_specs, copy_specs, copy_shapes


def _merge(x3, mod, g_norm1, attn, pooled, weights, gb, r, tn):
    n_g, n_t, d = _group_specs(x3, gb, r)
    tm = gb * r
    aw = attn.shape[1]
    pw = pooled.shape[1]
    n_c = d // tn
    w_specs, copy_specs, copy_shapes = _column_weight_specs(weights, tn, n_c)
    if copy_specs:
        assert n_g * n_t == 1, "weight copies are written by a single token tile"
    x_spec = pl.BlockSpec((gb, r, d), lambda g, t, c: (g, t, 0))
    mod_spec = lambda k: pl.BlockSpec((gb, 1, d), lambda g, t, c: (g, 0, k))
    out, *copies = pl.pallas_call(
        _merge_kernel,
        grid=(n_g, n_t, n_c),
        in_specs=[x_spec, mod_spec(0), mod_spec(1), mod_spec(2),
                  pl.BlockSpec((1, d), lambda g, t, c: (0, 0)),
                  pl.BlockSpec((tm, aw), lambda g, t, c: (g * n_t + t, 0)),
                  pl.BlockSpec((tm, pw), lambda g, t, c: (g * n_t + t, 0))] + w_specs,
        out_specs=[x_spec] + copy_specs,
        out_shape=[jax.ShapeDtypeStruct(x3.shape, F32)] + copy_shapes,
        scratch_shapes=[pltpu.VMEM((tm, d), BF16)],
        compiler_params=_params("arbitrary", "arbitrary", "arbitrary"),
        name="merge_out",
    )(x3, mod, mod, mod, g_norm1.reshape(1, d), attn, pooled, *[w for w, _, _ in weights])
    return out, copies


def _ffn_kernel(x_ref, sh_ref, sc_ref, gt_ref, g2_ref, wa_ref, wb_ref, wo_ref, o_ref, *rest):
    *copy_refs, h_scr = rest
    c = pl.program_id(2)

    @pl.when(c == 0)
    def _():
        _mod_norm_store(x_ref, g2_ref, sc_ref, sh_ref, h_scr)
        o_ref[...] = jnp.zeros_like(o_ref)

    wa, wb, wo = _bf16_tiles((wa_ref, wb_ref, wo_ref), copy_refs)
    h = h_scr[...]
    a = jnp.dot(h, wa, preferred_element_type=F32)
    b = jnp.dot(h, wb, preferred_element_type=F32)
    hid = (jax.nn.silu(a) * b).astype(BF16)
    o_ref[...] += jnp.dot(hid, wo, preferred_element_type=F32).reshape(o_ref.shape)

    @pl.when(c == pl.num_programs(2) - 1)
    def _():
        o_ref[...] = x_ref[...] + gt_ref[...] * o_ref[...]


def _ffn(x3, mod, g_norm2, weights, hidden, gb, r, tn):
    n_g, n_t, d = _group_specs(x3, gb, r)
    tm = gb * r
    n_c = hidden // tn
    w_specs, copy_specs, copy_shapes = _column_weight_specs(weights, tn, n_c)
    if copy_specs:
        assert n_g * n_t == 1, "weight copies are written by a single token tile"
    x_spec = pl.BlockSpec((gb, r, d), lambda g, t, c: (g, t, 0))
    mod_spec = lambda k: pl.BlockSpec((gb, 1, d), lambda g, t, c: (g, 0, k))
    out, *copies = pl.pallas_call(
        _ffn_kernel,
        grid=(n_g, n_t, n_c),
        in_specs=[x_spec, mod_spec(3), mod_spec(4), mod_spec(5),
                  pl.BlockSpec((1, d), lambda g, t, c: (0, 0))] + w_specs,
        out_specs=[x_spec] + copy_specs,
        out_shape=[jax.ShapeDtypeStruct(x3.shape, F32)] + copy_shapes,
        scratch_shapes=[pltpu.VMEM((tm, d), BF16)],
        compiler_params=_params("arbitrary", "arbitrary", "arbitrary"),
        name="ffn_swiglu",
    )(x3, mod, mod, mod, g_norm2.reshape(1, d), *[w for w, _, _ in weights])
    return out, copies


def _rope_tables(pos):
    half = HEAD_DIM // 2
    inv = ROPE_THETA ** (-jnp.arange(half, dtype=F32) / half)
    ang = pos.astype(F32)[:, None] * inv[None, :]
    cos, sin = jnp.cos(ang), jnp.sin(ang)
    return jnp.concatenate([cos, cos], axis=-1), jnp.concatenate([-sin, sin], axis=-1)


def _pick_rows(seq, target=512):
    r = min(seq, target)
    while seq % r:
        r //= 2
    return r


def kernel(x_prompt, x_sample, c_prompt, c_sample, cache_k, cache_v, state_pool, page_table, w_ada, b_ada, g_norm1, g_norm2, w_in, g_qnorm, g_knorm, w_pool, ls_pool, w_o_attn, w_o_pool, w_out, w_ffn_in, w_ffn_out):
    n_p, seq, d = x_prompt.shape
    n_s, dq, _ = x_sample.shape
    past = page_table.shape[1] * cache_k.shape[1]
    pool_w = state_pool.shape[2]
    n_state = state_pool.shape[1]

    w_qkvu_bf = w_in[:, :3 * ATTN_WIDTH + pool_w].astype(BF16)
    w_pool_bf = w_pool.astype(BF16)
    hidden = w_ffn_out.shape[0]

    n_c = n_p + n_s
    c_all = jnp.concatenate([c_prompt, c_sample], axis=0)
    c_all = jnp.pad(c_all, ((0, (-n_c) % 8), (0, 0)))
    mod = _ada(c_all, w_ada, b_ada)
    mod_p = mod[:n_p].reshape(n_p, 1, 6 * d)
    mod_s = mod[n_p:n_c].reshape(n_s, 1, 6 * d)

    cos_s, sin_s = _rope_tables(past + jnp.arange(dq))
    cos_s, sin_s = jnp.tile(cos_s, (n_s, 1)), jnp.tile(sin_s, (n_s, 1))
    q_s, k_s, v_s, u_s = _qkvu(x_sample, mod_s, g_norm1, w_qkvu_bf, g_qnorm, g_knorm, cos_s, sin_s, n_s, dq)
    attn_s = _moba_sample(q_s, k_s, v_s, cache_k, cache_v, page_table)
    halo_s = jnp.pad(state_pool, ((0, 0), (POOL_HALO - n_state, 0), (0, 0)))
    pooled_s = _pool(u_s, halo_s, w_pool_bf, ls_pool, n_s, dq, n_s, dq, past)
    gate0 = (w_in.shape[1] - 2 * d) // _TN_SAMPLE
    x1_s, merge_bf = _merge(
        x_sample, mod_s, g_norm1, attn_s, pooled_s,
        ((w_in, 1, gate0), (w_in, 1, gate0 + d // _TN_SAMPLE), (w_o_attn, 1, 0), (w_o_pool, 1, 0),
         (w_out, 0, 0)), n_s, dq, _TN_SAMPLE)
    y_s, ffn_bf = _ffn(
        x1_s, mod_s, g_norm2,
        ((w_ffn_in, 1, 0), (w_ffn_in, 1, hidden // _TN_PROMPT), (w_ffn_out, 0, 0)),
        hidden, n_s, dq, _TN_PROMPT)

    r_p = _pick_rows(seq)
    cos_p, sin_p = _rope_tables(jnp.arange(seq))
    q_p, k_p, v_p, u_p = _qkvu(x_prompt, mod_p, g_norm1, w_qkvu_bf, g_qnorm, g_knorm, cos_p, sin_p, 1, r_p)
    attn_p = _moba_prompt(q_p, k_p, v_p, n_p, seq)
    pooled_p = _pool(u_p, jnp.zeros((n_p, POOL_HALO, pool_w), F32), w_pool_bf, ls_pool,
                     n_p, seq, 1, r_p, 0)
    x1_p, _ = _merge(x_prompt, mod_p, g_norm1, attn_p, pooled_p,
                     tuple((w, axis, 0) for w, axis in zip(merge_bf, (1, 1, 1, 1, 0))), 1, r_p, _TN_PROMPT)
    y_p, _ = _ffn(x1_p, mod_p, g_norm2, tuple((w, axis, 0) for w, axis in zip(ffn_bf, (1, 1, 0))),
                  hidden, 1, r_p, _TN_PROMPT)

    k_prompt = k_p.reshape(n_p, seq, N_HEADS, HEAD_DIM)
    v_prompt = v_p.reshape(n_p, seq, N_HEADS, HEAD_DIM)
    pool_prompt = u_p.reshape(n_p, seq, pool_w)[:, seq - n_state:]
    k_sample = k_s.reshape(n_s, dq, N_HEADS, HEAD_DIM)
    v_sample = v_s.reshape(n_s, dq, N_HEADS, HEAD_DIM)
    pool_sample = jnp.concatenate([state_pool, u_s.reshape(n_s, dq, pool_w)], axis=1)[:, -n_state:]
    return (y_p, y_s, k_prompt, v_prompt, pool_prompt, k_sample, v_sample, pool_sample)
```

```python
import functools

import jax
import jax.numpy as jnp
from jax import lax
from jax.experimental import pallas as pl
from jax.experimental.pallas import tpu as pltpu

N_HEADS = 8
HEAD_DIM = 128
ATTN_WIDTH = N_HEADS * HEAD_DIM
MOBA_BLOCK = 256
MOBA_TOPK = 3
ROPE_THETA = 10000.0
POOL_WINDOWS = (2, 4, 8, 16)
POOL_HALO = 16
RMS_EPS = 1e-6
_TILE_GROUP = 4
_MASKED_MAX = -1e30
_TN_PROMPT = 512
_TN_SAMPLE = 256

F32 = jnp.float32
BF16 = jnp.bfloat16
NEG_INF = float("-inf")
_LOG2_E = 1.4426950408889634

_VMEM_LIMIT = 52 * 1024 * 1024


def _params(*sem):
    return pltpu.CompilerParams(dimension_semantics=sem, vmem_limit_bytes=_VMEM_LIMIT)


_NORM_ROWS = 16


def _mod_norm_store(x_ref, g_ref, sc_ref, sh_ref, h_scr):
    gb, r, d = x_ref.shape
    g = g_ref[...]
    if r >= _NORM_ROWS:
        per_group = r // _NORM_ROWS
        chunks = [(gi, 1, c * _NORM_ROWS, _NORM_ROWS) for gi in range(gb) for c in range(per_group)]
    else:
        n_g = _NORM_ROWS // r
        chunks = [(gi, n_g, 0, r) for gi in range(0, gb, n_g)]
    gains = {}
    for gi, n_g, r0, nr in chunks:
        if gi not in gains:
            gains[gi] = g * (1.0 + sc_ref[gi:gi + n_g])
        x = x_ref[gi:gi + n_g, r0:r0 + nr, :]
        ms = jnp.mean(x * x, axis=-1, keepdims=True)
        h = x * lax.rsqrt(ms + RMS_EPS) * gains[gi] + sh_ref[gi:gi + n_g]
        row0 = gi * r + r0
        h_scr[row0:row0 + n_g * nr, :] = h.reshape(n_g * nr, d).astype(BF16)


def _ada_kernel(c_ref, w_ref, b_ref, o_ref):
    o_ref[...] = jnp.dot(c_ref[...].astype(BF16), w_ref[...].astype(BF16),
                         preferred_element_type=F32) + b_ref[...]


def _ada(c_all, w_ada, b_ada, tn=1024):
    m, d = c_all.shape
    n = w_ada.shape[1]
    return pl.pallas_call(
        _ada_kernel,
        grid=(n // tn,),
        in_specs=[pl.BlockSpec((m, d), lambda j: (0, 0)),
                  pl.BlockSpec((d, tn), lambda j: (0, j)),
                  pl.BlockSpec((1, tn), lambda j: (0, j))],
        out_specs=pl.BlockSpec((m, tn), lambda j: (0, j)),
        out_shape=jax.ShapeDtypeStruct((m, n), F32),
        compiler_params=_params("arbitrary"),
        name="ada_mod",
    )(c_all, w_ada, b_ada.reshape(1, n))


def _head_norm_rope(th, g, cos, sin_signed):
    ms = jnp.mean(th * th, axis=-1, keepdims=True)
    y = th * lax.rsqrt(ms + RMS_EPS) * g
    return y * cos + pltpu.roll(y, HEAD_DIM // 2, 1) * sin_signed


_PROJ_CHUNK = 2 * HEAD_DIM


def _qkvu_kernel(x_ref, sh_ref, sc_ref, g1_ref, w_ref, gq_ref, gk_ref, cos_ref, sin_ref,
                 q_ref, k_ref, v_ref, u_ref, h_ref):
    _mod_norm_store(x_ref, g1_ref, sc_ref, sh_ref, h_ref)

    width = q_ref.shape[1]
    outs = ((q_ref, gq_ref), (k_ref, gk_ref), (v_ref, None), (u_ref, None))
    for i, (o_ref, g_ref) in enumerate(outs):
        for c in range(0, width, _PROJ_CHUNK):
            col = i * width + c
            acc = jnp.dot(h_ref[...], w_ref[:, col:col + _PROJ_CHUNK], preferred_element_type=F32)
            if g_ref is None:
                o_ref[:, c:c + _PROJ_CHUNK] = acc
            else:
                for hh in range(0, _PROJ_CHUNK, HEAD_DIM):
                    o_ref[:, c + hh:c + hh + HEAD_DIM] = _head_norm_rope(
                        acc[:, hh:hh + HEAD_DIM], g_ref[...], cos_ref[...], sin_ref[...])


def _group_specs(x3, gb, r):
    g_total, s, d = x3.shape
    n_g, n_t = g_total // gb, s // r
    return n_g, n_t, d


def _qkvu(x3, mod, g_norm1, w_in_bf, g_q, g_k, cos, sin_signed, gb, r):
    n_g, n_t, d = _group_specs(x3, gb, r)
    tm = gb * r
    tokens = x3.shape[0] * x3.shape[1]
    wq = ATTN_WIDTH
    x_spec = pl.BlockSpec((gb, r, d), lambda g, t: (g, t, 0))
    mod_spec = lambda k: pl.BlockSpec((gb, 1, d), lambda g, t: (g, 0, k))
    row_spec = pl.BlockSpec((1, d), lambda g, t: (0, 0))
    head_spec = pl.BlockSpec((1, HEAD_DIM), lambda g, t: (0, 0))
    tab_spec = pl.BlockSpec((tm, HEAD_DIM), lambda g, t: (t, 0))
    out_spec = pl.BlockSpec((tm, wq), lambda g, t: (g * n_t + t, 0))
    out_sds = jax.ShapeDtypeStruct((tokens, wq), F32)
    w_spec = pl.BlockSpec((d, 4 * wq), lambda g, t: (0, 0), pipeline_mode=pl.Buffered(1))
    return pl.pallas_call(
        _qkvu_kernel,
        grid=(n_g, n_t),
        in_specs=[x_spec, mod_spec(0), mod_spec(1), row_spec, w_spec,
                  head_spec, head_spec, tab_spec, tab_spec],
        out_specs=[out_spec] * 4 + [pl.BlockSpec((tm, d), lambda g, t: (g * n_t + t, 0))],
        out_shape=[out_sds] * 4 + [jax.ShapeDtypeStruct((tokens, d), BF16)],
        compiler_params=_params("arbitrary", "arbitrary"),
        name="qkvu_proj",
    )(x3, mod, mod, g_norm1.reshape(1, d), w_in_bf, g_q.reshape(1, HEAD_DIM),
      g_k.reshape(1, HEAD_DIM), cos, sin_signed)


def _topk_select(gate, valid, axis):
    n = gate.shape[axis]
    g = jnp.where(valid, gate, NEG_INF)
    idx = lax.broadcasted_iota(jnp.int32, gate.shape, axis)
    rank = jnp.zeros(gate.shape, jnp.int32)
    for m in range(n):
        gm = g[m:m + 1, :] if axis == 0 else g[:, m:m + 1]
        beats = (gm > g) | ((gm == g) & (m < idx))
        rank = rank + beats.astype(jnp.int32)
    return valid & (rank < MOBA_TOPK) & (jnp.abs(g) < float("inf"))


def _moba_prompt_kernel(q_ref, k_ref, v_ref, o_ref, kb_scr, vt_scr, kmean_scr, bias_scr, tri_scr, s_scr,
                        *, nb):
    blk = MOBA_BLOCK

    def prepare():
        for n in range(nb):
            kn = k_ref[n * blk:(n + 1) * blk, :]
            kb_scr[n * blk:(n + 1) * blk, :] = kn.astype(BF16)
            kmean_scr[n:n + 1, :] = jnp.mean(kn, axis=0, keepdims=True)
            vt_scr[n] = v_ref[n * blk:(n + 1) * blk, :].T.astype(BF16)
        kb_scr[nb * blk:, :] = jnp.zeros((kb_scr.shape[0] - nb * blk, HEAD_DIM), BF16)
        kmean = kmean_scr[...]
        for qb_i in range(nb):
            gate = lax.dot_general(kmean, q_ref[qb_i * blk:(qb_i + 1) * blk, :],
                                   (((1,), (1,)), ((), ())),
                                   precision=lax.Precision.HIGHEST, preferred_element_type=F32)
            n_idx = lax.broadcasted_iota(jnp.int32, gate.shape, 0)
            sel = _topk_select(gate, n_idx < qb_i, axis=0)
            bias_scr[qb_i, 0:nb, :] = jnp.where(sel | (n_idx == qb_i), 0.0, NEG_INF)
            bias_scr[qb_i, nb:, :] = jnp.full((bias_scr.shape[1] - nb, blk), NEG_INF, F32)
        kpos = lax.broadcasted_iota(jnp.int32, (blk, blk), 0)
        qpos = lax.broadcasted_iota(jnp.int32, (blk, blk), 1)
        tri_scr[0] = jnp.zeros((blk, blk), F32)
        tri_scr[1] = jnp.where(kpos <= qpos, 0.0, NEG_INF)

    prepare()

    exp2_scale = HEAD_DIM ** -0.5 * _LOG2_E
    width = _TILE_GROUP

    def fold(x, op):
        return op(x.reshape(blk // 8, 8, blk), axis=0)

    def queries(qi):
        return q_ref[pl.ds(pl.multiple_of(qi * blk, blk), blk), :].astype(BF16)

    def score_group(qi, qb, g, slot):
        j0 = g * width
        keys = kb_scr[pl.ds(pl.multiple_of(j0 * blk, blk), width * blk), :]
        s_all = lax.dot_general(keys, qb, (((1,), (1,)), ((), ())), preferred_element_type=F32)
        m8 = jnp.full((8, blk), NEG_INF, F32)
        for t in range(width):
            j = j0 + t
            s = (s_all[t * blk:(t + 1) * blk, :] + bias_scr[qi, pl.ds(j, 1), :]
                 + tri_scr[jnp.asarray(j == qi, jnp.int32)])
            s_scr[slot, t] = s
            m8 = jnp.maximum(m8, fold(s, jnp.max))
        return m8

    def absorb_group(g, slot, m_run, l8, acc, m8):
        m_new = jnp.maximum(m_run, jnp.max(m8, axis=0, keepdims=True))
        alpha = jnp.exp2((m_run - m_new) * exp2_scale)
        l8 = l8 * alpha
        acc = acc * alpha
        for t in range(width):
            p = jnp.exp2((s_scr[slot, t] - m_new) * exp2_scale)
            l8 = l8 + fold(p, jnp.sum)
            acc = acc + jnp.dot(vt_scr[jnp.minimum(g * width + t, nb - 1)], p.astype(BF16),
                                preferred_element_type=F32)
        return m_new, l8, acc

    def one_query_block(qi, state):
        m8_first, first_slot = state
        qb = queries(qi)
        n_groups = (qi + width) // width

        def body(g, carry):
            m_run, l8, acc, m8 = carry
            m_run, l8, acc = absorb_group(g, (first_slot + g) % 2, m_run, l8, acc, m8)
            return m_run, l8, acc, score_group(qi, qb, g + 1, (first_slot + g + 1) % 2)

        carry = (jnp.full((1, blk), _MASKED_MAX, F32), jnp.zeros((8, blk), F32),
                 jnp.zeros((HEAD_DIM, blk), F32), m8_first)
        m_run, l8, acc, m8 = lax.fori_loop(0, n_groups - 1, body, carry)
        last_slot = (first_slot + n_groups - 1) % 2
        _, l8, acc = absorb_group(n_groups - 1, last_slot, m_run, l8, acc, m8)
        nxt = jnp.minimum(qi + 1, nb - 1)
        m8_next = score_group(nxt, queries(nxt), 0, 1 - last_slot)
        l = jnp.sum(l8, axis=0, keepdims=True)
        o_ref[pl.ds(pl.multiple_of(qi * blk, blk), blk), :] = (acc / l).T.astype(o_ref.dtype)
        return m8_next, 1 - last_slot

    lax.fori_loop(0, nb, one_query_block, (score_group(0, queries(0), 0, 0), jnp.int32(0)))


def _moba_prompt(q, k, v, n_seq, seq):
    nb = seq // MOBA_BLOCK
    nb_pad = -(-nb // _TILE_GROUP) * _TILE_GROUP + _TILE_GROUP
    blk = MOBA_BLOCK
    head_spec = pl.BlockSpec((seq, HEAD_DIM), lambda b, h: (b, h))
    return pl.pallas_call(
        functools.partial(_moba_prompt_kernel, nb=nb),
        grid=(n_seq, N_HEADS),
        in_specs=[head_spec, head_spec, head_spec],
        out_specs=head_spec,
        out_shape=jax.ShapeDtypeStruct(q.shape, BF16),
        scratch_shapes=[pltpu.VMEM((nb_pad * blk, HEAD_DIM), BF16),
                        pltpu.VMEM((nb, HEAD_DIM, blk), BF16),
                        pltpu.VMEM((nb, HEAD_DIM), F32),
                        pltpu.VMEM((nb, nb_pad, blk), F32),
                        pltpu.VMEM((2, blk, blk), F32),
                        pltpu.VMEM((2, _TILE_GROUP, blk, blk), F32)],
        compiler_params=_params("arbitrary", "arbitrary"),
        name="moba_prompt",
    )(q, k, v)


def _page_copy_groups(cache_hbm, pt_ref, buf, sem, slot, b, first_page, hp, n_pages, ppb, wanted):
    groups = []
    for p0 in range(0, hp, ppb):
        pgs = [pt_ref[b * n_pages + first_page + p0 + i] for i in range(ppb)]
        for h in range(N_HEADS):
            pred = None if wanted is None else wanted(b, (first_page + p0) // ppb, h)
            groups.append((pred, [pltpu.make_async_copy(cache_hbm.at[pgs[i], :, h, :],
                                                        buf.at[slot, p0 + i, h], sem.at[slot])
                                  for i in range(ppb)]))
    return groups


def _for_each_copy(groups, method):
    for pred, copies in groups:
        def run(copies=copies):
            for c in copies:
                getattr(c, method)()
        if pred is None:
            run()
        else:
            pl.when(pred)(run)


def _paged_slots(pt_ref, cache_hbm, buf, sem, hp, n_pages, ppb, compute, wanted=None):
    n_slots = buf.shape[0]
    b, step = pl.program_id(0), pl.program_id(1)
    n_b, n_steps = pl.num_programs(0), pl.num_programs(1)
    is_first = jnp.logical_and(b == 0, step == 0)
    is_last = jnp.logical_and(b == n_b - 1, step == n_steps - 1)
    wrap = step == n_steps - 1
    nxt_b = jnp.where(wrap, b + 1, b)
    nxt_step = jnp.where(wrap, 0, step + 1)

    def groups(bb, ss, slot):
        return _page_copy_groups(cache_hbm, pt_ref, buf, sem, slot, bb, (n_slots * ss + slot) * hp,
                                 hp, n_pages, ppb, wanted)

    @pl.when(is_first)
    def _():
        for slot in range(n_slots):
            _for_each_copy(groups(b, step, slot), "start")

    for slot in range(n_slots):
        _for_each_copy(groups(b, step, slot), "wait")
        compute(slot, (n_slots * step + slot) * hp)

        @pl.when(jnp.logical_not(is_last))
        def _():
            _for_each_copy(groups(nxt_b, nxt_step, slot), "start")


def _moba_sample_k_kernel(pt_ref, q_ref, knew_ref, vnew_ref, cache_hbm, p_ref, oown_ref, linv_ref,
                          used_ref, buf, sem, s_scr, kpart_scr, qbd_scr, *, hp, page, n_blocks):
    step = pl.program_id(1)
    n_steps = pl.num_programs(1)
    dq = q_ref.shape[0]
    rows = N_HEADS * dq
    scale = HEAD_DIM ** -0.5
    pages_per_block = MOBA_BLOCK // page
    n_pages = n_blocks * pages_per_block

    qh = [q_ref[:, h * HEAD_DIM:(h + 1) * HEAD_DIM] for h in range(N_HEADS)]
    qh_bf = [x.astype(BF16) for x in qh]

    @pl.when(step == 0)
    def _():
        kpart_scr[...] = jnp.zeros_like(kpart_scr)
        q_rep = jnp.concatenate([q_ref[...]] * N_HEADS
                                + [jnp.zeros((qbd_scr.shape[1] - rows, ATTN_WIDTH), F32)], axis=0)
        row_head = lax.broadcasted_iota(jnp.int32, q_rep.shape, 0) // dq
        lane_head = lax.broadcasted_iota(jnp.int32, q_rep.shape, 1) // HEAD_DIM
        qbd_scr[...] = jnp.where(row_head == lane_head, q_rep, 0.0).T.astype(BF16)

    def compute(slot, first_page):
        heads = [[buf[slot, p, h] for h in range(N_HEADS)] for p in range(hp)]
        for p in range(hp):
            blk = (first_page + p) // pages_per_block
            for h in range(N_HEADS):
                kpart_scr[blk, h] += jnp.sum(heads[p][h].reshape(page // 8, 8, HEAD_DIM), axis=0)
        for p0 in range(0, hp, hp // 2):
            keys = jnp.concatenate([jnp.concatenate(hs, axis=1) for hs in heads[p0:p0 + hp // 2]],
                                   axis=0)
            s_t = jnp.dot(keys.astype(BF16), qbd_scr[...], preferred_element_type=F32) * scale
            for i in range(hp // 2):
                s_scr[first_page + p0 + i] = s_t[i * page:(i + 1) * page, :].T[:rows, :]

    _paged_slots(pt_ref, cache_hbm, buf, sem, hp, n_pages, pages_per_block, compute)

    @pl.when(step == n_steps - 1)
    def _():
        gates = []
        for h in range(N_HEADS):
            kmean_h = jnp.sum(kpart_scr[:, h], axis=1) * (1.0 / MOBA_BLOCK)
            gates.append(lax.dot_general(qh[h], kmean_h, (((1,), (1,)), ((), ())),
                                         precision=lax.Precision.HIGHEST,
                                         preferred_element_type=F32))
        gate = jnp.concatenate(gates, axis=0)
        sel = _topk_select(gate, jnp.ones(gate.shape, jnp.bool_), axis=1)
        bias = jnp.where(sel, 0.0, NEG_INF)
        used_ref[0] = jnp.concatenate(
            [jnp.max(sel[h * dq:(h + 1) * dq, :].astype(jnp.int32), axis=0, keepdims=True)
             for h in range(N_HEADS)], axis=0)

        s_own = []
        for h in range(N_HEADS):
            kn = knew_ref[:, h * HEAD_DIM:(h + 1) * HEAD_DIM].astype(BF16)
            s_own.append(lax.dot_general(qh_bf[h], kn, (((1,), (1,)), ((), ())),
                                         preferred_element_type=F32) * scale)
        s_own = jnp.concatenate(s_own, axis=0)
        row = lax.broadcasted_iota(jnp.int32, s_own.shape, 0) % dq
        col = lax.broadcasted_iota(jnp.int32, s_own.shape, 1)
        s_own = jnp.where(col <= row, s_own, NEG_INF)

        m_vec = jnp.full((rows, page), NEG_INF, F32)
        for g in range(n_pages):
            n = g // pages_per_block
            m_vec = jnp.maximum(m_vec, s_scr[g] + bias[:, n:n + 1])
        m = jnp.maximum(jnp.max(m_vec, axis=1, keepdims=True), jnp.max(s_own, axis=1, keepdims=True))
        p_own = jnp.exp(s_own - m)
        l_vec = jnp.zeros((rows, page), F32)
        for g in range(n_pages):
            n = g // pages_per_block
            pb = jnp.exp(s_scr[g] + bias[:, n:n + 1] - m)
            l_vec = l_vec + pb
            p_ref[0, :, g * page:(g + 1) * page] = pb
        l = jnp.sum(l_vec, axis=1, keepdims=True) + jnp.sum(p_own, axis=1, keepdims=True)
        linv_ref[0] = jnp.broadcast_to(1.0 / l, (rows, HEAD_DIM))
        for h in range(N_HEADS):
            vn = vnew_ref[:, h * HEAD_DIM:(h + 1) * HEAD_DIM].astype(BF16)
            oown_ref[0, h * dq:(h + 1) * dq, :] = jnp.dot(
                p_own[h * dq:(h + 1) * dq, :].astype(BF16), vn, preferred_element_type=F32)


def _moba_sample_v_kernel(pt_ref, used_ref, p_ref, oown_ref, linv_ref, cache_hbm, o_ref, buf, sem,
                          acc_scr, *, hp, page, n_pages):
    step = pl.program_id(1)
    n_steps = pl.num_programs(1)
    dq = o_ref.shape[0]
    ppb = MOBA_BLOCK // page
    n_blocks = n_pages // ppb

    @pl.when(jnp.logical_and(pl.program_id(0) == 0, step == 0))
    def _():
        buf[...] = jnp.zeros_like(buf)

    def wanted(b, blk, h):
        return used_ref[(b * N_HEADS + h) * n_blocks + blk] != 0

    @pl.when(step == 0)
    def _():
        acc_scr[...] = oown_ref[0]

    def compute(slot, first_page):
        for h in range(N_HEADS):
            tot = jnp.zeros((dq, HEAD_DIM), F32)
            for p in range(hp):
                col0 = (slot * hp + p) * page
                ph = p_ref[0, h * dq:(h + 1) * dq, col0:col0 + page].astype(BF16)
                tot = tot + jnp.dot(ph, buf[slot, p, h].astype(BF16), preferred_element_type=F32)
            acc_scr[h * dq:(h + 1) * dq, :] += tot

    _paged_slots(pt_ref, cache_hbm, buf, sem, hp, n_pages, ppb, compute, wanted)

    @pl.when(step == n_steps - 1)
    def _():
        for h in range(N_HEADS):
            rows_h = slice(h * dq, (h + 1) * dq)
            o_ref[:, h * HEAD_DIM:(h + 1) * HEAD_DIM] = acc_scr[rows_h, :] * linv_ref[0, rows_h, :]


def _moba_sample(q, k_new, v_new, cache_k, cache_v, page_table, hp=4, n_slots=4):
    dec_b, n_pages = page_table.shape
    page = cache_k.shape[1]
    dq = q.shape[0] // dec_b
    past = n_pages * page
    n_blocks = past // MOBA_BLOCK
    while n_pages % (n_slots * hp):
        n_slots //= 2
    n_steps = n_pages // (n_slots * hp)
    rows = N_HEADS * dq
    pt = page_table.reshape(-1)

    new_spec = pl.BlockSpec((dq, ATTN_WIDTH), lambda b, s, pt_ref: (b, 0))
    any_spec = pl.BlockSpec(memory_space=pl.ANY)
    stat_spec = pl.BlockSpec((1, rows, HEAD_DIM), lambda b, s, pt_ref: (b, 0, 0))
    stat_sds = jax.ShapeDtypeStruct((dec_b, rows, HEAD_DIM), F32)
    assert hp % (MOBA_BLOCK // page) == 0, "a page half must hold whole key blocks"
    page_buf = pltpu.VMEM((n_slots, hp, N_HEADS, page, HEAD_DIM), F32)
    probs, o_own, l_inv, used = pl.pallas_call(
        functools.partial(_moba_sample_k_kernel, hp=hp, page=page, n_blocks=n_blocks),
        grid_spec=pltpu.PrefetchScalarGridSpec(
            num_scalar_prefetch=1,
            grid=(dec_b, n_steps),
            in_specs=[new_spec, new_spec, new_spec, any_spec],
            out_specs=[pl.BlockSpec((1, rows, past), lambda b, s, pt_ref: (b, 0, 0)),
                       stat_spec, stat_spec,
                       pl.BlockSpec((1, N_HEADS, n_blocks), lambda b, s, pt_ref: (b, 0, 0))],
            scratch_shapes=[page_buf, pltpu.SemaphoreType.DMA((n_slots,)),
                            pltpu.VMEM((n_pages, rows, page), F32),
                            pltpu.VMEM((n_blocks, N_HEADS, 8, HEAD_DIM), F32),
                            pltpu.VMEM((ATTN_WIDTH, max(rows, HEAD_DIM)), BF16)]),
        out_shape=[jax.ShapeDtypeStruct((dec_b, rows, past), F32), stat_sds, stat_sds,
                   jax.ShapeDtypeStruct((dec_b, N_HEADS, n_blocks), jnp.int32)],
        compiler_params=_params("arbitrary", "arbitrary"),
        name="moba_sample_k",
    )(pt, q, k_new, v_new, cache_k)

    return pl.pallas_call(
        functools.partial(_moba_sample_v_kernel, hp=hp, page=page, n_pages=n_pages),
        grid_spec=pltpu.PrefetchScalarGridSpec(
            num_scalar_prefetch=2,
            grid=(dec_b, n_steps),
            in_specs=[pl.BlockSpec((1, rows, n_slots * hp * page), lambda b, s, *_: (b, 0, s)),
                      pl.BlockSpec((1, rows, HEAD_DIM), lambda b, s, *_: (b, 0, 0)),
                      pl.BlockSpec((1, rows, HEAD_DIM), lambda b, s, *_: (b, 0, 0)),
                      any_spec],
            out_specs=pl.BlockSpec((dq, ATTN_WIDTH), lambda b, s, *_: (b, 0)),
            scratch_shapes=[page_buf, pltpu.SemaphoreType.DMA((n_slots,)),
                            pltpu.VMEM((rows, HEAD_DIM), F32)]),
        out_shape=jax.ShapeDtypeStruct(q.shape, F32),
        compiler_params=_params("arbitrary", "arbitrary"),
        name="moba_sample_v",
    )(pt, used.reshape(-1), probs, o_own, l_inv, cache_v)


def _pool_kernel(u_ref, halo_ref, w_ref, ls_ref, o_ref, ext_scr, *, pos0):
    t = pl.program_id(1)
    gb, r, width = u_ref.shape
    grp = width // len(POOL_WINDOWS)

    @pl.when(t == 0)
    def _():
        ext_scr[:, 0:POOL_HALO, :] = halo_ref[...]

    @pl.when(t > 0)
    def _():
        ext_scr[:, 0:POOL_HALO, :] = ext_scr[:, r:r + POOL_HALO, :]

    ext_scr[:, POOL_HALO:POOL_HALO + r, :] = u_ref[...]

    pos = pos0 + t * r + lax.broadcasted_iota(jnp.int32, (1, r, grp), 1)
    for g, w in enumerate(POOL_WINDOWS):
        cols = slice(g * grp, (g + 1) * grp)
        tot = ext_scr[:, POOL_HALO:POOL_HALO + r, cols]
        for d in range(1, w):
            tot = tot + ext_scr[:, POOL_HALO - d:POOL_HALO - d + r, cols]
        cnt = jnp.minimum(w, pos + 1).astype(F32)
        y = tot / cnt - u_ref[:, :, cols]
        y = jnp.dot(y.reshape(gb * r, grp).astype(BF16), w_ref[g], preferred_element_type=F32)
        o_ref[:, cols] = (y * ls_ref[:, cols]).astype(o_ref.dtype)


def _pool(u, halo, w_pool_bf, ls_pool, n_seq, seq, gb, r, pos0):
    width = u.shape[1]
    n_g, n_t = n_seq // gb, seq // r
    u3 = u.reshape(n_seq, seq, width)
    return pl.pallas_call(
        functools.partial(_pool_kernel, pos0=pos0),
        grid=(n_g, n_t),
        in_specs=[pl.BlockSpec((gb, r, width), lambda g, t: (g, t, 0)),
                  pl.BlockSpec((gb, POOL_HALO, width), lambda g, t: (g, 0, 0)),
                  pl.BlockSpec(w_pool_bf.shape, lambda g, t: (0, 0, 0)),
                  pl.BlockSpec((1, width), lambda g, t: (0, 0))],
        out_specs=pl.BlockSpec((gb * r, width), lambda g, t: (g * n_t + t, 0)),
        out_shape=jax.ShapeDtypeStruct(u.shape, BF16),
        scratch_shapes=[pltpu.VMEM((gb, r + POOL_HALO, width), F32)],
        compiler_params=_params("arbitrary", "arbitrary"),
        name="pool_mixer",
    )(u3, halo, w_pool_bf, ls_pool.reshape(1, width))


def _bf16_tiles(w_refs, copy_refs):
    tiles = []
    for i, w_ref in enumerate(w_refs):
        w = w_ref[...]
        if w.dtype != BF16:
            w = w.astype(BF16)
            copy_refs[i][...] = w
        tiles.append(w)
    return tiles


def _merge_kernel(x_ref, gt_ref, h_ref, attn_ref, pool_ref,
                  wga_ref, wgb_ref, woa_ref, wob_ref, wout_ref, o_ref, *copy_refs):
    c = pl.program_id(2)

    @pl.when(c == 0)
    def _():
        o_ref[...] = jnp.zeros_like(o_ref)

    wga, wgb, woa, wob, wout = _bf16_tiles((wga_ref, wgb_ref, woa_ref, wob_ref, wout_ref), copy_refs)
    h = h_ref[...]
    g_a = jax.nn.sigmoid(jnp.dot(h, wga, preferred_element_type=F32))
    g_b = jax.nn.sigmoid(jnp.dot(h, wgb, preferred_element_type=F32))
    y_a = jnp.dot(attn_ref[...].astype(BF16), woa, preferred_element_type=F32)
    y_b = jnp.dot(pool_ref[...].astype(BF16), wob, preferred_element_type=F32)
    mix_in = (g_a * y_a + g_b * y_b).astype(BF16)
    o_ref[...] += jnp.dot(mix_in, wout, preferred_element_type=F32).reshape(o_ref.shape)

    @pl.when(c == pl.num_programs(2) - 1)
    def _():
        o_ref[...] = x_ref[...] + gt_ref[...] * o_ref[...]


def _column_weight_specs(weights, tn, n_c):
    in_specs, copy_specs, copy_shapes = [], [], []
    for w, axis, first in weights:
        if axis == 1:
            in_specs.append(pl.BlockSpec((w.shape[0], tn), lambda g, t, c, first=first: (0, first + c)))
            copy_specs.append(pl.BlockSpec((w.shape[0], tn), lambda g, t, c: (0, c)))
            copy_shapes.append(jax.ShapeDtypeStruct((w.shape[0], n_c * tn), BF16))
        else:
            in_specs.append(pl.BlockSpec((tn, w.shape[1]), lambda g, t, c, first=first: (first + c, 0)))
            copy_specs.append(pl.BlockSpec((tn, w.shape[1]), lambda g, t, c: (c, 0)))
            copy_shapes.append(jax.ShapeDtypeStruct((n_c * tn, w.shape[1]), BF16))
    if all(w.dtype == BF16 for w, _, _ in weights):
        return in_specs, [], []
    return in_specs, copy_specs, copy_shapes


def _merge(x3, mod, h, attn, pooled, weights, gb, r, tn):
    n_g, n_t, d = _group_specs(x3, gb, r)
    tm = gb * r
    aw = attn.shape[1]
    pw = pooled.shape[1]
    n_c = d // tn
    w_specs, copy_specs, copy_shapes = _column_weight_specs(weights, tn, n_c)
    if copy_specs:
        assert n_g * n_t == 1, "weight copies are written by a single token tile"
    x_spec = pl.BlockSpec((gb, r, d), lambda g, t, c: (g, t, 0))
    mod_spec = lambda k: pl.BlockSpec((gb, 1, d), lambda g, t, c: (g, 0, k))
    out, *copies = pl.pallas_call(
        _merge_kernel,
        grid=(n_g, n_t, n_c),
        in_specs=[x_spec, mod_spec(2),
                  pl.BlockSpec((tm, d), lambda g, t, c: (g * n_t + t, 0)),
                  pl.BlockSpec((tm, aw), lambda g, t, c: (g * n_t + t, 0)),
                  pl.BlockSpec((tm, pw), lambda g, t, c: (g * n_t + t, 0))] + w_specs,
        out_specs=[x_spec] + copy_specs,
        out_shape=[jax.ShapeDtypeStruct(x3.shape, F32)] + copy_shapes,
        compiler_params=_params("arbitrary", "arbitrary", "arbitrary"),
        name="merge_out",
    )(x3, mod, h, attn, pooled, *[w for w, _, _ in weights])
    return out, copies


def _ffn_kernel(x_ref, sh_ref, sc_ref, gt_ref, g2_ref, wa_ref, wb_ref, wo_ref, o_ref, *rest):
    *copy_refs, h_scr = rest
    c = pl.program_id(2)

    @pl.when(c == 0)
    def _():
        _mod_norm_store(x_ref, g2_ref, sc_ref, sh_ref, h_scr)
        o_ref[...] = jnp.zeros_like(o_ref)

    wa, wb, wo = _bf16_tiles((wa_ref, wb_ref, wo_ref), copy_refs)
    h = h_scr[...]
    a = jnp.dot(h, wa, preferred_element_type=F32)
    b = jnp.dot(h, wb, preferred_element_type=F32)
    hid = (jax.nn.silu(a) * b).astype(BF16)
    o_ref[...] += jnp.dot(hid, wo, preferred_element_type=F32).reshape(o_ref.shape)

    @pl.when(c == pl.num_programs(2) - 1)
    def _():
        o_ref[...] = x_ref[...] + gt_ref[...] * o_ref[...]


def _ffn(x3, mod, g_norm2, weights, hidden, gb, r, tn):
    n_g, n_t, d = _group_specs(x3, gb, r)
    tm = gb * r
    n_c = hidden // tn
    w_specs, copy_specs, copy_shapes = _column_weight_specs(weights, tn, n_c)
    if copy_specs:
        assert n_g * n_t == 1, "weight copies are written by a single token tile"
    x_spec = pl.BlockSpec((gb, r, d), lambda g, t, c: (g, t, 0))
    mod_spec = lambda k: pl.BlockSpec((gb, 1, d), lambda g, t, c: (g, 0, k))
    out, *copies = pl.pallas_call(
        _ffn_kernel,
        grid=(n_g, n_t, n_c),
        in_specs=[x_spec, mod_spec(3), mod_spec(4), mod_spec(5),
                  pl.BlockSpec((1, d), lambda g, t, c: (0, 0))] + w_specs,
        out_specs=[x_spec] + copy_specs,
        out_shape=[jax.ShapeDtypeStruct(x3.shape, F32)] + copy_shapes,
        scratch_shapes=[pltpu.VMEM((tm, d), BF16)],
        compiler_params=_params("arbitrary", "arbitrary", "arbitrary"),
        name="ffn_swiglu",
    )(x3, mod, mod, mod, g_norm2.reshape(1, d), *[w for w, _, _ in weights])
    return out, copies


def _rope_tables(pos):
    half = HEAD_DIM // 2
    inv = ROPE_THETA ** (-jnp.arange(half, dtype=F32) / half)
    ang = pos.astype(F32)[:, None] * inv[None, :]
    cos, sin = jnp.cos(ang), jnp.sin(ang)
    return jnp.concatenate([cos, cos], axis=-1), jnp.concatenate([-sin, sin], axis=-1)


def _pick_rows(seq, target=512):
    r = min(seq, target)
    while seq % r:
        r //= 2
    return r


def kernel(x_prompt, x_sample, c_prompt, c_sample, cache_k, cache_v, state_pool, page_table, w_ada, b_ada, g_norm1, g_norm2, w_in, g_qnorm, g_knorm, w_pool, ls_pool, w_o_attn, w_o_pool, w_out, w_ffn_in, w_ffn_out):
    n_p, seq, d = x_prompt.shape
    n_s, dq, _ = x_sample.shape
    past = page_table.shape[1] * cache_k.shape[1]
    pool_w = state_pool.shape[2]
    n_state = state_pool.shape[1]

    w_qkvu_bf = w_in[:, :3 * ATTN_WIDTH + pool_w].astype(BF16)
    w_pool_bf = w_pool.astype(BF16)
    hidden = w_ffn_out.shape[0]

    n_c = n_p + n_s
    c_all = jnp.concatenate([c_prompt, c_sample], axis=0)
    c_all = jnp.pad(c_all, ((0, (-n_c) % 8), (0, 0)))
    mod = _ada(c_all, w_ada, b_ada)
    mod_p = mod[:n_p].reshape(n_p, 1, 6 * d)
    mod_s = mod[n_p:n_c].reshape(n_s, 1, 6 * d)

    cos_s, sin_s = _rope_tables(past + jnp.arange(dq))
    cos_s, sin_s = jnp.tile(cos_s, (n_s, 1)), jnp.tile(sin_s, (n_s, 1))
    q_s, k_s, v_s, u_s, h_s = _qkvu(x_sample, mod_s, g_norm1, w_qkvu_bf, g_qnorm, g_knorm, cos_s, sin_s, n_s, dq)
    attn_s = _moba_sample(q_s, k_s, v_s, cache_k, cache_v, page_table)
    halo_s = jnp.pad(state_pool, ((0, 0), (POOL_HALO - n_state, 0), (0, 0)))
    pooled_s = _pool(u_s, halo_s, w_pool_bf, ls_pool, n_s, dq, n_s, dq, past)
    gate0 = (w_in.shape[1] - 2 * d) // _TN_SAMPLE
    x1_s, merge_bf = _merge(
        x_sample, mod_s, h_s, attn_s, pooled_s,
        ((w_in, 1, gate0), (w_in, 1, gate0 + d // _TN_SAMPLE), (w_o_attn, 1, 0), (w_o_pool, 1, 0),
         (w_out, 0, 0)), n_s, dq, _TN_SAMPLE)
    y_s, ffn_bf = _ffn(
        x1_s, mod_s, g_norm2,
        ((w_ffn_in, 1, 0), (w_ffn_in, 1, hidden // _TN_PROMPT), (w_ffn_out, 0, 0)),
        hidden, n_s, dq, _TN_PROMPT)

    r_p = _pick_rows(seq)
    cos_p, sin_p = _rope_tables(jnp.arange(seq))
    q_p, k_p, v_p, u_p, h_p = _qkvu(x_prompt, mod_p, g_norm1, w_qkvu_bf, g_qnorm, g_knorm, cos_p, sin_p, 1, r_p)
    attn_p = _moba_prompt(q_p, k_p, v_p, n_p, seq)
    pooled_p = _pool(u_p, jnp.zeros((n_p, POOL_HALO, pool_w), F32), w_pool_bf, ls_pool,
                     n_p, seq, 1, r_p, 0)
    x1_p, _ = _merge(x_prompt, mod_p, h_p, attn_p, pooled_p,
                     tuple((w, axis, 0) for w, axis in zip(merge_bf, (1, 1, 1, 1, 0))), 1, r_p, _TN_PROMPT)
    y_p, _ = _ffn(x1_p, mod_p, g_norm2, tuple((w, axis, 0) for w, axis in zip(ffn_bf, (1, 1, 0))),
                  hidden, 1, r_p, _TN_PROMPT)

    k_prompt = k_p.reshape(n_p, seq, N_HEADS, HEAD_DIM)
    v_prompt = v_p.reshape(n_p, seq, N_HEADS, HEAD_DIM)
    pool_prompt = u_p.reshape(n_p, seq, pool_w)[:, seq - n_state:]
    k_sample = k_s.reshape(n_s, dq, N_HEADS, HEAD_DIM)
    v_sample = v_s.reshape(n_s, dq, N_HEADS, HEAD_DIM)
    pool_sample = jnp.concatenate([state_pool, u_s.reshape(n_s, dq, pool_w)], axis=1)[:, -n_state:]
    return (y_p, y_s, k_prompt, v_prompt, pool_prompt, k_sample, v_sample, pool_sample)
```

```python
import functools

import jax
import jax.numpy as jnp
from jax import lax
from jax.experimental import pallas as pl
from jax.experimental.pallas import tpu as pltpu

N_HEADS = 8
HEAD_DIM = 128
ATTN_WIDTH = N_HEADS * HEAD_DIM
MOBA_BLOCK = 256
MOBA_TOPK = 3
ROPE_THETA = 10000.0
POOL_WINDOWS = (2, 4, 8, 16)
POOL_HALO = 16
RMS_EPS = 1e-6
_TILE_GROUP = 4
_MASKED_MAX = -1e30
_TN_PROMPT = 512
_TN_SAMPLE = 256

F32 = jnp.float32
BF16 = jnp.bfloat16
NEG_INF = float("-inf")
_LOG2_E = 1.4426950408889634

_VMEM_LIMIT = 52 * 1024 * 1024


def _params(*sem):
    return pltpu.CompilerParams(dimension_semantics=sem, vmem_limit_bytes=_VMEM_LIMIT)


_NORM_ROWS = 16


def _mod_norm_store(x_ref, g_ref, sc_ref, sh_ref, h_scr):
    gb, r, d = x_ref.shape
    g = g_ref[...]
    if r >= _NORM_ROWS:
        per_group = r // _NORM_ROWS
        chunks = [(gi, 1, c * _NORM_ROWS, _NORM_ROWS) for gi in range(gb) for c in range(per_group)]
    else:
        n_g = _NORM_ROWS // r
        chunks = [(gi, n_g, 0, r) for gi in range(0, gb, n_g)]
    gains = {}
    for gi, n_g, r0, nr in chunks:
        if gi not in gains:
            gains[gi] = g * (1.0 + sc_ref[gi:gi + n_g])
        x = x_ref[gi:gi + n_g, r0:r0 + nr, :]
        ms = jnp.mean(x * x, axis=-1, keepdims=True)
        h = x * lax.rsqrt(ms + RMS_EPS) * gains[gi] + sh_ref[gi:gi + n_g]
        row0 = gi * r + r0
        h_scr[row0:row0 + n_g * nr, :] = h.reshape(n_g * nr, d).astype(BF16)


def _ada_kernel(c_ref, w_ref, b_ref, o_ref):
    o_ref[...] = jnp.dot(c_ref[...].astype(BF16), w_ref[...].astype(BF16),
                         preferred_element_type=F32) + b_ref[...]


def _ada(c_all, w_ada, b_ada, tn=1024):
    m, d = c_all.shape
    n = w_ada.shape[1]
    return pl.pallas_call(
        _ada_kernel,
        grid=(n // tn,),
        in_specs=[pl.BlockSpec((m, d), lambda j: (0, 0)),
                  pl.BlockSpec((d, tn), lambda j: (0, j)),
                  pl.BlockSpec((1, tn), lambda j: (0, j))],
        out_specs=pl.BlockSpec((m, tn), lambda j: (0, j)),
        out_shape=jax.ShapeDtypeStruct((m, n), F32),
        compiler_params=_params("arbitrary"),
        name="ada_mod",
    )(c_all, w_ada, b_ada.reshape(1, n))


def _head_norm_rope(th, g, cos, sin_signed):
    ms = jnp.mean(th * th, axis=-1, keepdims=True)
    y = th * lax.rsqrt(ms + RMS_EPS) * g
    return y * cos + pltpu.roll(y, HEAD_DIM // 2, 1) * sin_signed


_PROJ_CHUNK = 2 * HEAD_DIM


def _qkvu_kernel(x_ref, sh_ref, sc_ref, g1_ref, w_ref, gq_ref, gk_ref, cos_ref, sin_ref,
                 q_ref, k_ref, v_ref, u_ref, h_ref):
    _mod_norm_store(x_ref, g1_ref, sc_ref, sh_ref, h_ref)

    width = q_ref.shape[1]
    outs = ((q_ref, gq_ref), (k_ref, gk_ref), (v_ref, None), (u_ref, None))
    for i, (o_ref, g_ref) in enumerate(outs):
        for c in range(0, width, _PROJ_CHUNK):
            col = i * width + c
            acc = jnp.dot(h_ref[...], w_ref[:, col:col + _PROJ_CHUNK], preferred_element_type=F32)
            if g_ref is None:
                o_ref[:, c:c + _PROJ_CHUNK] = acc
            else:
                for hh in range(0, _PROJ_CHUNK, HEAD_DIM):
                    o_ref[:, c + hh:c + hh + HEAD_DIM] = _head_norm_rope(
                        acc[:, hh:hh + HEAD_DIM], g_ref[...], cos_ref[...], sin_ref[...])


def _group_specs(x3, gb, r):
    g_total, s, d = x3.shape
    n_g, n_t = g_total // gb, s // r
    return n_g, n_t, d


def _qkvu(x3, mod, g_norm1, w_in_bf, g_q, g_k, cos, sin_signed, gb, r):
    n_g, n_t, d = _group_specs(x3, gb, r)
    tm = gb * r
    tokens = x3.shape[0] * x3.shape[1]
    wq = ATTN_WIDTH
    x_spec = pl.BlockSpec((gb, r, d), lambda g, t: (g, t, 0))
    mod_spec = lambda k: pl.BlockSpec((gb, 1, d), lambda g, t: (g, 0, k))
    row_spec = pl.BlockSpec((1, d), lambda g, t: (0, 0))
    head_spec = pl.BlockSpec((1, HEAD_DIM), lambda g, t: (0, 0))
    tab_spec = pl.BlockSpec((tm, HEAD_DIM), lambda g, t: (t, 0))
    out_spec = pl.BlockSpec((tm, wq), lambda g, t: (g * n_t + t, 0))
    out_sds = jax.ShapeDtypeStruct((tokens, wq), F32)
    w_spec = pl.BlockSpec((d, 4 * wq), lambda g, t: (0, 0), pipeline_mode=pl.Buffered(1))
    return pl.pallas_call(
        _qkvu_kernel,
        grid=(n_g, n_t),
        in_specs=[x_spec, mod_spec(0), mod_spec(1), row_spec, w_spec,
                  head_spec, head_spec, tab_spec, tab_spec],
        out_specs=[out_spec] * 4 + [pl.BlockSpec((tm, d), lambda g, t: (g * n_t + t, 0))],
        out_shape=[out_sds] * 4 + [jax.ShapeDtypeStruct((tokens, d), BF16)],
        compiler_params=_params("arbitrary", "arbitrary"),
        name="qkvu_proj",
    )(x3, mod, mod, g_norm1.reshape(1, d), w_in_bf, g_q.reshape(1, HEAD_DIM),
      g_k.reshape(1, HEAD_DIM), cos, sin_signed)


def _topk_select(gate, valid, axis, n_candidates=None):
    n = gate.shape[axis] if n_candidates is None else n_candidates
    g = jnp.where(valid, gate, NEG_INF)
    idx = lax.broadcasted_iota(jnp.int32, gate.shape, axis)
    rank = jnp.zeros(gate.shape, jnp.int32)
    for m in range(n):
        gm = g[m:m + 1, :] if axis == 0 else g[:, m:m + 1]
        beats = (gm > g) | ((gm == g) & (m < idx))
        rank = rank + beats.astype(jnp.int32)
    return valid & (rank < MOBA_TOPK) & (jnp.abs(g) < float("inf"))


def _moba_prompt_kernel(q_ref, k_ref, v_ref, o_ref, kb_scr, vt_scr, kmean_scr, bias_scr, tri_scr, s_scr,
                        *, nb):
    blk = MOBA_BLOCK

    def prepare():
        for n in range(nb):
            kn = k_ref[n * blk:(n + 1) * blk, :]
            kb_scr[n * blk:(n + 1) * blk, :] = kn.astype(BF16)
            kmean_scr[n:n + 1, :] = jnp.mean(kn, axis=0, keepdims=True)
            vt_scr[n] = v_ref[n * blk:(n + 1) * blk, :].T.astype(BF16)
        kb_scr[nb * blk:, :] = jnp.zeros((kb_scr.shape[0] - nb * blk, HEAD_DIM), BF16)
        kmean = kmean_scr[...]
        for qb_i in range(nb):
            gate = lax.dot_general(kmean, q_ref[qb_i * blk:(qb_i + 1) * blk, :],
                                   (((1,), (1,)), ((), ())),
                                   precision=lax.Precision.HIGHEST, preferred_element_type=F32)
            n_idx = lax.broadcasted_iota(jnp.int32, gate.shape, 0)
            sel = _topk_select(gate, n_idx < qb_i, axis=0, n_candidates=qb_i)
            bias_scr[qb_i, 0:nb, :] = jnp.where(sel | (n_idx == qb_i), 0.0, NEG_INF)
            bias_scr[qb_i, nb:, :] = jnp.full((bias_scr.shape[1] - nb, blk), NEG_INF, F32)
        kpos = lax.broadcasted_iota(jnp.int32, (blk, blk), 0)
        qpos = lax.broadcasted_iota(jnp.int32, (blk, blk), 1)
        tri_scr[0] = jnp.zeros((blk, blk), F32)
        tri_scr[1] = jnp.where(kpos <= qpos, 0.0, NEG_INF)

    prepare()

    exp2_scale = HEAD_DIM ** -0.5 * _LOG2_E
    width = _TILE_GROUP

    def fold(x, op):
        return op(x.reshape(blk // 8, 8, blk), axis=0)

    def queries(qi):
        return q_ref[pl.ds(pl.multiple_of(qi * blk, blk), blk), :].astype(BF16)

    def score_group(qi, qb, g, slot):
        j0 = g * width
        keys = kb_scr[pl.ds(pl.multiple_of(j0 * blk, blk), width * blk), :]
        s_all = lax.dot_general(keys, qb, (((1,), (1,)), ((), ())), preferred_element_type=F32)
        m8 = jnp.full((8, blk), NEG_INF, F32)
        for t in range(width):
            j = j0 + t
            s = (s_all[t * blk:(t + 1) * blk, :] + bias_scr[qi, pl.ds(j, 1), :]
                 + tri_scr[jnp.asarray(j == qi, jnp.int32)])
            s_scr[slot, t] = s
            m8 = jnp.maximum(m8, fold(s, jnp.max))
        return m8

    def absorb_group(g, slot, m_run, l8, acc, m8):
        m_new = jnp.maximum(m_run, jnp.max(m8, axis=0, keepdims=True))
        alpha = jnp.exp2((m_run - m_new) * exp2_scale)
        l8 = l8 * alpha
        acc = acc * alpha
        for t in range(width):
            p = jnp.exp2((s_scr[slot, t] - m_new) * exp2_scale)
            l8 = l8 + fold(p, jnp.sum)
            acc = acc + jnp.dot(vt_scr[jnp.minimum(g * width + t, nb - 1)], p.astype(BF16),
                                preferred_element_type=F32)
        return m_new, l8, acc

    def one_query_block(qi, state):
        m8_first, first_slot = state
        qb = queries(qi)
        n_groups = (qi + width) // width

        def body(g, carry):
            m_run, l8, acc, m8 = carry
            m_run, l8, acc = absorb_group(g, (first_slot + g) % 2, m_run, l8, acc, m8)
            return m_run, l8, acc, score_group(qi, qb, g + 1, (first_slot + g + 1) % 2)

        carry = (jnp.full((1, blk), _MASKED_MAX, F32), jnp.zeros((8, blk), F32),
                 jnp.zeros((HEAD_DIM, blk), F32), m8_first)
        m_run, l8, acc, m8 = lax.fori_loop(0, n_groups - 1, body, carry)
        last_slot = (first_slot + n_groups - 1) % 2
        _, l8, acc = absorb_group(n_groups - 1, last_slot, m_run, l8, acc, m8)
        nxt = jnp.minimum(qi + 1, nb - 1)
        m8_next = score_group(nxt, queries(nxt), 0, 1 - last_slot)
        l = jnp.sum(l8, axis=0, keepdims=True)
        o_ref[pl.ds(pl.multiple_of(qi * blk, blk), blk), :] = (acc / l).T.astype(o_ref.dtype)
        return m8_next, 1 - last_slot

    lax.fori_loop(0, nb, one_query_block, (score_group(0, queries(0), 0, 0), jnp.int32(0)))


def _moba_prompt(q, k, v, n_seq, seq):
    nb = seq // MOBA_BLOCK
    nb_pad = -(-nb // _TILE_GROUP) * _TILE_GROUP + _TILE_GROUP
    blk = MOBA_BLOCK
    head_spec = pl.BlockSpec((seq, HEAD_DIM), lambda b, h: (b, h))
    return pl.pallas_call(
        functools.partial(_moba_prompt_kernel, nb=nb),
        grid=(n_seq, N_HEADS),
        in_specs=[head_spec, head_spec, head_spec],
        out_specs=head_spec,
        out_shape=jax.ShapeDtypeStruct(q.shape, BF16),
        scratch_shapes=[pltpu.VMEM((nb_pad * blk, HEAD_DIM), BF16),
                        pltpu.VMEM((nb, HEAD_DIM, blk), BF16),
                        pltpu.VMEM((nb, HEAD_DIM), F32),
                        pltpu.VMEM((nb, nb_pad, blk), F32),
                        pltpu.VMEM((2, blk, blk), F32),
                        pltpu.VMEM((2, _TILE_GROUP, blk, blk), F32)],
        compiler_params=_params("arbitrary", "arbitrary"),
        name="moba_prompt",
    )(q, k, v)


def _page_copy_groups(cache_hbm, pt_ref, buf, sem, slot, b, first_page, hp, n_pages, ppb, wanted):
    groups = []
    for p0 in range(0, hp, ppb):
        pgs = [pt_ref[b * n_pages + first_page + p0 + i] for i in range(ppb)]
        for h in range(N_HEADS):
            pred = None if wanted is None else wanted(b, (first_page + p0) // ppb, h)
            groups.append((pred, [pltpu.make_async_copy(cache_hbm.at[pgs[i], :, h, :],
                                                        buf.at[slot, p0 + i, h], sem.at[slot])
                                  for i in range(ppb)]))
    return groups


def _for_each_copy(groups, method):
    for n, (pred, copies) in enumerate(groups):
        def run(copies=copies, n=n):
            for i, c in enumerate(copies):
                if method == "start":
                    c.start(priority=(n + i) % 2)
                else:
                    c.wait()
        if pred is None:
            run()
        else:
            pl.when(pred)(run)


def _paged_slots(pt_ref, cache_hbm, buf, sem, hp, n_pages, ppb, compute, wanted=None):
    n_slots = buf.shape[0]
    b, step = pl.program_id(0), pl.program_id(1)
    n_b, n_steps = pl.num_programs(0), pl.num_programs(1)
    is_first = jnp.logical_and(b == 0, step == 0)
    is_last = jnp.logical_and(b == n_b - 1, step == n_steps - 1)
    wrap = step == n_steps - 1
    nxt_b = jnp.where(wrap, b + 1, b)
    nxt_step = jnp.where(wrap, 0, step + 1)

    def groups(bb, ss, slot):
        return _page_copy_groups(cache_hbm, pt_ref, buf, sem, slot, bb, (n_slots * ss + slot) * hp,
                                 hp, n_pages, ppb, wanted)

    @pl.when(is_first)
    def _():
        for slot in range(n_slots):
            _for_each_copy(groups(b, step, slot), "start")

    for slot in range(n_slots):
        _for_each_copy(groups(b, step, slot), "wait")
        compute(slot, (n_slots * step + slot) * hp)

        @pl.when(jnp.logical_not(is_last))
        def _():
            _for_each_copy(groups(nxt_b, nxt_step, slot), "start")


def _moba_sample_k_kernel(pt_ref, q_ref, knew_ref, vnew_ref, cache_hbm, p_ref, oown_ref, linv_ref,
                          used_ref, buf, sem, s_scr, kpart_scr, qbd_scr, *, hp, page, n_blocks):
    step = pl.program_id(1)
    n_steps = pl.num_programs(1)
    dq = q_ref.shape[0]
    rows = N_HEADS * dq
    scale = HEAD_DIM ** -0.5
    pages_per_block = MOBA_BLOCK // page
    n_pages = n_blocks * pages_per_block

    qh = [q_ref[:, h * HEAD_DIM:(h + 1) * HEAD_DIM] for h in range(N_HEADS)]
    qh_bf = [x.astype(BF16) for x in qh]

    @pl.when(step == 0)
    def _():
        kpart_scr[...] = jnp.zeros_like(kpart_scr)
        q_rep = jnp.concatenate([q_ref[...]] * N_HEADS
                                + [jnp.zeros((qbd_scr.shape[1] - rows, ATTN_WIDTH), F32)], axis=0)
        row_head = lax.broadcasted_iota(jnp.int32, q_rep.shape, 0) // dq
        lane_head = lax.broadcasted_iota(jnp.int32, q_rep.shape, 1) // HEAD_DIM
        qbd_scr[...] = jnp.where(row_head == lane_head, q_rep, 0.0).T.astype(BF16)

    def compute(slot, first_page):
        heads = [[buf[slot, p, h] for h in range(N_HEADS)] for p in range(hp)]
        for p in range(hp):
            blk = (first_page + p) // pages_per_block
            for h in range(N_HEADS):
                kpart_scr[blk, h] += jnp.sum(heads[p][h].reshape(page // 8, 8, HEAD_DIM), axis=0)
        for p0 in range(0, hp, hp // 2):
            keys = jnp.concatenate([jnp.concatenate(hs, axis=1) for hs in heads[p0:p0 + hp // 2]],
                                   axis=0)
            s_t = jnp.dot(keys.astype(BF16), qbd_scr[...], preferred_element_type=F32) * scale
            for i in range(hp // 2):
                s_scr[first_page + p0 + i] = s_t[i * page:(i + 1) * page, :].T[:rows, :]

    _paged_slots(pt_ref, cache_hbm, buf, sem, hp, n_pages, pages_per_block, compute)

    @pl.when(step == n_steps - 1)
    def _():
        gates = []
        for h in range(N_HEADS):
            kmean_h = jnp.sum(kpart_scr[:, h], axis=1) * (1.0 / MOBA_BLOCK)
            gates.append(lax.dot_general(qh[h], kmean_h, (((1,), (1,)), ((), ())),
                                         precision=lax.Precision.HIGHEST,
                                         preferred_element_type=F32))
        gate = jnp.concatenate(gates, axis=0)
        sel = _topk_select(gate, jnp.ones(gate.shape, jnp.bool_), axis=1)
        bias = jnp.where(sel, 0.0, NEG_INF)
        used_ref[0] = jnp.concatenate(
            [jnp.max(sel[h * dq:(h + 1) * dq, :].astype(jnp.int32), axis=0, keepdims=True)
             for h in range(N_HEADS)], axis=0)

        s_own = []
        for h in range(N_HEADS):
            kn = knew_ref[:, h * HEAD_DIM:(h + 1) * HEAD_DIM].astype(BF16)
            s_own.append(lax.dot_general(qh_bf[h], kn, (((1,), (1,)), ((), ())),
                                         preferred_element_type=F32) * scale)
        s_own = jnp.concatenate(s_own, axis=0)
        row = lax.broadcasted_iota(jnp.int32, s_own.shape, 0) % dq
        col = lax.broadcasted_iota(jnp.int32, s_own.shape, 1)
        s_own = jnp.where(col <= row, s_own, NEG_INF)

        m_vec = jnp.full((rows, page), NEG_INF, F32)
        for g in range(n_pages):
            n = g // pages_per_block
            m_vec = jnp.maximum(m_vec, s_scr[g] + bias[:, n:n + 1])
        m = jnp.maximum(jnp.max(m_vec, axis=1, keepdims=True), jnp.max(s_own, axis=1, keepdims=True))
        p_own = jnp.exp(s_own - m)
        l_vec = jnp.zeros((rows, page), F32)
        for g in range(n_pages):
            n = g // pages_per_block
            pb = jnp.exp(s_scr[g] + bias[:, n:n + 1] - m)
            l_vec = l_vec + pb
            p_ref[0, :, g * page:(g + 1) * page] = pb
        l = jnp.sum(l_vec, axis=1, keepdims=True) + jnp.sum(p_own, axis=1, keepdims=True)
        linv_ref[0] = jnp.broadcast_to(1.0 / l, (rows, HEAD_DIM))
        for h in range(N_HEADS):
            vn = vnew_ref[:, h * HEAD_DIM:(h + 1) * HEAD_DIM].astype(BF16)
            oown_ref[0, h * dq:(h + 1) * dq, :] = jnp.dot(
                p_own[h * dq:(h + 1) * dq, :].astype(BF16), vn, preferred_element_type=F32)


def _moba_sample_v_kernel(pt_ref, used_ref, p_ref, oown_ref, linv_ref, cache_hbm, o_ref, buf, sem,
                          acc_scr, *, hp, page, n_pages):
    step = pl.program_id(1)
    n_steps = pl.num_programs(1)
    dq = o_ref.shape[0]
    ppb = MOBA_BLOCK // page
    n_blocks = n_pages // ppb

    @pl.when(jnp.logical_and(pl.program_id(0) == 0, step == 0))
    def _():
        buf[...] = jnp.zeros_like(buf)

    def wanted(b, blk, h):
        return used_ref[(b * N_HEADS + h) * n_blocks + blk] != 0

    @pl.when(step == 0)
    def _():
        acc_scr[...] = oown_ref[0]

    def compute(slot, first_page):
        for h in range(N_HEADS):
            tot = jnp.zeros((dq, HEAD_DIM), F32)
            for p in range(hp):
                col0 = (slot * hp + p) * page
                ph = p_ref[0, h * dq:(h + 1) * dq, col0:col0 + page].astype(BF16)
                tot = tot + jnp.dot(ph, buf[slot, p, h].astype(BF16), preferred_element_type=F32)
            acc_scr[h * dq:(h + 1) * dq, :] += tot

    _paged_slots(pt_ref, cache_hbm, buf, sem, hp, n_pages, ppb, compute, wanted)

    @pl.when(step == n_steps - 1)
    def _():
        for h in range(N_HEADS):
            rows_h = slice(h * dq, (h + 1) * dq)
            o_ref[:, h * HEAD_DIM:(h + 1) * HEAD_DIM] = acc_scr[rows_h, :] * linv_ref[0, rows_h, :]


def _moba_sample(q, k_new, v_new, cache_k, cache_v, page_table, hp=4, n_slots=4):
    dec_b, n_pages = page_table.shape
    page = cache_k.shape[1]
    dq = q.shape[0] // dec_b
    past = n_pages * page
    n_blocks = past // MOBA_BLOCK
    while n_pages % (n_slots * hp):
        n_slots //= 2
    n_steps = n_pages // (n_slots * hp)
    rows = N_HEADS * dq
    pt = page_table.reshape(-1)

    new_spec = pl.BlockSpec((dq, ATTN_WIDTH), lambda b, s, pt_ref: (b, 0))
    any_spec = pl.BlockSpec(memory_space=pl.ANY)
    stat_spec = pl.BlockSpec((1, rows, HEAD_DIM), lambda b, s, pt_ref: (b, 0, 0))
    stat_sds = jax.ShapeDtypeStruct((dec_b, rows, HEAD_DIM), F32)
    assert hp % (MOBA_BLOCK // page) == 0, "a page half must hold whole key blocks"
    page_buf = pltpu.VMEM((n_slots, hp, N_HEADS, page, HEAD_DIM), F32)
    probs, o_own, l_inv, used = pl.pallas_call(
        functools.partial(_moba_sample_k_kernel, hp=hp, page=page, n_blocks=n_blocks),
        grid_spec=pltpu.PrefetchScalarGridSpec(
            num_scalar_prefetch=1,
            grid=(dec_b, n_steps),
            in_specs=[new_spec, new_spec, new_spec, any_spec],
            out_specs=[pl.BlockSpec((1, rows, past), lambda b, s, pt_ref: (b, 0, 0)),
                       stat_spec, stat_spec,
                       pl.BlockSpec((1, N_HEADS, n_blocks), lambda b, s, pt_ref: (b, 0, 0))],
            scratch_shapes=[page_buf, pltpu.SemaphoreType.DMA((n_slots,)),
                            pltpu.VMEM((n_pages, rows, page), F32),
                            pltpu.VMEM((n_blocks, N_HEADS, 8, HEAD_DIM), F32),
                            pltpu.VMEM((ATTN_WIDTH, max(rows, HEAD_DIM)), BF16)]),
        out_shape=[jax.ShapeDtypeStruct((dec_b, rows, past), F32), stat_sds, stat_sds,
                   jax.ShapeDtypeStruct((dec_b, N_HEADS, n_blocks), jnp.int32)],
        compiler_params=_params("arbitrary", "arbitrary"),
        name="moba_sample_k",
    )(pt, q, k_new, v_new, cache_k)

    return pl.pallas_call(
        functools.partial(_moba_sample_v_kernel, hp=hp, page=page, n_pages=n_pages),
        grid_spec=pltpu.PrefetchScalarGridSpec(
            num_scalar_prefetch=2,
            grid=(dec_b, n_steps),
            in_specs=[pl.BlockSpec((1, rows, n_slots * hp * page), lambda b, s, *_: (b, 0, s)),
                      pl.BlockSpec((1, rows, HEAD_DIM), lambda b, s, *_: (b, 0, 0)),
                      pl.BlockSpec((1, rows, HEAD_DIM), lambda b, s, *_: (b, 0, 0)),
                      any_spec],
            out_specs=pl.BlockSpec((dq, ATTN_WIDTH), lambda b, s, *_: (b, 0)),
            scratch_shapes=[page_buf, pltpu.SemaphoreType.DMA((n_slots,)),
                            pltpu.VMEM((rows, HEAD_DIM), F32)]),
        out_shape=jax.ShapeDtypeStruct(q.shape, F32),
        compiler_params=_params("arbitrary", "arbitrary"),
        name="moba_sample_v",
    )(pt, used.reshape(-1), probs, o_own, l_inv, cache_v)


def _pool_kernel(u_ref, halo_ref, w_ref, ls_ref, o_ref, ext_scr, *, pos0):
    t = pl.program_id(1)
    gb, r, width = u_ref.shape
    grp = width // len(POOL_WINDOWS)

    @pl.when(t == 0)
    def _():
        ext_scr[:, 0:POOL_HALO, :] = halo_ref[...]

    @pl.when(t > 0)
    def _():
        ext_scr[:, 0:POOL_HALO, :] = ext_scr[:, r:r + POOL_HALO, :]

    ext_scr[:, POOL_HALO:POOL_HALO + r, :] = u_ref[...]

    pos = pos0 + t * r + lax.broadcasted_iota(jnp.int32, (1, r, grp), 1)
    for g, w in enumerate(POOL_WINDOWS):
        cols = slice(g * grp, (g + 1) * grp)
        tot = ext_scr[:, POOL_HALO:POOL_HALO + r, cols]
        for d in range(1, w):
            tot = tot + ext_scr[:, POOL_HALO - d:POOL_HALO - d + r, cols]
        cnt = jnp.minimum(w, pos + 1).astype(F32)
        y = tot / cnt - u_ref[:, :, cols]
        y = jnp.dot(y.reshape(gb * r, grp).astype(BF16), w_ref[g], preferred_element_type=F32)
        o_ref[:, cols] = (y * ls_ref[:, cols]).astype(o_ref.dtype)


def _pool(u, halo, w_pool_bf, ls_pool, n_seq, seq, gb, r, pos0):
    width = u.shape[1]
    n_g, n_t = n_seq // gb, seq // r
    u3 = u.reshape(n_seq, seq, width)
    return pl.pallas_call(
        functools.partial(_pool_kernel, pos0=pos0),
        grid=(n_g, n_t),
        in_specs=[pl.BlockSpec((gb, r, width), lambda g, t: (g, t, 0)),
                  pl.BlockSpec((gb, POOL_HALO, width), lambda g, t: (g, 0, 0)),
                  pl.BlockSpec(w_pool_bf.shape, lambda g, t: (0, 0, 0)),
                  pl.BlockSpec((1, width), lambda g, t: (0, 0))],
        out_specs=pl.BlockSpec((gb * r, width), lambda g, t: (g * n_t + t, 0)),
        out_shape=jax.ShapeDtypeStruct(u.shape, BF16),
        scratch_shapes=[pltpu.VMEM((gb, r + POOL_HALO, width), F32)],
        compiler_params=_params("arbitrary", "arbitrary"),
        name="pool_mixer",
    )(u3, halo, w_pool_bf, ls_pool.reshape(1, width))


def _bf16_tiles(w_refs, copy_refs):
    tiles = []
    for i, w_ref in enumerate(w_refs):
        w = w_ref[...]
        if w.dtype != BF16:
            w = w.astype(BF16)
            copy_refs[i][...] = w
        tiles.append(w)
    return tiles


def _merge_kernel(x_ref, gt_ref, h_ref, attn_ref, pool_ref,
                  wga_ref, wgb_ref, woa_ref, wob_ref, wout_ref, o_ref, *copy_refs):
    c = pl.program_id(2)

    @pl.when(c == 0)
    def _():
        o_ref[...] = jnp.zeros_like(o_ref)

    wga, wgb, woa, wob, wout = _bf16_tiles((wga_ref, wgb_ref, woa_ref, wob_ref, wout_ref), copy_refs)
    h = h_ref[...]
    g_a = jax.nn.sigmoid(jnp.dot(h, wga, preferred_element_type=F32))
    g_b = jax.nn.sigmoid(jnp.dot(h, wgb, preferred_element_type=F32))
    y_a = jnp.dot(attn_ref[...].astype(BF16), woa, preferred_element_type=F32)
    y_b = jnp.dot(pool_ref[...].astype(BF16), wob, preferred_element_type=F32)
    mix_in = (g_a * y_a + g_b * y_b).astype(BF16)
    o_ref[...] += jnp.dot(mix_in, wout, preferred_element_type=F32).reshape(o_ref.shape)

    @pl.when(c == pl.num_programs(2) - 1)
    def _():
        o_ref[...] = x_ref[...] + gt_ref[...] * o_ref[...]


def _column_weight_specs(weights, tn, n_c):
    in_specs, copy_specs, copy_shapes = [], [], []
    for w, axis, first in weights:
        if axis == 1:
            in_specs.append(pl.BlockSpec((w.shape[0], tn), lambda g, t, c, first=first: (0, first + c)))
            copy_specs.append(pl.BlockSpec((w.shape[0], tn), lambda g, t, c: (0, c)))
            copy_shapes.append(jax.ShapeDtypeStruct((w.shape[0], n_c * tn), BF16))
        else:
            in_specs.append(pl.BlockSpec((tn, w.shape[1]), lambda g, t, c, first=first: (first + c, 0)))
            copy_specs.append(pl.BlockSpec((tn, w.shape[1]), lambda g, t, c: (c, 0)))
            copy_shapes.append(jax.ShapeDtypeStruct((n_c * tn, w.shape[1]), BF16))
    if all(w.dtype == BF16 for w, _, _ in weights):
        return in_specs, [], []
    return in_specs, copy_specs, copy_shapes


def _merge(x3, mod, h, attn, pooled, weights, gb, r, tn):
    n_g, n_t, d = _group_specs(x3, gb, r)
    tm = gb * r
    aw = attn.shape[1]
    pw = pooled.shape[1]
    n_c = d // tn
    w_specs, copy_specs, copy_shapes = _column_weight_specs(weights, tn, n_c)
    if copy_specs:
        assert n_g * n_t == 1, "weight copies are written by a single token tile"
    x_spec = pl.BlockSpec((gb, r, d), lambda g, t, c: (g, t, 0))
    mod_spec = lambda k: pl.BlockSpec((gb, 1, d), lambda g, t, c: (g, 0, k))
    out, *copies = pl.pallas_call(
        _merge_kernel,
        grid=(n_g, n_t, n_c),
        in_specs=[x_spec, mod_spec(2),
                  pl.BlockSpec((tm, d), lambda g, t, c: (g * n_t + t, 0)),
                  pl.BlockSpec((tm, aw), lambda g, t, c: (g * n_t + t, 0)),
                  pl.BlockSpec((tm, pw), lambda g, t, c: (g * n_t + t, 0))] + w_specs,
        out_specs=[x_spec] + copy_specs,
        out_shape=[jax.ShapeDtypeStruct(x3.shape, F32)] + copy_shapes,
        compiler_params=_params("arbitrary", "arbitrary", "arbitrary"),
        name="merge_out",
    )(x3, mod, h, attn, pooled, *[w for w, _, _ in weights])
    return out, copies


def _ffn_kernel(x_ref, sh_ref, sc_ref, gt_ref, g2_ref, wa_ref, wb_ref, wo_ref, o_ref, *rest):
    *copy_refs, h_scr = rest
    c = pl.program_id(2)

    @pl.when(c == 0)
    def _():
        _mod_norm_store(x_ref, g2_ref, sc_ref, sh_ref, h_scr)
        o_ref[...] = jnp.zeros_like(o_ref)

    wa, wb, wo = _bf16_tiles((wa_ref, wb_ref, wo_ref), copy_refs)
    h = h_scr[...]
    a = jnp.dot(h, wa, preferred_element_type=F32)
    b = jnp.dot(h, wb, preferred_element_type=F32)
    hid = (jax.nn.silu(a) * b).astype(BF16)
    o_ref[...] += jnp.dot(hid, wo, preferred_element_type=F32).reshape(o_ref.shape)

    @pl.when(c == pl.num_programs(2) - 1)
    def _():
        o_ref[...] = x_ref[...] + gt_ref[...] * o_ref[...]


def _ffn(x3, mod, g_norm2, weights, hidden, gb, r, tn):
    n_g, n_t, d = _group_specs(x3, gb, r)
    tm = gb * r
    n_c = hidden // tn
    w_specs, copy_specs, copy_shapes = _column_weight_specs(weights, tn, n_c)
    if copy_specs:
        assert n_g * n_t == 1, "weight copies are written by a single token tile"
    x_spec = pl.BlockSpec((gb, r, d), lambda g, t, c: (g, t, 0))
    mod_spec = lambda k: pl.BlockSpec((gb, 1, d), lambda g, t, c: (g, 0, k))
    out, *copies = pl.pallas_call(
        _ffn_kernel,
        grid=(n_g, n_t, n_c),
        in_specs=[x_spec, mod_spec(3), mod_spec(4), mod_spec(5),
                  pl.BlockSpec((1, d), lambda g, t, c: (0, 0))] + w_specs,
        out_specs=[x_spec] + copy_specs,
        out_shape=[jax.ShapeDtypeStruct(x3.shape, F32)] + copy_shapes,
        scratch_shapes=[pltpu.VMEM((tm, d), BF16)],
        compiler_params=_params("arbitrary", "arbitrary", "arbitrary"),
        name="ffn_swiglu",
    )(x3, mod, mod, mod, g_norm2.reshape(1, d), *[w for w, _, _ in weights])
    return out, copies


def _rope_tables(pos):
    half = HEAD_DIM // 2
    inv = ROPE_THETA ** (-jnp.arange(half, dtype=F32) / half)
    ang = pos.astype(F32)[:, None] * inv[None, :]
    cos, sin = jnp.cos(ang), jnp.sin(ang)
    return jnp.concatenate([cos, cos], axis=-1), jnp.concatenate([-sin, sin], axis=-1)


def _pick_rows(seq, target=512):
    r = min(seq, target)
    while seq % r:
        r //= 2
    return r


def kernel(x_prompt, x_sample, c_prompt, c_sample, cache_k, cache_v, state_pool, page_table, w_ada, b_ada, g_norm1, g_norm2, w_in, g_qnorm, g_knorm, w_pool, ls_pool, w_o_attn, w_o_pool, w_out, w_ffn_in, w_ffn_out):
    n_p, seq, d = x_prompt.shape
    n_s, dq, _ = x_sample.shape
    past = page_table.shape[1] * cache_k.shape[1]
    pool_w = state_pool.shape[2]
    n_state = state_pool.shape[1]

    w_qkvu_bf = w_in[:, :3 * ATTN_WIDTH + pool_w].astype(BF16)
    w_pool_bf = w_pool.astype(BF16)
    hidden = w_ffn_out.shape[0]

    n_c = n_p + n_s
    c_all = jnp.concatenate([c_prompt, c_sample], axis=0)
    c_all = jnp.pad(c_all, ((0, (-n_c) % 8), (0, 0)))
    mod = _ada(c_all, w_ada, b_ada)
    mod_p = mod[:n_p].reshape(n_p, 1, 6 * d)
    mod_s = mod[n_p:n_c].reshape(n_s, 1, 6 * d)

    cos_s, sin_s = _rope_tables(past + jnp.arange(dq))
    cos_s, sin_s = jnp.tile(cos_s, (n_s, 1)), jnp.tile(sin_s, (n_s, 1))
    q_s, k_s, v_s, u_s, h_s = _qkvu(x_sample, mod_s, g_norm1, w_qkvu_bf, g_qnorm, g_knorm, cos_s, sin_s, n_s, dq)
    attn_s = _moba_sample(q_s, k_s, v_s, cache_k, cache_v, page_table)
    halo_s = jnp.pad(state_pool, ((0, 0), (POOL_HALO - n_state, 0), (0, 0)))
    pooled_s = _pool(u_s, halo_s, w_pool_bf, ls_pool, n_s, dq, n_s, dq, past)
    gate0 = (w_in.shape[1] - 2 * d) // _TN_SAMPLE
    x1_s, merge_bf = _merge(
        x_sample, mod_s, h_s, attn_s, pooled_s,
        ((w_in, 1, gate0), (w_in, 1, gate0 + d // _TN_SAMPLE), (w_o_attn, 1, 0), (w_o_pool, 1, 0),
         (w_out, 0, 0)), n_s, dq, _TN_SAMPLE)
    y_s, ffn_bf = _ffn(
        x1_s, mod_s, g_norm2,
        ((w_ffn_in, 1, 0), (w_ffn_in, 1, hidden // _TN_PROMPT), (w_ffn_out, 0, 0)),
        hidden, n_s, dq, _TN_PROMPT)

    r_p = _pick_rows(seq)
    cos_p, sin_p = _rope_tables(jnp.arange(seq))
    q_p, k_p, v_p, u_p, h_p = _qkvu(x_prompt, mod_p, g_norm1, w_qkvu_bf, g_qnorm, g_knorm, cos_p, sin_p, 1, r_p)
    attn_p = _moba_prompt(q_p, k_p, v_p, n_p, seq)
    pooled_p = _pool(u_p, jnp.zeros((n_p, POOL_HALO, pool_w), F32), w_pool_bf, ls_pool,
                     n_p, seq, 1, r_p, 0)
    x1_p, _ = _merge(x_prompt, mod_p, h_p, attn_p, pooled_p,
                     tuple((w, axis, 0) for w, axis in zip(merge_bf, (1, 1, 1, 1, 0))), 1, r_p, _TN_PROMPT)
    y_p, _ = _ffn(x1_p, mod_p, g_norm2, tuple((w, axis, 0) for w, axis in zip(ffn_bf, (1, 1, 0))),
                  hidden, 1, r_p, _TN_PROMPT)

    k_prompt = k_p.reshape(n_p, seq, N_HEADS, HEAD_DIM)
    v_prompt = v_p.reshape(n_p, seq, N_HEADS, HEAD_DIM)
    pool_prompt = u_p.reshape(n_p, seq, pool_w)[:, seq - n_state:]
    k_sample = k_s.reshape(n_s, dq, N_HEADS, HEAD_DIM)
    v_sample = v_s.reshape(n_s, dq, N_HEADS, HEAD_DIM)
    pool_sample = jnp.concatenate([state_pool, u_s.reshape(n_s, dq, pool_w)], axis=1)[:, -n_state:]
    return (y_p, y_s, k_prompt, v_prompt, pool_prompt, k_sample, v_sample, pool_sample)
```

```python
import functools

import jax
import jax.numpy as jnp
from jax import lax
from jax.experimental import pallas as pl
from jax.experimental.pallas import tpu as pltpu

N_HEADS = 8
HEAD_DIM = 128
ATTN_WIDTH = N_HEADS * HEAD_DIM
MOBA_BLOCK = 256
MOBA_TOPK = 3
ROPE_THETA = 10000.0
POOL_WINDOWS = (2, 4, 8, 16)
POOL_HALO = 16
RMS_EPS = 1e-6
_TILE_GROUP = 4
_MASKED_MAX = -1e30
_TN_PROMPT = 512
_TN_SAMPLE = 256

F32 = jnp.float32
BF16 = jnp.bfloat16
NEG_INF = float("-inf")
_LOG2_E = 1.4426950408889634

_VMEM_LIMIT = 52 * 1024 * 1024


def _params(*sem):
    return pltpu.CompilerParams(dimension_semantics=sem, vmem_limit_bytes=_VMEM_LIMIT)


_NORM_ROWS = 16


def _mod_norm_store(x_ref, g_ref, sc_ref, sh_ref, h_scr):
    gb, r, d = x_ref.shape
    g = g_ref[...]
    if r >= _NORM_ROWS:
        per_group = r // _NORM_ROWS
        chunks = [(gi, 1, c * _NORM_ROWS, _NORM_ROWS) for gi in range(gb) for c in range(per_group)]
    else:
        n_g = _NORM_ROWS // r
        chunks = [(gi, n_g, 0, r) for gi in range(0, gb, n_g)]
    gains = {}
    for gi, n_g, r0, nr in chunks:
        if gi not in gains:
            gains[gi] = g * (1.0 + sc_ref[gi:gi + n_g])
        x = x_ref[gi:gi + n_g, r0:r0 + nr, :]
        ms = jnp.mean(x * x, axis=-1, keepdims=True)
        h = x * lax.rsqrt(ms + RMS_EPS) * gains[gi] + sh_ref[gi:gi + n_g]
        row0 = gi * r + r0
        h_scr[row0:row0 + n_g * nr, :] = h.reshape(n_g * nr, d).astype(BF16)


def _ada_kernel(c_ref, w_ref, b_ref, o_ref):
    o_ref[...] = jnp.dot(c_ref[...].astype(BF16), w_ref[...].astype(BF16),
                         preferred_element_type=F32) + b_ref[...]


def _ada(c_all, w_ada, b_ada, tn=1024):
    m, d = c_all.shape
    n = w_ada.shape[1]
    return pl.pallas_call(
        _ada_kernel,
        grid=(n // tn,),
        in_specs=[pl.BlockSpec((m, d), lambda j: (0, 0)),
                  pl.BlockSpec((d, tn), lambda j: (0, j)),
                  pl.BlockSpec((1, tn), lambda j: (0, j))],
        out_specs=pl.BlockSpec((m, tn), lambda j: (0, j)),
        out_shape=jax.ShapeDtypeStruct((m, n), F32),
        compiler_params=_params("arbitrary"),
        name="ada_mod",
    )(c_all, w_ada, b_ada.reshape(1, n))


def _head_norm_rope(th, g, cos, sin_signed):
    ms = jnp.mean(th * th, axis=-1, keepdims=True)
    y = th * lax.rsqrt(ms + RMS_EPS) * g
    return y * cos + pltpu.roll(y, HEAD_DIM // 2, 1) * sin_signed


_PROJ_CHUNK = 2 * HEAD_DIM


def _qkvu_kernel(x_ref, sh_ref, sc_ref, g1_ref, w_ref, gq_ref, gk_ref, cos_ref, sin_ref,
                 q_ref, k_ref, v_ref, u_ref, h_ref):
    _mod_norm_store(x_ref, g1_ref, sc_ref, sh_ref, h_ref)

    width = q_ref.shape[1]
    outs = ((q_ref, gq_ref), (k_ref, gk_ref), (v_ref, None), (u_ref, None))
    for i, (o_ref, g_ref) in enumerate(outs):
        for c in range(0, width, _PROJ_CHUNK):
            col = i * width + c
            acc = jnp.dot(h_ref[...], w_ref[:, col:col + _PROJ_CHUNK], preferred_element_type=F32)
            if g_ref is None:
                o_ref[:, c:c + _PROJ_CHUNK] = acc
            else:
                for hh in range(0, _PROJ_CHUNK, HEAD_DIM):
                    o_ref[:, c + hh:c + hh + HEAD_DIM] = _head_norm_rope(
                        acc[:, hh:hh + HEAD_DIM], g_ref[...], cos_ref[...], sin_ref[...])


def _group_specs(x3, gb, r):
    g_total, s, d = x3.shape
    n_g, n_t = g_total // gb, s // r
    return n_g, n_t, d


def _qkvu(x3, mod, g_norm1, w_in_bf, g_q, g_k, cos, sin_signed, gb, r):
    n_g, n_t, d = _group_specs(x3, gb, r)
    tm = gb * r
    tokens = x3.shape[0] * x3.shape[1]
    wq = ATTN_WIDTH
    x_spec = pl.BlockSpec((gb, r, d), lambda g, t: (g, t, 0))
    mod_spec = lambda k: pl.BlockSpec((gb, 1, d), lambda g, t: (g, 0, k))
    row_spec = pl.BlockSpec((1, d), lambda g, t: (0, 0))
    head_spec = pl.BlockSpec((1, HEAD_DIM), lambda g, t: (0, 0))
    tab_spec = pl.BlockSpec((tm, HEAD_DIM), lambda g, t: (t, 0))
    out_spec = pl.BlockSpec((tm, wq), lambda g, t: (g * n_t + t, 0))
    out_sds = jax.ShapeDtypeStruct((tokens, wq), F32)
    w_spec = pl.BlockSpec((d, 4 * wq), lambda g, t: (0, 0), pipeline_mode=pl.Buffered(1))
    return pl.pallas_call(
        _qkvu_kernel,
        grid=(n_g, n_t),
        in_specs=[x_spec, mod_spec(0), mod_spec(1), row_spec, w_spec,
                  head_spec, head_spec, tab_spec, tab_spec],
        out_specs=[out_spec] * 4 + [pl.BlockSpec((tm, d), lambda g, t: (g * n_t + t, 0))],
        out_shape=[out_sds] * 4 + [jax.ShapeDtypeStruct((tokens, d), BF16)],
        compiler_params=_params("arbitrary", "arbitrary"),
        name="qkvu_proj",
    )(x3, mod, mod, g_norm1.reshape(1, d), w_in_bf, g_q.reshape(1, HEAD_DIM),
      g_k.reshape(1, HEAD_DIM), cos, sin_signed)


def _topk_select(gate, valid, axis):
    n = gate.shape[axis]
    g = jnp.where(valid, gate, NEG_INF)
    idx = lax.broadcasted_iota(jnp.int32, gate.shape, axis)
    rank = jnp.zeros(gate.shape, jnp.int32)
    for m in range(n):
        gm = g[m:m + 1, :] if axis == 0 else g[:, m:m + 1]
        beats = (gm > g) | ((gm == g) & (m < idx))
        rank = rank + beats.astype(jnp.int32)
    return valid & (rank < MOBA_TOPK) & (jnp.abs(g) < float("inf"))


def _moba_prompt_kernel(q_ref, k_ref, v_ref, o_ref, kb_scr, vt_scr, kmean_scr, bias_scr, tri_scr, s_scr,
                        *, nb):
    blk = MOBA_BLOCK

    def prepare():
        for n in range(nb):
            kn = k_ref[n * blk:(n + 1) * blk, :]
            kb_scr[n * blk:(n + 1) * blk, :] = kn.astype(BF16)
            kmean_scr[n:n + 1, :] = jnp.mean(kn, axis=0, keepdims=True)
            vt_scr[n] = v_ref[n * blk:(n + 1) * blk, :].T.astype(BF16)
        kb_scr[nb * blk:, :] = jnp.zeros((kb_scr.shape[0] - nb * blk, HEAD_DIM), BF16)
        kmean = kmean_scr[...]
        for qb_i in range(nb):
            gate = lax.dot_general(kmean, q_ref[qb_i * blk:(qb_i + 1) * blk, :],
                                   (((1,), (1,)), ((), ())),
                                   precision=lax.Precision.HIGHEST, preferred_element_type=F32)
            n_idx = lax.broadcasted_iota(jnp.int32, gate.shape, 0)
            sel = _topk_select(gate, n_idx < qb_i, axis=0)
            bias_scr[qb_i, 0:nb, :] = jnp.where(sel | (n_idx == qb_i), 0.0, NEG_INF)
            bias_scr[qb_i, nb:, :] = jnp.full((bias_scr.shape[1] - nb, blk), NEG_INF, F32)
        kpos = lax.broadcasted_iota(jnp.int32, (blk, blk), 0)
        qpos = lax.broadcasted_iota(jnp.int32, (blk, blk), 1)
        tri_scr[0] = jnp.zeros((blk, blk), F32)
        tri_scr[1] = jnp.where(kpos <= qpos, 0.0, NEG_INF)

    prepare()

    exp2_scale = HEAD_DIM ** -0.5 * _LOG2_E
    width = _TILE_GROUP

    def fold(x, op):
        return op(x.reshape(blk // 8, 8, blk), axis=0)

    def queries(qi):
        return q_ref[pl.ds(pl.multiple_of(qi * blk, blk), blk), :].astype(BF16)

    def score_group(qi, qb, g, slot):
        j0 = g * width
        keys = kb_scr[pl.ds(pl.multiple_of(j0 * blk, blk), width * blk), :]
        s_all = lax.dot_general(keys, qb, (((1,), (1,)), ((), ())), preferred_element_type=F32)
        m8 = jnp.full((8, blk), NEG_INF, F32)
        for t in range(width):
            j = j0 + t
            s = (s_all[t * blk:(t + 1) * blk, :] + bias_scr[qi, pl.ds(j, 1), :]
                 + tri_scr[jnp.asarray(j == qi, jnp.int32)])
            s_scr[slot, t] = s
            m8 = jnp.maximum(m8, fold(s, jnp.max))
        return m8

    def absorb_group(g, slot, m_run, l8, acc, m8):
        m_new = jnp.maximum(m_run, jnp.max(m8, axis=0, keepdims=True))
        alpha = jnp.exp2((m_run - m_new) * exp2_scale)
        l8 = l8 * alpha
        acc = acc * alpha
        for t in range(width):
            p = jnp.exp2((s_scr[slot, t] - m_new) * exp2_scale)
            l8 = l8 + fold(p, jnp.sum)
            acc = acc + jnp.dot(vt_scr[jnp.minimum(g * width + t, nb - 1)], p.astype(BF16),
                                preferred_element_type=F32)
        return m_new, l8, acc

    def one_query_block(qi, state):
        m8_first, first_slot = state
        qb = queries(qi)
        n_groups = (qi + width) // width

        def body(g, carry):
            m_run, l8, acc, m8 = carry
            m_run, l8, acc = absorb_group(g, (first_slot + g) % 2, m_run, l8, acc, m8)
            return m_run, l8, acc, score_group(qi, qb, g + 1, (first_slot + g + 1) % 2)

        carry = (jnp.full((1, blk), _MASKED_MAX, F32), jnp.zeros((8, blk), F32),
                 jnp.zeros((HEAD_DIM, blk), F32), m8_first)
        m_run, l8, acc, m8 = lax.fori_loop(0, n_groups - 1, body, carry)
        last_slot = (first_slot + n_groups - 1) % 2
        _, l8, acc = absorb_group(n_groups - 1, last_slot, m_run, l8, acc, m8)
        nxt = jnp.minimum(qi + 1, nb - 1)
        m8_next = score_group(nxt, queries(nxt), 0, 1 - last_slot)
        l = jnp.sum(l8, axis=0, keepdims=True)
        o_ref[pl.ds(pl.multiple_of(qi * blk, blk), blk), :] = (acc / l).T.astype(o_ref.dtype)
        return m8_next, 1 - last_slot

    lax.fori_loop(0, nb, one_query_block, (score_group(0, queries(0), 0, 0), jnp.int32(0)))


def _moba_prompt(q, k, v, n_seq, seq):
    nb = seq // MOBA_BLOCK
    nb_pad = -(-nb // _TILE_GROUP) * _TILE_GROUP + _TILE_GROUP
    blk = MOBA_BLOCK
    head_spec = pl.BlockSpec((seq, HEAD_DIM), lambda b, h: (b, h))
    return pl.pallas_call(
        functools.partial(_moba_prompt_kernel, nb=nb),
        grid=(n_seq, N_HEADS),
        in_specs=[head_spec, head_spec, head_spec],
        out_specs=head_spec,
        out_shape=jax.ShapeDtypeStruct(q.shape, BF16),
        scratch_shapes=[pltpu.VMEM((nb_pad * blk, HEAD_DIM), BF16),
                        pltpu.VMEM((nb, HEAD_DIM, blk), BF16),
                        pltpu.VMEM((nb, HEAD_DIM), F32),
                        pltpu.VMEM((nb, nb_pad, blk), F32),
                        pltpu.VMEM((2, blk, blk), F32),
                        pltpu.VMEM((2, _TILE_GROUP, blk, blk), F32)],
        compiler_params=_params("arbitrary", "arbitrary"),
        name="moba_prompt",
    )(q, k, v)


def _page_copy_groups(cache_hbm, pt_ref, buf, sem, slot, b, first_page, hp, n_pages, ppb, wanted):
    groups = []
    for p0 in range(0, hp, ppb):
        pgs = [pt_ref[b * n_pages + first_page + p0 + i] for i in range(ppb)]
        for h in range(N_HEADS):
            pred = None if wanted is None else wanted(b, (first_page + p0) // ppb, h)
            groups.append((pred, [pltpu.make_async_copy(cache_hbm.at[pgs[i], :, h, :],
                                                        buf.at[slot, p0 + i, h], sem.at[slot])
                                  for i in range(ppb)]))
    return groups


def _for_each_copy(groups, method):
    for pred, copies in groups:
        def run(copies=copies):
            for c in copies:
                getattr(c, method)()
        if pred is None:
            run()
        else:
            pl.when(pred)(run)


def _paged_slots(pt_ref, cache_hbm, buf, sem, hp, n_pages, ppb, compute, wanted=None):
    n_slots = buf.shape[0]
    b, step = pl.program_id(0), pl.program_id(1)
    n_b, n_steps = pl.num_programs(0), pl.num_programs(1)
    is_first = jnp.logical_and(b == 0, step == 0)
    is_last = jnp.logical_and(b == n_b - 1, step == n_steps - 1)
    wrap = step == n_steps - 1
    nxt_b = jnp.where(wrap, b + 1, b)
    nxt_step = jnp.where(wrap, 0, step + 1)

    def groups(bb, ss, slot):
        return _page_copy_groups(cache_hbm, pt_ref, buf, sem, slot, bb, (n_slots * ss + slot) * hp,
                                 hp, n_pages, ppb, wanted)

    @pl.when(is_first)
    def _():
        for slot in range(n_slots):
            _for_each_copy(groups(b, step, slot), "start")

    for slot in range(n_slots):
        _for_each_copy(groups(b, step, slot), "wait")
        compute(slot, (n_slots * step + slot) * hp)

        @pl.when(jnp.logical_not(is_last))
        def _():
            _for_each_copy(groups(nxt_b, nxt_step, slot), "start")


def _moba_sample_k_kernel(pt_ref, q_ref, knew_ref, vnew_ref, cache_hbm, p_ref, oown_ref, linv_ref,
                          used_ref, buf, sem, s_scr, kpart_scr, qbd_scr, *, hp, page, n_blocks):
    step = pl.program_id(1)
    n_steps = pl.num_programs(1)
    dq = q_ref.shape[0]
    rows = N_HEADS * dq
    scale = HEAD_DIM ** -0.5
    pages_per_block = MOBA_BLOCK // page
    n_pages = n_blocks * pages_per_block

    qh = [q_ref[:, h * HEAD_DIM:(h + 1) * HEAD_DIM] for h in range(N_HEADS)]
    qh_bf = [x.astype(BF16) for x in qh]

    @pl.when(step == 0)
    def _():
        kpart_scr[...] = jnp.zeros_like(kpart_scr)
        q_rep = jnp.concatenate([q_ref[...]] * N_HEADS
                                + [jnp.zeros((qbd_scr.shape[1] - rows, ATTN_WIDTH), F32)], axis=0)
        row_head = lax.broadcasted_iota(jnp.int32, q_rep.shape, 0) // dq
        lane_head = lax.broadcasted_iota(jnp.int32, q_rep.shape, 1) // HEAD_DIM
        qbd_scr[...] = jnp.where(row_head == lane_head, q_rep, 0.0).T.astype(BF16)

    def compute(slot, first_page):
        heads = [[buf[slot, p, h] for h in range(N_HEADS)] for p in range(hp)]
        for p in range(hp):
            blk = (first_page + p) // pages_per_block
            for h in range(N_HEADS):
                kpart_scr[blk, h] += jnp.sum(heads[p][h].reshape(page // 8, 8, HEAD_DIM), axis=0)
        for p0 in range(0, hp, hp // 2):
            keys = jnp.concatenate([jnp.concatenate(hs, axis=1) for hs in heads[p0:p0 + hp // 2]],
                                   axis=0)
            s_t = jnp.dot(keys.astype(BF16), qbd_scr[...], preferred_element_type=F32) * scale
            for i in range(hp // 2):
                s_scr[first_page + p0 + i] = s_t[i * page:(i + 1) * page, :].T[:rows, :]

    _paged_slots(pt_ref, cache_hbm, buf, sem, hp, n_pages, pages_per_block, compute)

    @pl.when(step == n_steps - 1)
    def _():
        gates = []
        for h in range(N_HEADS):
            kmean_h = jnp.sum(kpart_scr[:, h], axis=1) * (1.0 / MOBA_BLOCK)
            gates.append(lax.dot_general(qh[h], kmean_h, (((1,), (1,)), ((), ())),
                                         precision=lax.Precision.HIGHEST,
                                         preferred_element_type=F32))
        gate = jnp.concatenate(gates, axis=0)
        sel = _topk_select(gate, jnp.ones(gate.shape, jnp.bool_), axis=1)
        bias = jnp.where(sel, 0.0, NEG_INF)
        used_ref[0] = jnp.concatenate(
            [jnp.max(sel[h * dq:(h + 1) * dq, :].astype(jnp.int32), axis=0, keepdims=True)
             for h in range(N_HEADS)], axis=0)

        s_own = []
        for h in range(N_HEADS):
            kn = knew_ref[:, h * HEAD_DIM:(h + 1) * HEAD_DIM].astype(BF16)
            s_own.append(lax.dot_general(qh_bf[h], kn, (((1,), (1,)), ((), ())),
                                         preferred_element_type=F32) * scale)
        s_own = jnp.concatenate(s_own, axis=0)
        row = lax.broadcasted_iota(jnp.int32, s_own.shape, 0) % dq
        col = lax.broadcasted_iota(jnp.int32, s_own.shape, 1)
        s_own = jnp.where(col <= row, s_own, NEG_INF)

        m_vec = jnp.full((rows, page), NEG_INF, F32)
        for g in range(n_pages):
            n = g // pages_per_block
            m_vec = jnp.maximum(m_vec, s_scr[g] + bias[:, n:n + 1])
        m = jnp.maximum(jnp.max(m_vec, axis=1, keepdims=True), jnp.max(s_own, axis=1, keepdims=True))
        p_own = jnp.exp(s_own - m)
        l_vec = jnp.zeros((rows, page), F32)
        for g in range(n_pages):
            n = g // pages_per_block
            pb = jnp.exp(s_scr[g] + bias[:, n:n + 1] - m)
            l_vec = l_vec + pb
            p_ref[0, :, g * page:(g + 1) * page] = pb
        l = jnp.sum(l_vec, axis=1, keepdims=True) + jnp.sum(p_own, axis=1, keepdims=True)
        linv_ref[0] = jnp.broadcast_to(1.0 / l, (rows, HEAD_DIM))
        for h in range(N_HEADS):
            vn = vnew_ref[:, h * HEAD_DIM:(h + 1) * HEAD_DIM].astype(BF16)
            oown_ref[0, h * dq:(h + 1) * dq, :] = jnp.dot(
                p_own[h * dq:(h + 1) * dq, :].astype(BF16), vn, preferred_element_type=F32)


def _moba_sample_v_kernel(pt_ref, used_ref, p_ref, oown_ref, linv_ref, cache_hbm, o_ref, buf, sem,
                          acc_scr, *, hp, page, n_pages):
    step = pl.program_id(1)
    n_steps = pl.num_programs(1)
    dq = o_ref.shape[0]
    ppb = MOBA_BLOCK // page
    n_blocks = n_pages // ppb

    @pl.when(jnp.logical_and(pl.program_id(0) == 0, step == 0))
    def _():
        buf[...] = jnp.zeros_like(buf)

    def wanted(b, blk, h):
        return used_ref[(b * N_HEADS + h) * n_blocks + blk] != 0

    @pl.when(step == 0)
    def _():
        acc_scr[...] = oown_ref[0]

    def compute(slot, first_page):
        for h in range(N_HEADS):
            tot = jnp.zeros((dq, HEAD_DIM), F32)
            for p in range(hp):
                col0 = (slot * hp + p) * page
                ph = p_ref[0, h * dq:(h + 1) * dq, col0:col0 + page].astype(BF16)
                tot = tot + jnp.dot(ph, buf[slot, p, h].astype(BF16), preferred_element_type=F32)
            acc_scr[h * dq:(h + 1) * dq, :] += tot

    _paged_slots(pt_ref, cache_hbm, buf, sem, hp, n_pages, ppb, compute, wanted)

    @pl.when(step == n_steps - 1)
    def _():
        for h in range(N_HEADS):
            rows_h = slice(h * dq, (h + 1) * dq)
            o_ref[:, h * HEAD_DIM:(h + 1) * HEAD_DIM] = acc_scr[rows_h, :] * linv_ref[0, rows_h, :]


def _moba_sample(q, k_new, v_new, cache_k, cache_v, page_table, hp=4, n_slots=4):
    dec_b, n_pages = page_table.shape
    page = cache_k.shape[1]
    dq = q.shape[0] // dec_b
    past = n_pages * page
    n_blocks = past // MOBA_BLOCK
    while n_pages % (n_slots * hp):
        n_slots //= 2
    n_steps = n_pages // (n_slots * hp)
    rows = N_HEADS * dq
    pt = page_table.reshape(-1)

    new_spec = pl.BlockSpec((dq, ATTN_WIDTH), lambda b, s, pt_ref: (b, 0))
    any_spec = pl.BlockSpec(memory_space=pl.ANY)
    stat_spec = pl.BlockSpec((1, rows, HEAD_DIM), lambda b, s, pt_ref: (b, 0, 0))
    stat_sds = jax.ShapeDtypeStruct((dec_b, rows, HEAD_DIM), F32)
    assert hp % (MOBA_BLOCK // page) == 0, "a page half must hold whole key blocks"
    page_buf = pltpu.VMEM((n_slots, hp, N_HEADS, page, HEAD_DIM), F32)
    probs, o_own, l_inv, used = pl.pallas_call(
        functools.partial(_moba_sample_k_kernel, hp=hp, page=page, n_blocks=n_blocks),
        grid_spec=pltpu.PrefetchScalarGridSpec(
            num_scalar_prefetch=1,
            grid=(dec_b, n_steps),
            in_specs=[new_spec, new_spec, new_spec, any_spec],
            out_specs=[pl.BlockSpec((1, rows, past), lambda b, s, pt_ref: (b, 0, 0)),
                       stat_spec, stat_spec,
                       pl.BlockSpec((1, N_HEADS, n_blocks), lambda b, s, pt_ref: (b, 0, 0))],
            scratch_shapes=[page_buf, pltpu.SemaphoreType.DMA((n_slots,)),
                            pltpu.VMEM((n_pages, rows, page), F32),
                            pltpu.VMEM((n_blocks, N_HEADS, 8, HEAD_DIM), F32),
                            pltpu.VMEM((ATTN_WIDTH, max(rows, HEAD_DIM)), BF16)]),
        out_shape=[jax.ShapeDtypeStruct((dec_b, rows, past), F32), stat_sds, stat_sds,
                   jax.ShapeDtypeStruct((dec_b, N_HEADS, n_blocks), jnp.int32)],
        compiler_params=_params("arbitrary", "arbitrary"),
        name="moba_sample_k",
    )(pt, q, k_new, v_new, cache_k)

    return pl.pallas_call(
        functools.partial(_moba_sample_v_kernel, hp=hp, page=page, n_pages=n_pages),
        grid_spec=pltpu.PrefetchScalarGridSpec(
            num_scalar_prefetch=2,
            grid=(dec_b, n_steps),
            in_specs=[pl.BlockSpec((1, rows, n_slots * hp * page), lambda b, s, *_: (b, 0, s)),
                      pl.BlockSpec((1, rows, HEAD_DIM), lambda b, s, *_: (b, 0, 0)),
                      pl.BlockSpec((1, rows, HEAD_DIM), lambda b, s, *_: (b, 0, 0)),
                      any_spec],
            out_specs=pl.BlockSpec((dq, ATTN_WIDTH), lambda b, s, *_: (b, 0)),
            scratch_shapes=[page_buf, pltpu.SemaphoreType.DMA((n_slots,)),
                            pltpu.VMEM((rows, HEAD_DIM), F32)]),
        out_shape=jax.ShapeDtypeStruct(q.shape, F32),
        compiler_params=_params("arbitrary", "arbitrary"),
        name="moba_sample_v",
    )(pt, used.reshape(-1), probs, o_own, l_inv, cache_v)


def _pool_kernel(u_ref, halo_ref, w_ref, ls_ref, o_ref, ext_scr, sum_a, sum_b, *, pos0):
    t = pl.program_id(1)
    gb, r, width = u_ref.shape
    grp = width // len(POOL_WINDOWS)
    h = POOL_HALO
    assert all(w == 2 << k for k, w in enumerate(POOL_WINDOWS)) and POOL_WINDOWS[-1] <= h

    @pl.when(t == 0)
    def _():
        for scr in (ext_scr, sum_a, sum_b):
            scr[:, 0:h, :] = jnp.zeros((gb, h, width), F32)
        ext_scr[:, h:2 * h, :] = halo_ref[...]

    @pl.when(t > 0)
    def _():
        ext_scr[:, h:2 * h, :] = ext_scr[:, r + h:r + 2 * h, :]

    ext_scr[:, 2 * h:2 * h + r, :] = u_ref[...]

    pos = pos0 + t * r + lax.broadcasted_iota(jnp.int32, (1, r, grp), 1)
    src, dst = ext_scr, sum_a
    for g, w in enumerate(POOL_WINDOWS):
        d = w // 2
        live = slice(g * grp, width)
        dst[:, h:2 * h + r, live] = (src[:, h:2 * h + r, live] + src[:, h - d:2 * h + r - d, live])
        cols = slice(g * grp, (g + 1) * grp)
        tot = dst[:, 2 * h:2 * h + r, cols]
        cnt = jnp.minimum(w, pos + 1).astype(F32)
        y = tot / cnt - u_ref[:, :, cols]
        y = jnp.dot(y.reshape(gb * r, grp).astype(BF16), w_ref[g], preferred_element_type=F32)
        o_ref[:, cols] = (y * ls_ref[:, cols]).astype(o_ref.dtype)
        src, dst = dst, (sum_b if dst is sum_a else sum_a)


def _pool(u, halo, w_pool_bf, ls_pool, n_seq, seq, gb, r, pos0):
    width = u.shape[1]
    n_g, n_t = n_seq // gb, seq // r
    u3 = u.reshape(n_seq, seq, width)
    return pl.pallas_call(
        functools.partial(_pool_kernel, pos0=pos0),
        grid=(n_g, n_t),
        in_specs=[pl.BlockSpec((gb, r, width), lambda g, t: (g, t, 0)),
                  pl.BlockSpec((gb, POOL_HALO, width), lambda g, t: (g, 0, 0)),
                  pl.BlockSpec(w_pool_bf.shape, lambda g, t: (0, 0, 0)),
                  pl.BlockSpec((1, width), lambda g, t: (0, 0))],
        out_specs=pl.BlockSpec((gb * r, width), lambda g, t: (g * n_t + t, 0)),
        out_shape=jax.ShapeDtypeStruct(u.shape, BF16),
        scratch_shapes=[pltpu.VMEM((gb, r + 2 * POOL_HALO, width), F32)] * 3,
        compiler_params=_params("arbitrary", "arbitrary"),
        name="pool_mixer",
    )(u3, halo, w_pool_bf, ls_pool.reshape(1, width))


def _bf16_tiles(w_refs, copy_refs):
    tiles = []
    for i, w_ref in enumerate(w_refs):
        w = w_ref[...]
        if w.dtype != BF16:
            w = w.astype(BF16)
            copy_refs[i][...] = w
        tiles.append(w)
    return tiles


def _merge_kernel(x_ref, gt_ref, h_ref, attn_ref, pool_ref,
                  wga_ref, wgb_ref, woa_ref, wob_ref, wout_ref, o_ref, *copy_refs):
    c = pl.program_id(2)

    @pl.when(c == 0)
    def _():
        o_ref[...] = jnp.zeros_like(o_ref)

    wga, wgb, woa, wob, wout = _bf16_tiles((wga_ref, wgb_ref, woa_ref, wob_ref, wout_ref), copy_refs)
    h = h_ref[...]
    g_a = jax.nn.sigmoid(jnp.dot(h, wga, preferred_element_type=F32))
    g_b = jax.nn.sigmoid(jnp.dot(h, wgb, preferred_element_type=F32))
    y_a = jnp.dot(attn_ref[...].astype(BF16), woa, preferred_element_type=F32)
    y_b = jnp.dot(pool_ref[...].astype(BF16), wob, preferred_element_type=F32)
    mix_in = (g_a * y_a + g_b * y_b).astype(BF16)
    o_ref[...] += jnp.dot(mix_in, wout, preferred_element_type=F32).reshape(o_ref.shape)

    @pl.when(c == pl.num_programs(2) - 1)
    def _():
        o_ref[...] = x_ref[...] + gt_ref[...] * o_ref[...]


def _column_weight_specs(weights, tn, n_c):
    in_specs, copy_specs, copy_shapes = [], [], []
    for w, axis, first in weights:
        if axis == 1:
            in_specs.append(pl.BlockSpec((w.shape[0], tn), lambda g, t, c, first=first: (0, first + c)))
            copy_specs.append(pl.BlockSpec((w.shape[0], tn), lambda g, t, c: (0, c)))
            copy_shapes.append(jax.ShapeDtypeStruct((w.shape[0], n_c * tn), BF16))
        else:
            in_specs.append(pl.BlockSpec((tn, w.shape[1]), lambda g, t, c, first=first: (first + c, 0)))
            copy_specs.append(pl.BlockSpec((tn, w.shape[1]), lambda g, t, c: (c, 0)))
            copy_shapes.append(jax.ShapeDtypeStruct((n_c * tn, w.shape[1]), BF16))
    if all(w.dtype == BF16 for w, _, _ in weights):
        return in_specs, [], []
    return in_specs, copy_specs, copy_shapes


def _merge(x3, mod, h, attn, pooled, weights, gb, r, tn):
    n_g, n_t, d = _group_specs(x3, gb, r)
    tm = gb * r
    aw = attn.shape[1]
    pw = pooled.shape[1]
    n_c = d // tn
    w_specs, copy_specs, copy_shapes = _column_weight_specs(weights, tn, n_c)
    if copy_specs:
        assert n_g * n_t == 1, "weight copies are written by a single token tile"
    x_spec = pl.BlockSpec((gb, r, d), lambda g, t, c: (g, t, 0))
    mod_spec = lambda k: pl.BlockSpec((gb, 1, d), lambda g, t, c: (g, 0, k))
    out, *copies = pl.pallas_call(
        _merge_kernel,
        grid=(n_g, n_t, n_c),
        in_specs=[x_spec, mod_spec(2),
                  pl.BlockSpec((tm, d), lambda g, t, c: (g * n_t + t, 0)),
                  pl.BlockSpec((tm, aw), lambda g, t, c: (g * n_t + t, 0)),
                  pl.BlockSpec((tm, pw), lambda g, t, c: (g * n_t + t, 0))] + w_specs,
        out_specs=[x_spec] + copy_specs,
        out_shape=[jax.ShapeDtypeStruct(x3.shape, F32)] + copy_shapes,
        compiler_params=_params("arbitrary", "arbitrary", "arbitrary"),
        name="merge_out",
    )(x3, mod, h, attn, pooled, *[w for w, _, _ in weights])
    return out, copies


def _ffn_kernel(x_ref, sh_ref, sc_ref, gt_ref, g2_ref, wa_ref, wb_ref, wo_ref, o_ref, *rest):
    *copy_refs, h_scr = rest
    c = pl.program_id(2)

    @pl.when(c == 0)
    def _():
        _mod_norm_store(x_ref, g2_ref, sc_ref, sh_ref, h_scr)
        o_ref[...] = jnp.zeros_like(o_ref)

    wa, wb, wo = _bf16_tiles((wa_ref, wb_ref, wo_ref), copy_refs)
    h = h_scr[...]
    a = jnp.dot(h, wa, preferred_element_type=F32)
    b = jnp.dot(h, wb, preferred_element_type=F32)
    hid = (jax.nn.silu(a) * b).astype(BF16)
    o_ref[...] += jnp.dot(hid, wo, preferred_element_type=F32).reshape(o_ref.shape)

    @pl.when(c == pl.num_programs(2) - 1)
    def _():
        o_ref[...] = x_ref[...] + gt_ref[...] * o_ref[...]


def _ffn(x3, mod, g_norm2, weights, hidden, gb, r, tn):
    n_g, n_t, d = _group_specs(x3, gb, r)
    tm = gb * r
    n_c = hidden // tn
    w_specs, copy_specs, copy_shapes = _column_weight_specs(weights, tn, n_c)
    if copy_specs:
        assert n_g * n_t == 1, "weight copies are written by a single token tile"
    x_spec = pl.BlockSpec((gb, r, d), lambda g, t, c: (g, t, 0))
    mod_spec = lambda k: pl.BlockSpec((gb, 1, d), lambda g, t, c: (g, 0, k))
    out, *copies = pl.pallas_call(
        _ffn_kernel,
        grid=(n_g, n_t, n_c),
        in_specs=[x_spec, mod_spec(3), mod_spec(4), mod_spec(5),
                  pl.BlockSpec((1, d), lambda g, t, c: (0, 0))] + w_specs,
        out_specs=[x_spec] + copy_specs,
        out_shape=[jax.ShapeDtypeStruct(x3.shape, F32)] + copy_shapes,
        scratch_shapes=[pltpu.VMEM((tm, d), BF16)],
        compiler_params=_params("arbitrary", "arbitrary", "arbitrary"),
        name="ffn_swiglu",
    )(x3, mod, mod, mod, g_norm2.reshape(1, d), *[w for w, _, _ in weights])
    return out, copies


def _rope_tables(pos):
    half = HEAD_DIM // 2
    inv = ROPE_THETA ** (-jnp.arange(half, dtype=F32) / half)
    ang = pos.astype(F32)[:, None] * inv[None, :]
    cos, sin = jnp.cos(ang), jnp.sin(ang)
    return jnp.concatenate([cos, cos], axis=-1), jnp.concatenate([-sin, sin], axis=-1)


def _pick_rows(seq, target=512):
    r = min(seq, target)
    while seq % r:
        r //= 2
    return r


def kernel(x_prompt, x_sample, c_prompt, c_sample, cache_k, cache_v, state_pool, page_table, w_ada, b_ada, g_norm1, g_norm2, w_in, g_qnorm, g_knorm, w_pool, ls_pool, w_o_attn, w_o_pool, w_out, w_ffn_in, w_ffn_out):
    n_p, seq, d = x_prompt.shape
    n_s, dq, _ = x_sample.shape
    past = page_table.shape[1] * cache_k.shape[1]
    pool_w = state_pool.shape[2]
    n_state = state_pool.shape[1]

    w_qkvu_bf = w_in[:, :3 * ATTN_WIDTH + pool_w].astype(BF16)
    w_pool_bf = w_pool.astype(BF16)
    hidden = w_ffn_out.shape[0]

    n_c = n_p + n_s
    c_all = jnp.concatenate([c_prompt, c_sample], axis=0)
    c_all = jnp.pad(c_all, ((0, (-n_c) % 8), (0, 0)))
    mod = _ada(c_all, w_ada, b_ada)
    mod_p = mod[:n_p].reshape(n_p, 1, 6 * d)
    mod_s = mod[n_p:n_c].reshape(n_s, 1, 6 * d)

    cos_s, sin_s = _rope_tables(past + jnp.arange(dq))
    cos_s, sin_s = jnp.tile(cos_s, (n_s, 1)), jnp.tile(sin_s, (n_s, 1))
    q_s, k_s, v_s, u_s, h_s = _qkvu(x_sample, mod_s, g_norm1, w_qkvu_bf, g_qnorm, g_knorm, cos_s, sin_s, n_s, dq)
    attn_s = _moba_sample(q_s, k_s, v_s, cache_k, cache_v, page_table)
    halo_s = jnp.pad(state_pool, ((0, 0), (POOL_HALO - n_state, 0), (0, 0)))
    pooled_s = _pool(u_s, halo_s, w_pool_bf, ls_pool, n_s, dq, n_s, dq, past)
    gate0 = (w_in.shape[1] - 2 * d) // _TN_SAMPLE
    x1_s, merge_bf = _merge(
        x_sample, mod_s, h_s, attn_s, pooled_s,
        ((w_in, 1, gate0), (w_in, 1, gate0 + d // _TN_SAMPLE), (w_o_attn, 1, 0), (w_o_pool, 1, 0),
         (w_out, 0, 0)), n_s, dq, _TN_SAMPLE)
    y_s, ffn_bf = _ffn(
        x1_s, mod_s, g_norm2,
        ((w_ffn_in, 1, 0), (w_ffn_in, 1, hidden // _TN_PROMPT), (w_ffn_out, 0, 0)),
        hidden, n_s, dq, _TN_PROMPT)

    r_p = _pick_rows(seq)
    cos_p, sin_p = _rope_tables(jnp.arange(seq))
    q_p, k_p, v_p, u_p, h_p = _qkvu(x_prompt, mod_p, g_norm1, w_qkvu_bf, g_qnorm, g_knorm, cos_p, sin_p, 1, r_p)
    attn_p = _moba_prompt(q_p, k_p, v_p, n_p, seq)
    pooled_p = _pool(u_p, jnp.zeros((n_p, POOL_HALO, pool_w), F32), w_pool_bf, ls_pool,
                     n_p, seq, 1, r_p, 0)
    x1_p, _ = _merge(x_prompt, mod_p, h_p, attn_p, pooled_p,
                     tuple((w, axis, 0) for w, axis in zip(merge_bf, (1, 1, 1, 1, 0))), 1, r_p, _TN_PROMPT)
    y_p, _ = _ffn(x1_p, mod_p, g_norm2, tuple((w, axis, 0) for w, axis in zip(ffn_bf, (1, 1, 0))),
                  hidden, 1, r_p, _TN_PROMPT)

    k_prompt = k_p.reshape(n_p, seq, N_HEADS, HEAD_DIM)
    v_prompt = v_p.reshape(n_p, seq, N_HEADS, HEAD_DIM)
    pool_prompt = u_p.reshape(n_p, seq, pool_w)[:, seq - n_state:]
    k_sample = k_s.reshape(n_s, dq, N_HEADS, HEAD_DIM)
    v_sample = v_s.reshape(n_s, dq, N_HEADS, HEAD_DIM)
    pool_sample = jnp.concatenate([state_pool, u_s.reshape(n_s, dq, pool_w)], axis=1)[:, -n_state:]
    return (y_p, y_s, k_prompt, v_prompt, pool_prompt, k_sample, v_sample, pool_sample)
```

```python
import functools

import jax
import jax.numpy as jnp
from jax import lax
from jax.experimental import pallas as pl
from jax.experimental.pallas import tpu as pltpu

N_HEADS = 8
HEAD_DIM = 128
ATTN_WIDTH = N_HEADS * HEAD_DIM
MOBA_BLOCK = 256
MOBA_TOPK = 3
ROPE_THETA = 10000.0
POOL_WINDOWS = (2, 4, 8, 16)
POOL_HALO = 16
RMS_EPS = 1e-6
_TILE_GROUP = 4
_MASKED_MAX = -1e30
_TN_PROMPT = 512
_TN_SAMPLE = 256

F32 = jnp.float32
BF16 = jnp.bfloat16
NEG_INF = float("-inf")
_LOG2_E = 1.4426950408889634

_VMEM_LIMIT = 52 * 1024 * 1024


def _params(*sem):
    return pltpu.CompilerParams(dimension_semantics=sem, vmem_limit_bytes=_VMEM_LIMIT)


_NORM_ROWS = 16


def _mod_norm_store(x_ref, g_ref, sc_ref, sh_ref, h_scr):
    gb, r, d = x_ref.shape
    g = g_ref[...]
    if r >= _NORM_ROWS:
        per_group = r // _NORM_ROWS
        chunks = [(gi, 1, c * _NORM_ROWS, _NORM_ROWS) for gi in range(gb) for c in range(per_group)]
    else:
        n_g = _NORM_ROWS // r
        chunks = [(gi, n_g, 0, r) for gi in range(0, gb, n_g)]
    gains = {}
    for gi, n_g, r0, nr in chunks:
        if gi not in gains:
            gains[gi] = g * (1.0 + sc_ref[gi:gi + n_g])
        x = x_ref[gi:gi + n_g, r0:r0 + nr, :]
        ms = jnp.mean(x * x, axis=-1, keepdims=True)
        h = x * lax.rsqrt(ms + RMS_EPS) * gains[gi] + sh_ref[gi:gi + n_g]
        row0 = gi * r + r0
        h_scr[row0:row0 + n_g * nr, :] = h.reshape(n_g * nr, d).astype(BF16)


def _ada_kernel(c_ref, w_ref, b_ref, o_ref):
    o_ref[...] = jnp.dot(c_ref[...].astype(BF16), w_ref[...].astype(BF16),
                         preferred_element_type=F32) + b_ref[...]


def _ada(c_all, w_ada, b_ada, tn=1024):
    m, d = c_all.shape
    n = w_ada.shape[1]
    return pl.pallas_call(
        _ada_kernel,
        grid=(n // tn,),
        in_specs=[pl.BlockSpec((m, d), lambda j: (0, 0)),
                  pl.BlockSpec((d, tn), lambda j: (0, j)),
                  pl.BlockSpec((1, tn), lambda j: (0, j))],
        out_specs=pl.BlockSpec((m, tn), lambda j: (0, j)),
        out_shape=jax.ShapeDtypeStruct((m, n), F32),
        compiler_params=_params("arbitrary"),
        name="ada_mod",
    )(c_all, w_ada, b_ada.reshape(1, n))


def _head_norm_rope(th, g, cos, sin_signed):
    ms = jnp.mean(th * th, axis=-1, keepdims=True)
    y = th * lax.rsqrt(ms + RMS_EPS) * g
    return y * cos + pltpu.roll(y, HEAD_DIM // 2, 1) * sin_signed


_PROJ_CHUNK = 2 * HEAD_DIM


def _qkvu_kernel(x_ref, sh_ref, sc_ref, g1_ref, w_ref, gq_ref, gk_ref, cos_ref, sin_ref,
                 q_ref, k_ref, v_ref, u_ref, h_ref):
    _mod_norm_store(x_ref, g1_ref, sc_ref, sh_ref, h_ref)

    width = q_ref.shape[1]
    outs = ((q_ref, gq_ref), (k_ref, gk_ref), (v_ref, None), (u_ref, None))
    for i, (o_ref, g_ref) in enumerate(outs):
        for c in range(0, width, _PROJ_CHUNK):
            col = i * width + c
            acc = jnp.dot(h_ref[...], w_ref[:, col:col + _PROJ_CHUNK], preferred_element_type=F32)
            if g_ref is None:
                o_ref[:, c:c + _PROJ_CHUNK] = acc
            else:
                for hh in range(0, _PROJ_CHUNK, HEAD_DIM):
                    o_ref[:, c + hh:c + hh + HEAD_DIM] = _head_norm_rope(
                        acc[:, hh:hh + HEAD_DIM], g_ref[...], cos_ref[...], sin_ref[...])


def _group_specs(x3, gb, r):
    g_total, s, d = x3.shape
    n_g, n_t = g_total // gb, s // r
    return n_g, n_t, d


def _qkvu(x3, mod, g_norm1, w_in_bf, g_q, g_k, cos, sin_signed, gb, r):
    n_g, n_t, d = _group_specs(x3, gb, r)
    tm = gb * r
    tokens = x3.shape[0] * x3.shape[1]
    wq = ATTN_WIDTH
    x_spec = pl.BlockSpec((gb, r, d), lambda g, t: (g, t, 0))
    mod_spec = lambda k: pl.BlockSpec((gb, 1, d), lambda g, t: (g, 0, k))
    row_spec = pl.BlockSpec((1, d), lambda g, t: (0, 0))
    head_spec = pl.BlockSpec((1, HEAD_DIM), lambda g, t: (0, 0))
    tab_spec = pl.BlockSpec((tm, HEAD_DIM), lambda g, t: (t, 0))
    out_spec = pl.BlockSpec((tm, wq), lambda g, t: (g * n_t + t, 0))
    out_sds = jax.ShapeDtypeStruct((tokens, wq), F32)
    w_spec = pl.BlockSpec((d, 4 * wq), lambda g, t: (0, 0), pipeline_mode=pl.Buffered(1))
    return pl.pallas_call(
        _qkvu_kernel,
        grid=(n_g, n_t),
        in_specs=[x_spec, mod_spec(0), mod_spec(1), row_spec, w_spec,
                  head_spec, head_spec, tab_spec, tab_spec],
        out_specs=[out_spec] * 4 + [pl.BlockSpec((tm, d), lambda g, t: (g * n_t + t, 0))],
        out_shape=[out_sds] * 4 + [jax.ShapeDtypeStruct((tokens, d), BF16)],
        compiler_params=_params("arbitrary", "arbitrary"),
        name="qkvu_proj",
    )(x3, mod, mod, g_norm1.reshape(1, d), w_in_bf, g_q.reshape(1, HEAD_DIM),
      g_k.reshape(1, HEAD_DIM), cos, sin_signed)


def _topk_select(gate, valid, axis):
    n = gate.shape[axis]
    g = jnp.where(valid, gate, NEG_INF)
    idx = lax.broadcasted_iota(jnp.int32, gate.shape, axis)
    rank = jnp.zeros(gate.shape, jnp.int32)
    for m in range(n):
        gm = g[m:m + 1, :] if axis == 0 else g[:, m:m + 1]
        beats = (gm > g) | ((gm == g) & (m < idx))
        rank = rank + beats.astype(jnp.int32)
    return valid & (rank < MOBA_TOPK) & (jnp.abs(g) < float("inf"))


def _moba_prompt_kernel(q_ref, k_ref, v_ref, o_ref, kb_scr, vt_scr, kmean_scr, bias_scr, tri_scr, s_scr,
                        *, nb):
    blk = MOBA_BLOCK

    def prepare():
        for n in range(nb):
            kn = k_ref[n * blk:(n + 1) * blk, :]
            kb_scr[n * blk:(n + 1) * blk, :] = kn.astype(BF16)
            kmean_scr[n:n + 1, :] = jnp.mean(kn, axis=0, keepdims=True)
            vt_scr[n] = v_ref[n * blk:(n + 1) * blk, :].T.astype(BF16)
        kb_scr[nb * blk:, :] = jnp.zeros((kb_scr.shape[0] - nb * blk, HEAD_DIM), BF16)
        kmean = kmean_scr[...]
        for qb_i in range(nb):
            gate = lax.dot_general(kmean, q_ref[qb_i * blk:(qb_i + 1) * blk, :],
                                   (((1,), (1,)), ((), ())),
                                   precision=lax.Precision.HIGHEST, preferred_element_type=F32)
            n_idx = lax.broadcasted_iota(jnp.int32, gate.shape, 0)
            sel = _topk_select(gate, n_idx < qb_i, axis=0)
            bias_scr[qb_i, 0:nb, :] = jnp.where(sel | (n_idx == qb_i), 0.0, NEG_INF)
            bias_scr[qb_i, nb:, :] = jnp.full((bias_scr.shape[1] - nb, blk), NEG_INF, F32)
        kpos = lax.broadcasted_iota(jnp.int32, (blk, blk), 0)
        qpos = lax.broadcasted_iota(jnp.int32, (blk, blk), 1)
        tri_scr[0] = jnp.zeros((blk, blk), F32)
        tri_scr[1] = jnp.where(kpos <= qpos, 0.0, NEG_INF)

    prepare()

    exp2_scale = HEAD_DIM ** -0.5 * _LOG2_E
    width = _TILE_GROUP

    def fold(x, op):
        return op(x.reshape(blk // 8, 8, blk), axis=0)

    def queries(qi):
        return q_ref[pl.ds(pl.multiple_of(qi * blk, blk), blk), :].astype(BF16)

    def score_group(qi, qb, g, slot):
        j0 = g * width
        keys = kb_scr[pl.ds(pl.multiple_of(j0 * blk, blk), width * blk), :]
        s_all = lax.dot_general(keys, qb, (((1,), (1,)), ((), ())), preferred_element_type=F32)
        m8 = jnp.full((8, blk), NEG_INF, F32)
        for t in range(width):
            j = j0 + t
            s = (s_all[t * blk:(t + 1) * blk, :] + bias_scr[qi, pl.ds(j, 1), :]
                 + tri_scr[jnp.asarray(j == qi, jnp.int32)])
            s_scr[slot, t] = s
            m8 = jnp.maximum(m8, fold(s, jnp.max))
        return m8

    def absorb_group(g, slot, m_run, l8, acc, m8):
        m_new = jnp.maximum(m_run, jnp.max(m8, axis=0, keepdims=True))
        alpha = jnp.exp2((m_run - m_new) * exp2_scale)
        l8 = l8 * alpha
        acc = acc * alpha
        for t in range(width):
            p = jnp.exp2((s_scr[slot, t] - m_new) * exp2_scale)
            l8 = l8 + fold(p, jnp.sum)
            acc = acc + jnp.dot(vt_scr[jnp.minimum(g * width + t, nb - 1)], p.astype(BF16),
                                preferred_element_type=F32)
        return m_new, l8, acc

    def one_query_block(qi, state):
        m8_first, first_slot = state
        qb = queries(qi)
        n_groups = (qi + width) // width

        def body(g, carry):
            m_run, l8, acc, m8 = carry
            m_run, l8, acc = absorb_group(g, (first_slot + g) % 2, m_run, l8, acc, m8)
            return m_run, l8, acc, score_group(qi, qb, g + 1, (first_slot + g + 1) % 2)

        carry = (jnp.full((1, blk), _MASKED_MAX, F32), jnp.zeros((8, blk), F32),
                 jnp.zeros((HEAD_DIM, blk), F32), m8_first)
        m_run, l8, acc, m8 = lax.fori_loop(0, n_groups - 1, body, carry)
        last_slot = (first_slot + n_groups - 1) % 2
        _, l8, acc = absorb_group(n_groups - 1, last_slot, m_run, l8, acc, m8)
        nxt = jnp.minimum(qi + 1, nb - 1)
        m8_next = score_group(nxt, queries(nxt), 0, 1 - last_slot)
        l = jnp.sum(l8, axis=0, keepdims=True)
        o_ref[pl.ds(pl.multiple_of(qi * blk, blk), blk), :] = (acc / l).T.astype(o_ref.dtype)
        return m8_next, 1 - last_slot

    lax.fori_loop(0, nb, one_query_block, (score_group(0, queries(0), 0, 0), jnp.int32(0)))


def _moba_prompt(q, k, v, n_seq, seq):
    nb = seq // MOBA_BLOCK
    nb_pad = -(-nb // _TILE_GROUP) * _TILE_GROUP + _TILE_GROUP
    blk = MOBA_BLOCK
    head_spec = pl.BlockSpec((seq, HEAD_DIM), lambda b, h: (b, h))
    return pl.pallas_call(
        functools.partial(_moba_prompt_kernel, nb=nb),
        grid=(n_seq, N_HEADS),
        in_specs=[head_spec, head_spec, head_spec],
        out_specs=head_spec,
        out_shape=jax.ShapeDtypeStruct(q.shape, BF16),
        scratch_shapes=[pltpu.VMEM((nb_pad * blk, HEAD_DIM), BF16),
                        pltpu.VMEM((nb, HEAD_DIM, blk), BF16),
                        pltpu.VMEM((nb, HEAD_DIM), F32),
                        pltpu.VMEM((nb, nb_pad, blk), F32),
                        pltpu.VMEM((2, blk, blk), F32),
                        pltpu.VMEM((2, _TILE_GROUP, blk, blk), F32)],
        compiler_params=_params("arbitrary", "arbitrary"),
        name="moba_prompt",
    )(q, k, v)


def _page_copy_groups(cache_hbm, pt_ref, buf, sem, slot, b, first_page, hp, n_pages, ppb, wanted):
    groups = []
    for p0 in range(0, hp, ppb):
        pgs = [pt_ref[b * n_pages + first_page + p0 + i] for i in range(ppb)]
        for h in range(N_HEADS):
            pred = None if wanted is None else wanted(b, (first_page + p0) // ppb, h)
            groups.append((pred, [pltpu.make_async_copy(cache_hbm.at[pgs[i], :, h, :],
                                                        buf.at[slot, p0 + i, h], sem.at[slot])
                                  for i in range(ppb)]))
    return groups


def _for_each_copy(groups, method):
    for pred, copies in groups:
        def run(copies=copies):
            for c in copies:
                getattr(c, method)()
        if pred is None:
            run()
        else:
            pl.when(pred)(run)


def _paged_slots(pt_ref, cache_hbm, buf, sem, hp, n_pages, ppb, compute, wanted=None):
    n_slots = buf.shape[0]
    b, step = pl.program_id(0), pl.program_id(1)
    n_b, n_steps = pl.num_programs(0), pl.num_programs(1)
    is_first = jnp.logical_and(b == 0, step == 0)
    is_last = jnp.logical_and(b == n_b - 1, step == n_steps - 1)
    wrap = step == n_steps - 1
    nxt_b = jnp.where(wrap, b + 1, b)
    nxt_step = jnp.where(wrap, 0, step + 1)

    def groups(bb, ss, slot):
        return _page_copy_groups(cache_hbm, pt_ref, buf, sem, slot, bb, (n_slots * ss + slot) * hp,
                                 hp, n_pages, ppb, wanted)

    @pl.when(is_first)
    def _():
        for slot in range(n_slots):
            _for_each_copy(groups(b, step, slot), "start")

    for slot in range(n_slots):
        _for_each_copy(groups(b, step, slot), "wait")
        compute(slot, (n_slots * step + slot) * hp)

        @pl.when(jnp.logical_not(is_last))
        def _():
            _for_each_copy(groups(nxt_b, nxt_step, slot), "start")


def _moba_sample_k_kernel(pt_ref, q_ref, knew_ref, vnew_ref, cache_hbm, p_ref, oown_ref, linv_ref,
                          used_ref, buf, sem, s_scr, kpart_scr, qbd_scr, *, hp, page, n_blocks):
    step = pl.program_id(1)
    n_steps = pl.num_programs(1)
    dq = q_ref.shape[0]
    rows = N_HEADS * dq
    scale = HEAD_DIM ** -0.5
    pages_per_block = MOBA_BLOCK // page
    n_pages = n_blocks * pages_per_block

    qh = [q_ref[:, h * HEAD_DIM:(h + 1) * HEAD_DIM] for h in range(N_HEADS)]
    qh_bf = [x.astype(BF16) for x in qh]

    @pl.when(step == 0)
    def _():
        kpart_scr[...] = jnp.zeros_like(kpart_scr)
        q_rep = jnp.concatenate([q_ref[...]] * N_HEADS
                                + [jnp.zeros((qbd_scr.shape[1] - rows, ATTN_WIDTH), F32)], axis=0)
        row_head = lax.broadcasted_iota(jnp.int32, q_rep.shape, 0) // dq
        lane_head = lax.broadcasted_iota(jnp.int32, q_rep.shape, 1) // HEAD_DIM
        qbd_scr[...] = jnp.where(row_head == lane_head, q_rep, 0.0).T.astype(BF16)

    def compute(slot, first_page):
        heads = [[buf[slot, p, h] for h in range(N_HEADS)] for p in range(hp)]
        for p in range(hp):
            blk = (first_page + p) // pages_per_block
            for h in range(N_HEADS):
                kpart_scr[blk, h] += jnp.sum(heads[p][h].reshape(page // 8, 8, HEAD_DIM), axis=0)
        for p0 in range(0, hp, hp // 2):
            keys = jnp.concatenate([jnp.concatenate(hs, axis=1) for hs in heads[p0:p0 + hp // 2]],
                                   axis=0)
            s_t = jnp.dot(keys.astype(BF16), qbd_scr[...], preferred_element_type=F32) * scale
            for i in range(hp // 2):
                s_scr[first_page + p0 + i] = s_t[i * page:(i + 1) * page, :].T[:rows, :]

    _paged_slots(pt_ref, cache_hbm, buf, sem, hp, n_pages, pages_per_block, compute)

    @pl.when(step == n_steps - 1)
    def _():
        gates = []
        for h in range(N_HEADS):
            kmean_h = jnp.sum(kpart_scr[:, h], axis=1) * (1.0 / MOBA_BLOCK)
            gates.append(lax.dot_general(qh[h], kmean_h, (((1,), (1,)), ((), ())),
                                         precision=lax.Precision.HIGHEST,
                                         preferred_element_type=F32))
        gate = jnp.concatenate(gates, axis=0)
        sel = _topk_select(gate, jnp.ones(gate.shape, jnp.bool_), axis=1)
        bias = jnp.where(sel, 0.0, NEG_INF)
        used_ref[0] = jnp.concatenate(
            [jnp.max(sel[h * dq:(h + 1) * dq, :].astype(jnp.int32), axis=0, keepdims=True)
             for h in range(N_HEADS)], axis=0)

        s_own = []
        for h in range(N_HEADS):
            kn = knew_ref[:, h * HEAD_DIM:(h + 1) * HEAD_DIM].astype(BF16)
            s_own.append(lax.dot_general(qh_bf[h], kn, (((1,), (1,)), ((), ())),
                                         preferred_element_type=F32) * scale)
        s_own = jnp.concatenate(s_own, axis=0)
        row = lax.broadcasted_iota(jnp.int32, s_own.shape, 0) % dq
        col = lax.broadcasted_iota(jnp.int32, s_own.shape, 1)
        s_own = jnp.where(col <= row, s_own, NEG_INF)

        m_vec = jnp.full((rows, page), NEG_INF, F32)
        for g in range(n_pages):
            n = g // pages_per_block
            m_vec = jnp.maximum(m_vec, s_scr[g] + bias[:, n:n + 1])
        m = jnp.maximum(jnp.max(m_vec, axis=1, keepdims=True), jnp.max(s_own, axis=1, keepdims=True))
        p_own = jnp.exp(s_own - m)
        l_vec = jnp.zeros((rows, page), F32)
        for g in range(n_pages):
            n = g // pages_per_block
            pb = jnp.exp(s_scr[g] + bias[:, n:n + 1] - m)
            l_vec = l_vec + pb
            p_ref[0, :, g * page:(g + 1) * page] = pb
        l = jnp.sum(l_vec, axis=1, keepdims=True) + jnp.sum(p_own, axis=1, keepdims=True)
        linv_ref[0] = jnp.broadcast_to(1.0 / l, (rows, HEAD_DIM))
        for h in range(N_HEADS):
            vn = vnew_ref[:, h * HEAD_DIM:(h + 1) * HEAD_DIM].astype(BF16)
            oown_ref[0, h * dq:(h + 1) * dq, :] = jnp.dot(
                p_own[h * dq:(h + 1) * dq, :].astype(BF16), vn, preferred_element_type=F32)


def _moba_sample_v_kernel(pt_ref, used_ref, p_ref, oown_ref, linv_ref, cache_hbm, o_ref, buf, sem,
                          acc_scr, *, hp, page, n_pages):
    step = pl.program_id(1)
    n_steps = pl.num_programs(1)
    dq = o_ref.shape[0]
    ppb = MOBA_BLOCK // page
    n_blocks = n_pages // ppb

    @pl.when(jnp.logical_and(pl.program_id(0) == 0, step == 0))
    def _():
        buf[...] = jnp.zeros_like(buf)

    def wanted(b, blk, h):
        return used_ref[(b * N_HEADS + h) * n_blocks + blk] != 0

    @pl.when(step == 0)
    def _():
        acc_scr[...] = oown_ref[0]

    def compute(slot, first_page):
        for h in range(N_HEADS):
            tot = jnp.zeros((dq, HEAD_DIM), F32)
            for p in range(hp):
                col0 = (slot * hp + p) * page
                ph = p_ref[0, h * dq:(h + 1) * dq, col0:col0 + page].astype(BF16)
                tot = tot + jnp.dot(ph, buf[slot, p, h].astype(BF16), preferred_element_type=F32)
            acc_scr[h * dq:(h + 1) * dq, :] += tot

    _paged_slots(pt_ref, cache_hbm, buf, sem, hp, n_pages, ppb, compute, wanted)

    @pl.when(step == n_steps - 1)
    def _():
        for h in range(N_HEADS):
            rows_h = slice(h * dq, (h + 1) * dq)
            o_ref[:, h * HEAD_DIM:(h + 1) * HEAD_DIM] = acc_scr[rows_h, :] * linv_ref[0, rows_h, :]


def _moba_sample(q, k_new, v_new, cache_k, cache_v, page_table, hp=4, n_slots=4):
    dec_b, n_pages = page_table.shape
    page = cache_k.shape[1]
    dq = q.shape[0] // dec_b
    past = n_pages * page
    n_blocks = past // MOBA_BLOCK
    while n_pages % (n_slots * hp):
        n_slots //= 2
    n_steps = n_pages // (n_slots * hp)
    rows = N_HEADS * dq
    pt = page_table.reshape(-1)

    new_spec = pl.BlockSpec((dq, ATTN_WIDTH), lambda b, s, pt_ref: (b, 0))
    any_spec = pl.BlockSpec(memory_space=pl.ANY)
    stat_spec = pl.BlockSpec((1, rows, HEAD_DIM), lambda b, s, pt_ref: (b, 0, 0))
    stat_sds = jax.ShapeDtypeStruct((dec_b, rows, HEAD_DIM), F32)
    assert hp % (MOBA_BLOCK // page) == 0, "a page half must hold whole key blocks"
    page_buf = pltpu.VMEM((n_slots, hp, N_HEADS, page, HEAD_DIM), F32)
    probs, o_own, l_inv, used = pl.pallas_call(
        functools.partial(_moba_sample_k_kernel, hp=hp, page=page, n_blocks=n_blocks),
        grid_spec=pltpu.PrefetchScalarGridSpec(
            num_scalar_prefetch=1,
            grid=(dec_b, n_steps),
            in_specs=[new_spec, new_spec, new_spec, any_spec],
            out_specs=[pl.BlockSpec((1, rows, past), lambda b, s, pt_ref: (b, 0, 0)),
                       stat_spec, stat_spec,
                       pl.BlockSpec((1, N_HEADS, n_blocks), lambda b, s, pt_ref: (b, 0, 0))],
            scratch_shapes=[page_buf, pltpu.SemaphoreType.DMA((n_slots,)),
                            pltpu.VMEM((n_pages, rows, page), F32),
                            pltpu.VMEM((n_blocks, N_HEADS, 8, HEAD_DIM), F32),
                            pltpu.VMEM((ATTN_WIDTH, max(rows, HEAD_DIM)), BF16)]),
        out_shape=[jax.ShapeDtypeStruct((dec_b, rows, past), F32), stat_sds, stat_sds,
                   jax.ShapeDtypeStruct((dec_b, N_HEADS, n_blocks), jnp.int32)],
        compiler_params=_params("arbitrary", "arbitrary"),
        name="moba_sample_k",
    )(pt, q, k_new, v_new, cache_k)

    v_hp = max(hp // 2, MOBA_BLOCK // page)
    v_slots = n_slots * hp // v_hp
    return pl.pallas_call(
        functools.partial(_moba_sample_v_kernel, hp=v_hp, page=page, n_pages=n_pages),
        grid_spec=pltpu.PrefetchScalarGridSpec(
            num_scalar_prefetch=2,
            grid=(dec_b, n_steps),
            in_specs=[pl.BlockSpec((1, rows, n_slots * hp * page), lambda b, s, *_: (b, 0, s)),
                      pl.BlockSpec((1, rows, HEAD_DIM), lambda b, s, *_: (b, 0, 0)),
                      pl.BlockSpec((1, rows, HEAD_DIM), lambda b, s, *_: (b, 0, 0)),
                      any_spec],
            out_specs=pl.BlockSpec((dq, ATTN_WIDTH), lambda b, s, *_: (b, 0)),
            scratch_shapes=[pltpu.VMEM((v_slots, v_hp, N_HEADS, page, HEAD_DIM), F32),
                            pltpu.SemaphoreType.DMA((v_slots,)),
                            pltpu.VMEM((rows, HEAD_DIM), F32)]),
        out_shape=jax.ShapeDtypeStruct(q.shape, F32),
        compiler_params=_params("arbitrary", "arbitrary"),
        name="moba_sample_v",
    )(pt, used.reshape(-1), probs, o_own, l_inv, cache_v)


def _pool_kernel(u_ref, halo_ref, w_ref, ls_ref, o_ref, ext_scr, sum_a, sum_b, *, pos0):
    t = pl.program_id(1)
    gb, r, width = u_ref.shape
    grp = width // len(POOL_WINDOWS)
    h = POOL_HALO
    assert all(w == 2 << k for k, w in enumerate(POOL_WINDOWS)) and POOL_WINDOWS[-1] <= h

    @pl.when(t == 0)
    def _():
        for scr in (ext_scr, sum_a, sum_b):
            scr[:, 0:h, :] = jnp.zeros((gb, h, width), F32)
        ext_scr[:, h:2 * h, :] = halo_ref[...]

    @pl.when(t > 0)
    def _():
        ext_scr[:, h:2 * h, :] = ext_scr[:, r + h:r + 2 * h, :]

    ext_scr[:, 2 * h:2 * h + r, :] = u_ref[...]

    pos = pos0 + t * r + lax.broadcasted_iota(jnp.int32, (1, r, grp), 1)
    src, dst = ext_scr, sum_a
    for g, w in enumerate(POOL_WINDOWS):
        d = w // 2
        live = slice(g * grp, width)
        dst[:, h:2 * h + r, live] = (src[:, h:2 * h + r, live] + src[:, h - d:2 * h + r - d, live])
        cols = slice(g * grp, (g + 1) * grp)
        tot = dst[:, 2 * h:2 * h + r, cols]
        cnt = jnp.minimum(w, pos + 1).astype(F32)
        y = tot / cnt - u_ref[:, :, cols]
        y = jnp.dot(y.reshape(gb * r, grp).astype(BF16), w_ref[g], preferred_element_type=F32)
        o_ref[:, cols] = (y * ls_ref[:, cols]).astype(o_ref.dtype)
        src, dst = dst, (sum_b if dst is sum_a else sum_a)


def _pool(u, halo, w_pool_bf, ls_pool, n_seq, seq, gb, r, pos0):
    width = u.shape[1]
    n_g, n_t = n_seq // gb, seq // r
    u3 = u.reshape(n_seq, seq, width)
    return pl.pallas_call(
        functools.partial(_pool_kernel, pos0=pos0),
        grid=(n_g, n_t),
        in_specs=[pl.BlockSpec((gb, r, width), lambda g, t: (g, t, 0)),
                  pl.BlockSpec((gb, POOL_HALO, width), lambda g, t: (g, 0, 0)),
                  pl.BlockSpec(w_pool_bf.shape, lambda g, t: (0, 0, 0)),
                  pl.BlockSpec((1, width), lambda g, t: (0, 0))],
        out_specs=pl.BlockSpec((gb * r, width), lambda g, t: (g * n_t + t, 0)),
        out_shape=jax.ShapeDtypeStruct(u.shape, BF16),
        scratch_shapes=[pltpu.VMEM((gb, r + 2 * POOL_HALO, width), F32)] * 3,
        compiler_params=_params("arbitrary", "arbitrary"),
        name="pool_mixer",
    )(u3, halo, w_pool_bf, ls_pool.reshape(1, width))


def _bf16_tiles(w_refs, copy_refs):
    tiles = []
    for i, w_ref in enumerate(w_refs):
        w = w_ref[...]
        if w.dtype != BF16:
            w = w.astype(BF16)
            copy_refs[i][...] = w
        tiles.append(w)
    return tiles


def _merge_kernel(x_ref, gt_ref, h_ref, attn_ref, pool_ref,
                  wga_ref, wgb_ref, woa_ref, wob_ref, wout_ref, o_ref, *copy_refs):
    c = pl.program_id(2)

    @pl.when(c == 0)
    def _():
        o_ref[...] = jnp.zeros_like(o_ref)

    wga, wgb, woa, wob, wout = _bf16_tiles((wga_ref, wgb_ref, woa_ref, wob_ref, wout_ref), copy_refs)
    h = h_ref[...]
    g_a = jax.nn.sigmoid(jnp.dot(h, wga, preferred_element_type=F32))
    g_b = jax.nn.sigmoid(jnp.dot(h, wgb, preferred_element_type=F32))
    y_a = jnp.dot(attn_ref[...].astype(BF16), woa, preferred_element_type=F32)
    y_b = jnp.dot(pool_ref[...].astype(BF16), wob, preferred_element_type=F32)
    mix_in = (g_a * y_a + g_b * y_b).astype(BF16)
    o_ref[...] += jnp.dot(mix_in, wout, preferred_element_type=F32).reshape(o_ref.shape)

    @pl.when(c == pl.num_programs(2) - 1)
    def _():
        o_ref[...] = x_ref[...] + gt_ref[...] * o_ref[...]


def _column_weight_specs(weights, tn, n_c):
    in_specs, copy_specs, copy_shapes = [], [], []
    for w, axis, first in weights:
        if axis == 1:
            in_specs.append(pl.BlockSpec((w.shape[0], tn), lambda g, t, c, first=first: (0, first + c)))
            copy_specs.append(pl.BlockSpec((w.shape[0], tn), lambda g, t, c: (0, c)))
            copy_shapes.append(jax.ShapeDtypeStruct((w.shape[0], n_c * tn), BF16))
        else:
            in_specs.append(pl.BlockSpec((tn, w.shape[1]), lambda g, t, c, first=first: (first + c, 0)))
            copy_specs.append(pl.BlockSpec((tn, w.shape[1]), lambda g, t, c: (c, 0)))
            copy_shapes.append(jax.ShapeDtypeStruct((n_c * tn, w.shape[1]), BF16))
    if all(w.dtype == BF16 for w, _, _ in weights):
        return in_specs, [], []
    return in_specs, copy_specs, copy_shapes


def _merge(x3, mod, h, attn, pooled, weights, gb, r, tn):
    n_g, n_t, d = _group_specs(x3, gb, r)
    tm = gb * r
    aw = attn.shape[1]
    pw = pooled.shape[1]
    n_c = d // tn
    w_specs, copy_specs, copy_shapes = _column_weight_specs(weights, tn, n_c)
    if copy_specs:
        assert n_g * n_t == 1, "weight copies are written by a single token tile"
    x_spec = pl.BlockSpec((gb, r, d), lambda g, t, c: (g, t, 0))
    mod_spec = lambda k: pl.BlockSpec((gb, 1, d), lambda g, t, c: (g, 0, k))
    out, *copies = pl.pallas_call(
        _merge_kernel,
        grid=(n_g, n_t, n_c),
        in_specs=[x_spec, mod_spec(2),
                  pl.BlockSpec((tm, d), lambda g, t, c: (g * n_t + t, 0)),
                  pl.BlockSpec((tm, aw), lambda g, t, c: (g * n_t + t, 0)),
                  pl.BlockSpec((tm, pw), lambda g, t, c: (g * n_t + t, 0))] + w_specs,
        out_specs=[x_spec] + copy_specs,
        out_shape=[jax.ShapeDtypeStruct(x3.shape, F32)] + copy_shapes,
        compiler_params=_params("arbitrary", "arbitrary", "arbitrary"),
        name="merge_out",
    )(x3, mod, h, attn, pooled, *[w for w, _, _ in weights])
    return out, copies


def _ffn_kernel(x_ref, sh_ref, sc_ref, gt_ref, g2_ref, wa_ref, wb_ref, wo_ref, o_ref, *rest):
    *copy_refs, h_scr = rest
    c = pl.program_id(2)

    @pl.when(c == 0)
    def _():
        _mod_norm_store(x_ref, g2_ref, sc_ref, sh_ref, h_scr)
        o_ref[...] = jnp.zeros_like(o_ref)

    wa, wb, wo = _bf16_tiles((wa_ref, wb_ref, wo_ref), copy_refs)
    h = h_scr[...]
    a = jnp.dot(h, wa, preferred_element_type=F32)
    b = jnp.dot(h, wb, preferred_element_type=F32)
    hid = (jax.nn.silu(a) * b).astype(BF16)
    o_ref[...] += jnp.dot(hid, wo, preferred_element_type=F32).reshape(o_ref.shape)

    @pl.when(c == pl.num_programs(2) - 1)
    def _():
        o_ref[...] = x_ref[...] + gt_ref[...] * o_ref[...]


def _ffn(x3, mod, g_norm2, weights, hidden, gb, r, tn):
    n_g, n_t, d = _group_specs(x3, gb, r)
    tm = gb * r
    n_c = hidden // tn
    w_specs, copy_specs, copy_shapes = _column_weight_specs(weights, tn, n_c)
    if copy_specs:
        assert n_g * n_t == 1, "weight copies are written by a single token tile"
    x_spec = pl.BlockSpec((gb, r, d), lambda g, t, c: (g, t, 0))
    mod_spec = lambda k: pl.BlockSpec((gb, 1, d), lambda g, t, c: (g, 0, k))
    out, *copies = pl.pallas_call(
        _ffn_kernel,
        grid=(n_g, n_t, n_c),
        in_specs=[x_spec, mod_spec(3), mod_spec(4), mod_spec(5),
                  pl.BlockSpec((1, d), lambda g, t, c: (0, 0))] + w_specs,
        out_specs=[x_spec] + copy_specs,
        out_shape=[jax.ShapeDtypeStruct(x3.shape, F32)] + copy_shapes,
        scratch_shapes=[pltpu.VMEM((tm, d), BF16)],
        compiler_params=_params("arbitrary", "arbitrary", "arbitrary"),
        name="ffn_swiglu",
    )(x3, mod, mod, mod, g_norm2.reshape(1, d), *[w for w, _, _ in weights])
    return out, copies


def _rope_tables(pos):
    half = HEAD_DIM // 2
    inv = ROPE_THETA ** (-jnp.arange(half, dtype=F32) / half)
    ang = pos.astype(F32)[:, None] * inv[None, :]
    cos, sin = jnp.cos(ang), jnp.sin(ang)
    return jnp.concatenate([cos, cos], axis=-1), jnp.concatenate([-sin, sin], axis=-1)


def _pick_rows(seq, target=512):
    r = min(seq, target)
    while seq % r:
        r //= 2
    return r


def kernel(x_prompt, x_sample, c_prompt, c_sample, cache_k, cache_v, state_pool, page_table, w_ada, b_ada, g_norm1, g_norm2, w_in, g_qnorm, g_knorm, w_pool, ls_pool, w_o_attn, w_o_pool, w_out, w_ffn_in, w_ffn_out):
    n_p, seq, d = x_prompt.shape
    n_s, dq, _ = x_sample.shape
    past = page_table.shape[1] * cache_k.shape[1]
    pool_w = state_pool.shape[2]
    n_state = state_pool.shape[1]

    w_qkvu_bf = w_in[:, :3 * ATTN_WIDTH + pool_w].astype(BF16)
    w_pool_bf = w_pool.astype(BF16)
    hidden = w_ffn_out.shape[0]

    n_c = n_p + n_s
    c_all = jnp.concatenate([c_prompt, c_sample], axis=0)
    c_all = jnp.pad(c_all, ((0, (-n_c) % 8), (0, 0)))
    mod = _ada(c_all, w_ada, b_ada)
    mod_p = mod[:n_p].reshape(n_p, 1, 6 * d)
    mod_s = mod[n_p:n_c].reshape(n_s, 1, 6 * d)

    cos_s, sin_s = _rope_tables(past + jnp.arange(dq))
    cos_s, sin_s = jnp.tile(cos_s, (n_s, 1)), jnp.tile(sin_s, (n_s, 1))
    q_s, k_s, v_s, u_s, h_s = _qkvu(x_sample, mod_s, g_norm1, w_qkvu_bf, g_qnorm, g_knorm, cos_s, sin_s, n_s, dq)
    attn_s = _moba_sample(q_s, k_s, v_s, cache_k, cache_v, page_table)
    halo_s = jnp.pad(state_pool, ((0, 0), (POOL_HALO - n_state, 0), (0, 0)))
    pooled_s = _pool(u_s, halo_s, w_pool_bf, ls_pool, n_s, dq, n_s, dq, past)
    gate0 = (w_in.shape[1] - 2 * d) // _TN_SAMPLE
    x1_s, merge_bf = _merge(
        x_sample, mod_s, h_s, attn_s, pooled_s,
        ((w_in, 1, gate0), (w_in, 1, gate0 + d // _TN_SAMPLE), (w_o_attn, 1, 0), (w_o_pool, 1, 0),
         (w_out, 0, 0)), n_s, dq, _TN_SAMPLE)
    y_s, ffn_bf = _ffn(
        x1_s, mod_s, g_norm2,
        ((w_ffn_in, 1, 0), (w_ffn_in, 1, hidden // _TN_PROMPT), (w_ffn_out, 0, 0)),
        hidden, n_s, dq, _TN_PROMPT)

    r_p = _pick_rows(seq)
    cos_p, sin_p = _rope_tables(jnp.arange(seq))
    q_p, k_p, v_p, u_p, h_p = _qkvu(x_prompt, mod_p, g_norm1, w_qkvu_bf, g_qnorm, g_knorm, cos_p, sin_p, 1, r_p)
    attn_p = _moba_prompt(q_p, k_p, v_p, n_p, seq)
    pooled_p = _pool(u_p, jnp.zeros((n_p, POOL_HALO, pool_w), F32), w_pool_bf, ls_pool,
                     n_p, seq, 1, r_p, 0)
    x1_p, _ = _merge(x_prompt, mod_p, h_p, attn_p, pooled_p,
                     tuple((w, axis, 0) for w, axis in zip(merge_bf, (1, 1, 1, 1, 0))), 1, r_p, _TN_PROMPT)
    y_p, _ = _ffn(x1_p, mod_p, g_norm2, tuple((w, axis, 0) for w, axis in zip(ffn_bf, (1, 1, 0))),
                  hidden, 1, r_p, _TN_PROMPT)

    k_prompt = k_p.reshape(n_p, seq, N_HEADS, HEAD_DIM)
    v_prompt = v_p.reshape(n_p, seq, N_HEADS, HEAD_DIM)
    pool_prompt = u_p.reshape(n_p, seq, pool_w)[:, seq - n_state:]
    k_sample = k_s.reshape(n_s, dq, N_HEADS, HEAD_DIM)
    v_sample = v_s.reshape(n_s, dq, N_HEADS, HEAD_DIM)
    pool_sample = jnp.concatenate([state_pool, u_s.reshape(n_s, dq, pool_w)], axis=1)[:, -n_state:]
    return (y_p, y_s, k_prompt, v_prompt, pool_prompt, k_sample, v_sample, pool_sample)
```

```python
import functools

import jax
import jax.numpy as jnp
from jax import lax
from jax.experimental import pallas as pl
from jax.experimental.pallas import tpu as pltpu

N_HEADS = 8
HEAD_DIM = 128
ATTN_WIDTH = N_HEADS * HEAD_DIM
MOBA_BLOCK = 256
MOBA_TOPK = 3
ROPE_THETA = 10000.0
POOL_WINDOWS = (2, 4, 8, 16)
POOL_HALO = 16
RMS_EPS = 1e-6
_TILE_GROUP = 4
_MASKED_MAX = -1e30
_TN_PROMPT = 512
_TN_SAMPLE = 256

F32 = jnp.float32
BF16 = jnp.bfloat16
NEG_INF = float("-inf")
_LOG2_E = 1.4426950408889634

_VMEM_LIMIT = 52 * 1024 * 1024


def _params(*sem):
    return pltpu.CompilerParams(dimension_semantics=sem, vmem_limit_bytes=_VMEM_LIMIT)


_NORM_ROWS = 16


def _mod_norm_store(x_ref, g_ref, sc_ref, sh_ref, h_scr):
    gb, r, d = x_ref.shape
    g = g_ref[...]
    if r >= _NORM_ROWS:
        per_group = r // _NORM_ROWS
        chunks = [(gi, 1, c * _NORM_ROWS, _NORM_ROWS) for gi in range(gb) for c in range(per_group)]
    else:
        n_g = _NORM_ROWS // r
        chunks = [(gi, n_g, 0, r) for gi in range(0, gb, n_g)]
    gains = {}
    for gi, n_g, r0, nr in chunks:
        if gi not in gains:
            gains[gi] = g * (1.0 + sc_ref[gi:gi + n_g])
        x = x_ref[gi:gi + n_g, r0:r0 + nr, :]
        ms = jnp.mean(x * x, axis=-1, keepdims=True)
        h = x * lax.rsqrt(ms + RMS_EPS) * gains[gi] + sh_ref[gi:gi + n_g]
        row0 = gi * r + r0
        h_scr[row0:row0 + n_g * nr, :] = h.reshape(n_g * nr, d).astype(BF16)


def _ada_kernel(c_ref, w_ref, b_ref, o_ref):
    o_ref[...] = jnp.dot(c_ref[...].astype(BF16), w_ref[...].astype(BF16),
                         preferred_element_type=F32) + b_ref[...]


def _ada(c_all, w_ada, b_ada, tn=1024):
    m, d = c_all.shape
    n = w_ada.shape[1]
    return pl.pallas_call(
        _ada_kernel,
        grid=(n // tn,),
        in_specs=[pl.BlockSpec((m, d), lambda j: (0, 0)),
                  pl.BlockSpec((d, tn), lambda j: (0, j)),
                  pl.BlockSpec((1, tn), lambda j: (0, j))],
        out_specs=pl.BlockSpec((m, tn), lambda j: (0, j)),
        out_shape=jax.ShapeDtypeStruct((m, n), F32),
        compiler_params=_params("arbitrary"),
        name="ada_mod",
    )(c_all, w_ada, b_ada.reshape(1, n))


def _head_norm_rope(th, g, cos, sin_signed):
    ms = jnp.mean(th * th, axis=-1, keepdims=True)
    y = th * lax.rsqrt(ms + RMS_EPS) * g
    return y * cos + pltpu.roll(y, HEAD_DIM // 2, 1) * sin_signed


_PROJ_CHUNK = 2 * HEAD_DIM


def _qkvu_kernel(x_ref, sh_ref, sc_ref, g1_ref, w_ref, gq_ref, gk_ref, cos_ref, sin_ref,
                 q_ref, k_ref, v_ref, u_ref, h_ref):
    _mod_norm_store(x_ref, g1_ref, sc_ref, sh_ref, h_ref)

    width = q_ref.shape[1]
    outs = ((q_ref, gq_ref), (k_ref, gk_ref), (v_ref, None), (u_ref, None))
    for i, (o_ref, g_ref) in enumerate(outs):
        for c in range(0, width, _PROJ_CHUNK):
            col = i * width + c
            acc = jnp.dot(h_ref[...], w_ref[:, col:col + _PROJ_CHUNK], preferred_element_type=F32)
            if g_ref is None:
                o_ref[:, c:c + _PROJ_CHUNK] = acc
            else:
                for hh in range(0, _PROJ_CHUNK, HEAD_DIM):
                    o_ref[:, c + hh:c + hh + HEAD_DIM] = _head_norm_rope(
                        acc[:, hh:hh + HEAD_DIM], g_ref[...], cos_ref[...], sin_ref[...])


def _group_specs(x3, gb, r):
    g_total, s, d = x3.shape
    n_g, n_t = g_total // gb, s // r
    return n_g, n_t, d


def _qkvu(x3, mod, g_norm1, w_in_bf, g_q, g_k, cos, sin_signed, gb, r):
    n_g, n_t, d = _group_specs(x3, gb, r)
    tm = gb * r
    tokens = x3.shape[0] * x3.shape[1]
    wq = ATTN_WIDTH
    x_spec = pl.BlockSpec((gb, r, d), lambda g, t: (g, t, 0))
    mod_spec = lambda k: pl.BlockSpec((gb, 1, d), lambda g, t: (g, 0, k))
    row_spec = pl.BlockSpec((1, d), lambda g, t: (0, 0))
    head_spec = pl.BlockSpec((1, HEAD_DIM), lambda g, t: (0, 0))
    tab_spec = pl.BlockSpec((tm, HEAD_DIM), lambda g, t: (t, 0))
    out_spec = pl.BlockSpec((tm, wq), lambda g, t: (g * n_t + t, 0))
    out_sds = jax.ShapeDtypeStruct((tokens, wq), F32)
    w_spec = pl.BlockSpec((d, 4 * wq), lambda g, t: (0, 0), pipeline_mode=pl.Buffered(1))
    return pl.pallas_call(
        _qkvu_kernel,
        grid=(n_g, n_t),
        in_specs=[x_spec, mod_spec(0), mod_spec(1), row_spec, w_spec,
                  head_spec, head_spec, tab_spec, tab_spec],
        out_specs=[out_spec] * 4 + [pl.BlockSpec((tm, d), lambda g, t: (g * n_t + t, 0))],
        out_shape=[out_sds] * 4 + [jax.ShapeDtypeStruct((tokens, d), BF16)],
        compiler_params=pltpu.CompilerParams(
            dimension_semantics=("arbitrary", "arbitrary"), vmem_limit_bytes=_VMEM_LIMIT,
            allow_input_fusion=[False] * 4 + [True] + [False] * 4),
        name="qkvu_proj",
    )(x3, mod, mod, g_norm1.reshape(1, d), w_in_bf, g_q.reshape(1, HEAD_DIM),
      g_k.reshape(1, HEAD_DIM), cos, sin_signed)


def _topk_select(gate, valid, axis):
    n = gate.shape[axis]
    g = jnp.where(valid, gate, NEG_INF)
    idx = lax.broadcasted_iota(jnp.int32, gate.shape, axis)
    rank = jnp.zeros(gate.shape, jnp.int32)
    for m in range(n):
        gm = g[m:m + 1, :] if axis == 0 else g[:, m:m + 1]
        beats = (gm > g) | ((gm == g) & (m < idx))
        rank = rank + beats.astype(jnp.int32)
    return valid & (rank < MOBA_TOPK) & (jnp.abs(g) < float("inf"))


def _moba_prompt_kernel(q_ref, k_ref, v_ref, o_ref, kb_scr, vt_scr, kmean_scr, bias_scr, tri_scr, s_scr,
                        *, nb):
    blk = MOBA_BLOCK

    def prepare():
        for n in range(nb):
            kn = k_ref[n * blk:(n + 1) * blk, :]
            kb_scr[n * blk:(n + 1) * blk, :] = kn.astype(BF16)
            kmean_scr[n:n + 1, :] = jnp.mean(kn, axis=0, keepdims=True)
            vt_scr[n] = v_ref[n * blk:(n + 1) * blk, :].T.astype(BF16)
        kb_scr[nb * blk:, :] = jnp.zeros((kb_scr.shape[0] - nb * blk, HEAD_DIM), BF16)
        kmean = kmean_scr[...]
        for qb_i in range(nb):
            gate = lax.dot_general(kmean, q_ref[qb_i * blk:(qb_i + 1) * blk, :],
                                   (((1,), (1,)), ((), ())),
                                   precision=lax.Precision.HIGHEST, preferred_element_type=F32)
            n_idx = lax.broadcasted_iota(jnp.int32, gate.shape, 0)
            sel = _topk_select(gate, n_idx < qb_i, axis=0)
            bias_scr[qb_i, 0:nb, :] = jnp.where(sel | (n_idx == qb_i), 0.0, NEG_INF)
            bias_scr[qb_i, nb:, :] = jnp.full((bias_scr.shape[1] - nb, blk), NEG_INF, F32)
        kpos = lax.broadcasted_iota(jnp.int32, (blk, blk), 0)
        qpos = lax.broadcasted_iota(jnp.int32, (blk, blk), 1)
        tri_scr[0] = jnp.zeros((blk, blk), F32)
        tri_scr[1] = jnp.where(kpos <= qpos, 0.0, NEG_INF)

    prepare()

    exp2_scale = HEAD_DIM ** -0.5 * _LOG2_E
    width = _TILE_GROUP

    def fold(x, op):
        return op(x.reshape(blk // 8, 8, blk), axis=0)

    def queries(qi):
        return q_ref[pl.ds(pl.multiple_of(qi * blk, blk), blk), :].astype(BF16)

    def score_group(qi, qb, g, slot):
        j0 = g * width
        keys = kb_scr[pl.ds(pl.multiple_of(j0 * blk, blk), width * blk), :]
        s_all = lax.dot_general(keys, qb, (((1,), (1,)), ((), ())), preferred_element_type=F32)
        m8 = jnp.full((8, blk), NEG_INF, F32)
        for t in range(width):
            j = j0 + t
            s = (s_all[t * blk:(t + 1) * blk, :] + bias_scr[qi, pl.ds(j, 1), :]
                 + tri_scr[jnp.asarray(j == qi, jnp.int32)])
            s_scr[slot, t] = s
            m8 = jnp.maximum(m8, fold(s, jnp.max))
        return m8

    def absorb_group(g, slot, m_run, l8, acc, m8):
        m_new = jnp.maximum(m_run, jnp.max(m8, axis=0, keepdims=True))
        alpha = jnp.exp2((m_run - m_new) * exp2_scale)
        l8 = l8 * alpha
        acc = acc * alpha
        for t in range(width):
            p = jnp.exp2((s_scr[slot, t] - m_new) * exp2_scale)
            l8 = l8 + fold(p, jnp.sum)
            acc = acc + jnp.dot(vt_scr[jnp.minimum(g * width + t, nb - 1)], p.astype(BF16),
                                preferred_element_type=F32)
        return m_new, l8, acc

    def one_query_block(qi, state):
        m8_first, first_slot = state
        qb = queries(qi)
        n_groups = (qi + width) // width

        def body(g, carry):
            m_run, l8, acc, m8 = carry
            m_run, l8, acc = absorb_group(g, (first_slot + g) % 2, m_run, l8, acc, m8)
            return m_run, l8, acc, score_group(qi, qb, g + 1, (first_slot + g + 1) % 2)

        carry = (jnp.full((1, blk), _MASKED_MAX, F32), jnp.zeros((8, blk), F32),
                 jnp.zeros((HEAD_DIM, blk), F32), m8_first)
        m_run, l8, acc, m8 = lax.fori_loop(0, n_groups - 1, body, carry)
        last_slot = (first_slot + n_groups - 1) % 2
        _, l8, acc = absorb_group(n_groups - 1, last_slot, m_run, l8, acc, m8)
        nxt = jnp.minimum(qi + 1, nb - 1)
        m8_next = score_group(nxt, queries(nxt), 0, 1 - last_slot)
        l = jnp.sum(l8, axis=0, keepdims=True)
        o_ref[pl.ds(pl.multiple_of(qi * blk, blk), blk), :] = (acc / l).T.astype(o_ref.dtype)
        return m8_next, 1 - last_slot

    lax.fori_loop(0, nb, one_query_block, (score_group(0, queries(0), 0, 0), jnp.int32(0)))


def _moba_prompt(q, k, v, n_seq, seq):
    nb = seq // MOBA_BLOCK
    nb_pad = -(-nb // _TILE_GROUP) * _TILE_GROUP + _TILE_GROUP
    blk = MOBA_BLOCK
    head_spec = pl.BlockSpec((seq, HEAD_DIM), lambda b, h: (b, h))
    return pl.pallas_call(
        functools.partial(_moba_prompt_kernel, nb=nb),
        grid=(n_seq, N_HEADS),
        in_specs=[head_spec, head_spec, head_spec],
        out_specs=head_spec,
        out_shape=jax.ShapeDtypeStruct(q.shape, BF16),
        scratch_shapes=[pltpu.VMEM((nb_pad * blk, HEAD_DIM), BF16),
                        pltpu.VMEM((nb, HEAD_DIM, blk), BF16),
                        pltpu.VMEM((nb, HEAD_DIM), F32),
                        pltpu.VMEM((nb, nb_pad, blk), F32),
                        pltpu.VMEM((2, blk, blk), F32),
                        pltpu.VMEM((2, _TILE_GROUP, blk, blk), F32)],
        compiler_params=_params("arbitrary", "arbitrary"),
        name="moba_prompt",
    )(q, k, v)


def _page_copy_groups(cache_hbm, pt_ref, buf, sem, slot, b, first_page, hp, n_pages, ppb, wanted):
    groups = []
    for p0 in range(0, hp, ppb):
        pgs = [pt_ref[b * n_pages + first_page + p0 + i] for i in range(ppb)]
        for h in range(N_HEADS):
            pred = None if wanted is None else wanted(b, (first_page + p0) // ppb, h)
            groups.append((pred, [pltpu.make_async_copy(cache_hbm.at[pgs[i], :, h, :],
                                                        buf.at[slot, p0 + i, h], sem.at[slot])
                                  for i in range(ppb)]))
    return groups


def _for_each_copy(groups, method):
    for pred, copies in groups:
        def run(copies=copies):
            for c in copies:
                getattr(c, method)()
        if pred is None:
            run()
        else:
            pl.when(pred)(run)


def _paged_slots(pt_ref, cache_hbm, buf, sem, hp, n_pages, ppb, compute, wanted=None):
    n_slots = buf.shape[0]
    b, step = pl.program_id(0), pl.program_id(1)
    n_b, n_steps = pl.num_programs(0), pl.num_programs(1)
    is_first = jnp.logical_and(b == 0, step == 0)
    is_last = jnp.logical_and(b == n_b - 1, step == n_steps - 1)
    wrap = step == n_steps - 1
    nxt_b = jnp.where(wrap, b + 1, b)
    nxt_step = jnp.where(wrap, 0, step + 1)

    def groups(bb, ss, slot):
        return _page_copy_groups(cache_hbm, pt_ref, buf, sem, slot, bb, (n_slots * ss + slot) * hp,
                                 hp, n_pages, ppb, wanted)

    @pl.when(is_first)
    def _():
        for slot in range(n_slots):
            _for_each_copy(groups(b, step, slot), "start")

    for slot in range(n_slots):
        _for_each_copy(groups(b, step, slot), "wait")
        compute(slot, (n_slots * step + slot) * hp)

        @pl.when(jnp.logical_not(is_last))
        def _():
            _for_each_copy(groups(nxt_b, nxt_step, slot), "start")


def _moba_sample_k_kernel(pt_ref, q_ref, knew_ref, vnew_ref, cache_hbm, p_ref, oown_ref, linv_ref,
                          used_ref, buf, sem, s_scr, kpart_scr, qbd_scr, *, hp, page, n_blocks):
    step = pl.program_id(1)
    n_steps = pl.num_programs(1)
    dq = q_ref.shape[0]
    rows = N_HEADS * dq
    scale = HEAD_DIM ** -0.5
    pages_per_block = MOBA_BLOCK // page
    n_pages = n_blocks * pages_per_block

    qh = [q_ref[:, h * HEAD_DIM:(h + 1) * HEAD_DIM] for h in range(N_HEADS)]
    qh_bf = [x.astype(BF16) for x in qh]

    @pl.when(step == 0)
    def _():
        kpart_scr[...] = jnp.zeros_like(kpart_scr)
        q_rep = jnp.concatenate([q_ref[...]] * N_HEADS
                                + [jnp.zeros((qbd_scr.shape[1] - rows, ATTN_WIDTH), F32)], axis=0)
        row_head = lax.broadcasted_iota(jnp.int32, q_rep.shape, 0) // dq
        lane_head = lax.broadcasted_iota(jnp.int32, q_rep.shape, 1) // HEAD_DIM
        qbd_scr[...] = jnp.where(row_head == lane_head, q_rep, 0.0).T.astype(BF16)

    def compute(slot, first_page):
        heads = [[buf[slot, p, h] for h in range(N_HEADS)] for p in range(hp)]
        for p in range(hp):
            blk = (first_page + p) // pages_per_block
            for h in range(N_HEADS):
                kpart_scr[blk, h] += jnp.sum(heads[p][h].reshape(page // 8, 8, HEAD_DIM), axis=0)
        for p0 in range(0, hp, hp // 2):
            keys = jnp.concatenate([jnp.concatenate(hs, axis=1) for hs in heads[p0:p0 + hp // 2]],
                                   axis=0)
            s_t = jnp.dot(keys.astype(BF16), qbd_scr[...], preferred_element_type=F32) * scale
            for i in range(hp // 2):
                s_scr[first_page + p0 + i] = s_t[i * page:(i + 1) * page, :].T[:rows, :]

    _paged_slots(pt_ref, cache_hbm, buf, sem, hp, n_pages, pages_per_block, compute)

    @pl.when(step == n_steps - 1)
    def _():
        gates = []
        for h in range(N_HEADS):
            kmean_h = jnp.sum(kpart_scr[:, h], axis=1) * (1.0 / MOBA_BLOCK)
            gates.append(lax.dot_general(qh[h], kmean_h, (((1,), (1,)), ((), ())),
                                         precision=lax.Precision.HIGHEST,
                                         preferred_element_type=F32))
        gate = jnp.concatenate(gates, axis=0)
        sel = _topk_select(gate, jnp.ones(gate.shape, jnp.bool_), axis=1)
        bias = jnp.where(sel, 0.0, NEG_INF)
        used_ref[0] = jnp.concatenate(
            [jnp.max(sel[h * dq:(h + 1) * dq, :].astype(jnp.int32), axis=0, keepdims=True)
             for h in range(N_HEADS)], axis=0)

        s_own = []
        for h in range(N_HEADS):
            kn = knew_ref[:, h * HEAD_DIM:(h + 1) * HEAD_DIM].astype(BF16)
            s_own.append(lax.dot_general(qh_bf[h], kn, (((1,), (1,)), ((), ())),
                                         preferred_element_type=F32) * scale)
        s_own = jnp.concatenate(s_own, axis=0)
        row = lax.broadcasted_iota(jnp.int32, s_own.shape, 0) % dq
        col = lax.broadcasted_iota(jnp.int32, s_own.shape, 1)
        s_own = jnp.where(col <= row, s_own, NEG_INF)

        m_vec = jnp.full((rows, page), NEG_INF, F32)
        for g in range(n_pages):
            n = g // pages_per_block
            m_vec = jnp.maximum(m_vec, s_scr[g] + bias[:, n:n + 1])
        m = jnp.maximum(jnp.max(m_vec, axis=1, keepdims=True), jnp.max(s_own, axis=1, keepdims=True))
        p_own = jnp.exp(s_own - m)
        l_vec = jnp.zeros((rows, page), F32)
        for g in range(n_pages):
            n = g // pages_per_block
            pb = jnp.exp(s_scr[g] + bias[:, n:n + 1] - m)
            l_vec = l_vec + pb
            p_ref[0, :, g * page:(g + 1) * page] = pb
        l = jnp.sum(l_vec, axis=1, keepdims=True) + jnp.sum(p_own, axis=1, keepdims=True)
        linv_ref[0] = jnp.broadcast_to(1.0 / l, (rows, HEAD_DIM))
        for h in range(N_HEADS):
            vn = vnew_ref[:, h * HEAD_DIM:(h + 1) * HEAD_DIM].astype(BF16)
            oown_ref[0, h * dq:(h + 1) * dq, :] = jnp.dot(
                p_own[h * dq:(h + 1) * dq, :].astype(BF16), vn, preferred_element_type=F32)


def _moba_sample_v_kernel(pt_ref, used_ref, p_ref, oown_ref, linv_ref, cache_hbm, o_ref, buf, sem,
                          acc_scr, *, hp, page, n_pages):
    step = pl.program_id(1)
    n_steps = pl.num_programs(1)
    dq = o_ref.shape[0]
    ppb = MOBA_BLOCK // page
    n_blocks = n_pages // ppb

    @pl.when(jnp.logical_and(pl.program_id(0) == 0, step == 0))
    def _():
        buf[...] = jnp.zeros_like(buf)

    def wanted(b, blk, h):
        return used_ref[(b * N_HEADS + h) * n_blocks + blk] != 0

    @pl.when(step == 0)
    def _():
        acc_scr[...] = oown_ref[0]

    def compute(slot, first_page):
        for h in range(N_HEADS):
            tot = jnp.zeros((dq, HEAD_DIM), F32)
            for p in range(hp):
                col0 = (slot * hp + p) * page
                ph = p_ref[0, h * dq:(h + 1) * dq, col0:col0 + page].astype(BF16)
                tot = tot + jnp.dot(ph, buf[slot, p, h].astype(BF16), preferred_element_type=F32)
            acc_scr[h * dq:(h + 1) * dq, :] += tot

    _paged_slots(pt_ref, cache_hbm, buf, sem, hp, n_pages, ppb, compute, wanted)

    @pl.when(step == n_steps - 1)
    def _():
        for h in range(N_HEADS):
            rows_h = slice(h * dq, (h + 1) * dq)
            o_ref[:, h * HEAD_DIM:(h + 1) * HEAD_DIM] = acc_scr[rows_h, :] * linv_ref[0, rows_h, :]


def _moba_sample(q, k_new, v_new, cache_k, cache_v, page_table, hp=4, n_slots=4):
    dec_b, n_pages = page_table.shape
    page = cache_k.shape[1]
    dq = q.shape[0] // dec_b
    past = n_pages * page
    n_blocks = past // MOBA_BLOCK
    while n_pages % (n_slots * hp):
        n_slots //= 2
    n_steps = n_pages // (n_slots * hp)
    rows = N_HEADS * dq
    pt = page_table.reshape(-1)

    new_spec = pl.BlockSpec((dq, ATTN_WIDTH), lambda b, s, pt_ref: (b, 0))
    any_spec = pl.BlockSpec(memory_space=pl.ANY)
    stat_spec = pl.BlockSpec((1, rows, HEAD_DIM), lambda b, s, pt_ref: (b, 0, 0))
    stat_sds = jax.ShapeDtypeStruct((dec_b, rows, HEAD_DIM), F32)
    assert hp % (MOBA_BLOCK // page) == 0, "a page half must hold whole key blocks"
    page_buf = pltpu.VMEM((n_slots, hp, N_HEADS, page, HEAD_DIM), F32)
    probs, o_own, l_inv, used = pl.pallas_call(
        functools.partial(_moba_sample_k_kernel, hp=hp, page=page, n_blocks=n_blocks),
        grid_spec=pltpu.PrefetchScalarGridSpec(
            num_scalar_prefetch=1,
            grid=(dec_b, n_steps),
            in_specs=[new_spec, new_spec, new_spec, any_spec],
            out_specs=[pl.BlockSpec((1, rows, past), lambda b, s, pt_ref: (b, 0, 0)),
                       stat_spec, stat_spec,
                       pl.BlockSpec((1, N_HEADS, n_blocks), lambda b, s, pt_ref: (b, 0, 0))],
            scratch_shapes=[page_buf, pltpu.SemaphoreType.DMA((n_slots,)),
                            pltpu.VMEM((n_pages, rows, page), F32),
                            pltpu.VMEM((n_blocks, N_HEADS, 8, HEAD_DIM), F32),
                            pltpu.VMEM((ATTN_WIDTH, max(rows, HEAD_DIM)), BF16)]),
        out_shape=[jax.ShapeDtypeStruct((dec_b, rows, past), F32), stat_sds, stat_sds,
                   jax.ShapeDtypeStruct((dec_b, N_HEADS, n_blocks), jnp.int32)],
        compiler_params=_params("arbitrary", "arbitrary"),
        name="moba_sample_k",
    )(pt, q, k_new, v_new, cache_k)

    v_hp = max(hp // 2, MOBA_BLOCK // page)
    v_slots = n_slots * hp // v_hp
    return pl.pallas_call(
        functools.partial(_moba_sample_v_kernel, hp=v_hp, page=page, n_pages=n_pages),
        grid_spec=pltpu.PrefetchScalarGridSpec(
            num_scalar_prefetch=2,
            grid=(dec_b, n_steps),
            in_specs=[pl.BlockSpec((1, rows, n_slots * hp * page), lambda b, s, *_: (b, 0, s)),
                      pl.BlockSpec((1, rows, HEAD_DIM), lambda b, s, *_: (b, 0, 0)),
                      pl.BlockSpec((1, rows, HEAD_DIM), lambda b, s, *_: (b, 0, 0)),
                      any_spec],
            out_specs=pl.BlockSpec((dq, ATTN_WIDTH), lambda b, s, *_: (b, 0)),
            scratch_shapes=[pltpu.VMEM((v_slots, v_hp, N_HEADS, page, HEAD_DIM), F32),
                            pltpu.SemaphoreType.DMA((v_slots,)),
                            pltpu.VMEM((rows, HEAD_DIM), F32)]),
        out_shape=jax.ShapeDtypeStruct(q.shape, F32),
        compiler_params=_params("arbitrary", "arbitrary"),
        name="moba_sample_v",
    )(pt, used.reshape(-1), probs, o_own, l_inv, cache_v)


def _pool_kernel(u_ref, halo_ref, w_ref, ls_ref, o_ref, ext_scr, sum_a, sum_b, *, pos0):
    t = pl.program_id(1)
    gb, r, width = u_ref.shape
    grp = width // len(POOL_WINDOWS)
    h = POOL_HALO
    assert all(w == 2 << k for k, w in enumerate(POOL_WINDOWS)) and POOL_WINDOWS[-1] <= h

    @pl.when(t == 0)
    def _():
        for scr in (ext_scr, sum_a, sum_b):
            scr[:, 0:h, :] = jnp.zeros((gb, h, width), F32)
        ext_scr[:, h:2 * h, :] = halo_ref[...]

    @pl.when(t > 0)
    def _():
        ext_scr[:, h:2 * h, :] = ext_scr[:, r + h:r + 2 * h, :]

    ext_scr[:, 2 * h:2 * h + r, :] = u_ref[...]

    pos = pos0 + t * r + lax.broadcasted_iota(jnp.int32, (1, r, grp), 1)
    src, dst = ext_scr, sum_a
    for g, w in enumerate(POOL_WINDOWS):
        d = w // 2
        live = slice(g * grp, width)
        dst[:, h:2 * h + r, live] = (src[:, h:2 * h + r, live] + src[:, h - d:2 * h + r - d, live])
        cols = slice(g * grp, (g + 1) * grp)
        tot = dst[:, 2 * h:2 * h + r, cols]
        cnt = jnp.minimum(w, pos + 1).astype(F32)
        y = tot / cnt - u_ref[:, :, cols]
        y = jnp.dot(y.reshape(gb * r, grp).astype(BF16), w_ref[g], preferred_element_type=F32)
        o_ref[:, cols] = (y * ls_ref[:, cols]).astype(o_ref.dtype)
        src, dst = dst, (sum_b if dst is sum_a else sum_a)


def _pool(u, halo, w_pool_bf, ls_pool, n_seq, seq, gb, r, pos0):
    width = u.shape[1]
    n_g, n_t = n_seq // gb, seq // r
    u3 = u.reshape(n_seq, seq, width)
    return pl.pallas_call(
        functools.partial(_pool_kernel, pos0=pos0),
        grid=(n_g, n_t),
        in_specs=[pl.BlockSpec((gb, r, width), lambda g, t: (g, t, 0)),
                  pl.BlockSpec((gb, POOL_HALO, width), lambda g, t: (g, 0, 0)),
                  pl.BlockSpec(w_pool_bf.shape, lambda g, t: (0, 0, 0)),
                  pl.BlockSpec((1, width), lambda g, t: (0, 0))],
        out_specs=pl.BlockSpec((gb * r, width), lambda g, t: (g * n_t + t, 0)),
        out_shape=jax.ShapeDtypeStruct(u.shape, BF16),
        scratch_shapes=[pltpu.VMEM((gb, r + 2 * POOL_HALO, width), F32)] * 3,
        compiler_params=_params("arbitrary", "arbitrary"),
        name="pool_mixer",
    )(u3, halo, w_pool_bf, ls_pool.reshape(1, width))


def _bf16_tiles(w_refs, copy_refs):
    tiles = []
    for i, w_ref in enumerate(w_refs):
        w = w_ref[...]
        if w.dtype != BF16:
            w = w.astype(BF16)
            copy_refs[i][...] = w
        tiles.append(w)
    return tiles


def _merge_kernel(x_ref, gt_ref, h_ref, attn_ref, pool_ref,
                  wga_ref, wgb_ref, woa_ref, wob_ref, wout_ref, o_ref, *copy_refs):
    c = pl.program_id(2)

    @pl.when(c == 0)
    def _():
        o_ref[...] = jnp.zeros_like(o_ref)

    wga, wgb, woa, wob, wout = _bf16_tiles((wga_ref, wgb_ref, woa_ref, wob_ref, wout_ref), copy_refs)
    h = h_ref[...]
    g_a = jax.nn.sigmoid(jnp.dot(h, wga, preferred_element_type=F32))
    g_b = jax.nn.sigmoid(jnp.dot(h, wgb, preferred_element_type=F32))
    y_a = jnp.dot(attn_ref[...].astype(BF16), woa, preferred_element_type=F32)
    y_b = jnp.dot(pool_ref[...].astype(BF16), wob, preferred_element_type=F32)
    mix_in = (g_a * y_a + g_b * y_b).astype(BF16)
    o_ref[...] += jnp.dot(mix_in, wout, preferred_element_type=F32).reshape(o_ref.shape)

    @pl.when(c == pl.num_programs(2) - 1)
    def _():
        o_ref[...] = x_ref[...] + gt_ref[...] * o_ref[...]


def _column_weight_specs(weights, tn, n_c):
    in_specs, copy_specs, copy_shapes = [], [], []
    for w, axis, first in weights:
        if axis == 1:
            in_specs.append(pl.BlockSpec((w.shape[0], tn), lambda g, t, c, first=first: (0, first + c)))
            copy_specs.append(pl.BlockSpec((w.shape[0], tn), lambda g, t, c: (0, c)))
            copy_shapes.append(jax.ShapeDtypeStruct((w.shape[0], n_c * tn), BF16))
        else:
            in_specs.append(pl.BlockSpec((tn, w.shape[1]), lambda g, t, c, first=first: (first + c, 0)))
            copy_specs.append(pl.BlockSpec((tn, w.shape[1]), lambda g, t, c: (c, 0)))
            copy_shapes.append(jax.ShapeDtypeStruct((n_c * tn, w.shape[1]), BF16))
    if all(w.dtype == BF16 for w, _, _ in weights):
        return in_specs, [], []
    return in_specs, copy_specs, copy_shapes


def _merge(x3, mod, h, attn, pooled, weights, gb, r, tn):
    n_g, n_t, d = _group_specs(x3, gb, r)
    tm = gb * r
    aw = attn.shape[1]
    pw = pooled.shape[1]
    n_c = d // tn
    w_specs, copy_specs, copy_shapes = _column_weight_specs(weights, tn, n_c)
    if copy_specs:
        assert n_g * n_t == 1, "weight copies are written by a single token tile"
    x_spec = pl.BlockSpec((gb, r, d), lambda g, t, c: (g, t, 0))
    mod_spec = lambda k: pl.BlockSpec((gb, 1, d), lambda g, t, c: (g, 0, k))
    out, *copies = pl.pallas_call(
        _merge_kernel,
        grid=(n_g, n_t, n_c),
        in_specs=[x_spec, mod_spec(2),
                  pl.BlockSpec((tm, d), lambda g, t, c: (g * n_t + t, 0)),
                  pl.BlockSpec((tm, aw), lambda g, t, c: (g * n_t + t, 0)),
                  pl.BlockSpec((tm, pw), lambda g, t, c: (g * n_t + t, 0))] + w_specs,
        out_specs=[x_spec] + copy_specs,
        out_shape=[jax.ShapeDtypeStruct(x3.shape, F32)] + copy_shapes,
        compiler_params=_params("arbitrary", "arbitrary", "arbitrary"),
        name="merge_out",
    )(x3, mod, h, attn, pooled, *[w for w, _, _ in weights])
    return out, copies


def _ffn_kernel(x_ref, sh_ref, sc_ref, gt_ref, g2_ref, wa_ref, wb_ref, wo_ref, o_ref, *rest):
    *copy_refs, h_scr = rest
    c = pl.program_id(2)

    @pl.when(c == 0)
    def _():
        _mod_norm_store(x_ref, g2_ref, sc_ref, sh_ref, h_scr)
        o_ref[...] = jnp.zeros_like(o_ref)

    wa, wb, wo = _bf16_tiles((wa_ref, wb_ref, wo_ref), copy_refs)
    h = h_scr[...]
    a = jnp.dot(h, wa, preferred_element_type=F32)
    b = jnp.dot(h, wb, preferred_element_type=F32)
    hid = (jax.nn.silu(a) * b).astype(BF16)
    o_ref[...] += jnp.dot(hid, wo, preferred_element_type=F32).reshape(o_ref.shape)

    @pl.when(c == pl.num_programs(2) - 1)
    def _():
        o_ref[...] = x_ref[...] + gt_ref[...] * o_ref[...]


def _ffn(x3, mod, g_norm2, weights, hidden, gb, r, tn):
    n_g, n_t, d = _group_specs(x3, gb, r)
    tm = gb * r
    n_c = hidden // tn
    w_specs, copy_specs, copy_shapes = _column_weight_specs(weights, tn, n_c)
    if copy_specs:
        assert n_g * n_t == 1, "weight copies are written by a single token tile"
    x_spec = pl.BlockSpec((gb, r, d), lambda g, t, c: (g, t, 0))
    mod_spec = lambda k: pl.BlockSpec((gb, 1, d), lambda g, t, c: (g, 0, k))
    out, *copies = pl.pallas_call(
        _ffn_kernel,
        grid=(n_g, n_t, n_c),
        in_specs=[x_spec, mod_spec(3), mod_spec(4), mod_spec(5),
                  pl.BlockSpec((1, d), lambda g, t, c: (0, 0))] + w_specs,
        out_specs=[x_spec] + copy_specs,
        out_shape=[jax.ShapeDtypeStruct(x3.shape, F32)] + copy_shapes,
        scratch_shapes=[pltpu.VMEM((tm, d), BF16)],
        compiler_params=_params("arbitrary", "arbitrary", "arbitrary"),
        name="ffn_swiglu",
    )(x3, mod, mod, mod, g_norm2.reshape(1, d), *[w for w, _, _ in weights])
    return out, copies


def _rope_tables(pos):
    half = HEAD_DIM // 2
    inv = ROPE_THETA ** (-jnp.arange(half, dtype=F32) / half)
    ang = pos.astype(F32)[:, None] * inv[None, :]
    cos, sin = jnp.cos(ang), jnp.sin(ang)
    return jnp.concatenate([cos, cos], axis=-1), jnp.concatenate([-sin, sin], axis=-1)


def _pick_rows(seq, target=512):
    r = min(seq, target)
    while seq % r:
        r //= 2
    return r


def kernel(x_prompt, x_sample, c_prompt, c_sample, cache_k, cache_v, state_pool, page_table, w_ada, b_ada, g_norm1, g_norm2, w_in, g_qnorm, g_knorm, w_pool, ls_pool, w_o_attn, w_o_pool, w_out, w_ffn_in, w_ffn_out):
    n_p, seq, d = x_prompt.shape
    n_s, dq, _ = x_sample.shape
    past = page_table.shape[1] * cache_k.shape[1]
    pool_w = state_pool.shape[2]
    n_state = state_pool.shape[1]

    w_qkvu_bf = w_in[:, :3 * ATTN_WIDTH + pool_w].astype(BF16)
    w_pool_bf = w_pool.astype(BF16)
    hidden = w_ffn_out.shape[0]

    n_c = n_p + n_s
    c_all = jnp.concatenate([c_prompt, c_sample], axis=0)
    c_all = jnp.pad(c_all, ((0, (-n_c) % 8), (0, 0)))
    mod = _ada(c_all, w_ada, b_ada)
    mod_p = mod[:n_p].reshape(n_p, 1, 6 * d)
    mod_s = mod[n_p:n_c].reshape(n_s, 1, 6 * d)

    cos_s, sin_s = _rope_tables(past + jnp.arange(dq))
    cos_s, sin_s = jnp.tile(cos_s, (n_s, 1)), jnp.tile(sin_s, (n_s, 1))
    q_s, k_s, v_s, u_s, h_s = _qkvu(x_sample, mod_s, g_norm1, w_qkvu_bf, g_qnorm, g_knorm, cos_s, sin_s, n_s, dq)
    attn_s = _moba_sample(q_s, k_s, v_s, cache_k, cache_v, page_table)
    halo_s = jnp.pad(state_pool, ((0, 0), (POOL_HALO - n_state, 0), (0, 0)))
    pooled_s = _pool(u_s, halo_s, w_pool_bf, ls_pool, n_s, dq, n_s, dq, past)
    gate0 = (w_in.shape[1] - 2 * d) // _TN_SAMPLE
    x1_s, merge_bf = _merge(
        x_sample, mod_s, h_s, attn_s, pooled_s,
        ((w_in, 1, gate0), (w_in, 1, gate0 + d // _TN_SAMPLE), (w_o_attn, 1, 0), (w_o_pool, 1, 0),
         (w_out, 0, 0)), n_s, dq, _TN_SAMPLE)
    y_s, ffn_bf = _ffn(
        x1_s, mod_s, g_norm2,
        ((w_ffn_in, 1, 0), (w_ffn_in, 1, hidden // _TN_PROMPT), (w_ffn_out, 0, 0)),
        hidden, n_s, dq, _TN_PROMPT)

    r_p = _pick_rows(seq)
    cos_p, sin_p = _rope_tables(jnp.arange(seq))
    q_p, k_p, v_p, u_p, h_p = _qkvu(x_prompt, mod_p, g_norm1, w_qkvu_bf, g_qnorm, g_knorm, cos_p, sin_p, 1, r_p)
    attn_p = _moba_prompt(q_p, k_p, v_p, n_p, seq)
    pooled_p = _pool(u_p, jnp.zeros((n_p, POOL_HALO, pool_w), F32), w_pool_bf, ls_pool,
                     n_p, seq, 1, r_p, 0)
    x1_p, _ = _merge(x_prompt, mod_p, h_p, attn_p, pooled_p,
                     tuple((w, axis, 0) for w, axis in zip(merge_bf, (1, 1, 1, 1, 0))), 1, r_p, _TN_PROMPT)
    y_p, _ = _ffn(x1_p, mod_p, g_norm2, tuple((w, axis, 0) for w, axis in zip(ffn_bf, (1, 1, 0))),
                  hidden, 1, r_p, _TN_PROMPT)

    k_prompt = k_p.reshape(n_p, seq, N_HEADS, HEAD_DIM)
    v_prompt = v_p.reshape(n_p, seq, N_HEADS, HEAD_DIM)
    pool_prompt = u_p.reshape(n_p, seq, pool_w)[:, seq - n_state:]
    k_sample = k_s.reshape(n_s, dq, N_HEADS, HEAD_DIM)
    v_sample = v_s.reshape(n_s, dq, N_HEADS, HEAD_DIM)
    pool_sample = jnp.concatenate([state_pool, u_s.reshape(n_s, dq, pool_w)], axis=1)[:, -n_state:]
    return (y_p, y_s, k_prompt, v_prompt, pool_prompt, k_sample, v_sample, pool_sample)
```
